```python
import jax, jax.numpy as jnp
from jax import lax
import numpy as np

D_MODEL = 1024
BATCH = 32
SEQ = 2048
DEPTH = 2

HEAD_DIM = 64
Q_BLOCK = 128
EPS = 1e-6
A_HEADS = 8
A_W = A_HEADS * HEAD_DIM
B_HEADS = 8
B_W = B_HEADS * HEAD_DIM
B_Q_RANK = 256
B_KV_RANK = 128
IDX_HEADS = 8
IDX_DIM = 32
IDX_TOPK_MAX = 256
C_GROUPS = ((128, 1), (512, 4), (2048, 16))
C_HEADS_PER_GROUP = 4
C_HEADS = C_HEADS_PER_GROUP * len(C_GROUPS)
C_W = C_HEADS * HEAD_DIM
IN_WIDTHS = (A_W, A_W, A_W, A_HEADS, A_W,
             B_Q_RANK, B_KV_RANK, IDX_DIM, IDX_HEADS, B_W,
             C_W, C_W, C_W, C_W,
             D_MODEL, D_MODEL, D_MODEL)
D_IN = sum(IN_WIDTHS)

kernel_name = "hybrid_fox_dsa_dilated_gated"

F32 = jnp.float32


def rmsnorm(x, g):
    xf = x.astype(F32)
    y = xf * lax.rsqrt(jnp.mean(xf * xf, axis=-1, keepdims=True) + EPS)
    return (y * g.astype(F32)).astype(x.dtype)


def alibi_slopes(n):
    return jnp.exp2(-8.0 * (jnp.arange(n, dtype=F32) + 1.0) / n)


def _unblock(y, batch, seq):
    return jnp.swapaxes(y, 0, 1).reshape((batch, seq) + y.shape[3:])


def forgetting_attention(q, k, v, log_f):
    B, S, H, Dh = q.shape
    scale = Dh ** -0.5
    c = jnp.cumsum(log_f, axis=1).transpose(0, 2, 1)
    kpos = jnp.arange(S)

    def block(i):
        t0 = i * Q_BLOCK
        qb = lax.dynamic_slice_in_dim(q, t0, Q_BLOCK, 1)
        cb = lax.dynamic_slice_in_dim(c, t0, Q_BLOCK, 2)
        qpos = t0 + jnp.arange(Q_BLOCK)
        logits = jnp.einsum('bqhd,bkhd->bhqk', qb, k, preferred_element_type=F32) * scale
        logits = logits + (cb[..., :, None] - c[..., None, :])
        logits = jnp.where(kpos[None, :] <= qpos[:, None], logits, -jnp.inf)
        p = jax.nn.softmax(logits, axis=-1)
        return jnp.einsum('bhqk,bkhd->bqhd', p.astype(v.dtype), v)

    out = lax.map(block, jnp.arange(S // Q_BLOCK))
    return _unblock(out, B, S)


def dsa_attention(q, k, v, q_idx, k_idx, w_idx, slopes):
    B, S, H, Dh = q.shape
    top_k = min(IDX_TOPK_MAX, S // 4)
    scale = Dh ** -0.5
    kpos = jnp.arange(S)
    bidx = jnp.arange(B)[:, None, None]

    def block(i):
        t0 = i * Q_BLOCK
        qpos = t0 + jnp.arange(Q_BLOCK)
        qib = lax.dynamic_slice_in_dim(q_idx, t0, Q_BLOCK, 1)
        wb = lax.dynamic_slice_in_dim(w_idx, t0, Q_BLOCK, 1)
        dots = jnp.einsum('bqhd,bkd->bqhk', qib, k_idx, preferred_element_type=F32)
        score = jnp.einsum('bqh,bqhk->bqk', wb.astype(F32), jax.nn.relu(dots))
        score = jnp.where(kpos[None, :] <= qpos[:, None], score, -jnp.inf)
        _, sel = lax.top_k(score, top_k)
        ks = k[bidx, sel]
        vs = v[bidx, sel]
        qb = lax.dynamic_slice_in_dim(q, t0, Q_BLOCK, 1)
        logits = jnp.einsum('bqhd,bqkd->bhqk', qb, ks, preferred_element_type=F32) * scale
        dist = (qpos[None, :, None] - sel).astype(F32)
        logits = logits - slopes[None, :, None, None] * dist[:, None]
        valid = sel <= qpos[None, :, None]
        logits = jnp.where(valid[:, None], logits, -jnp.inf)
        p = jax.nn.softmax(logits, axis=-1)
        return jnp.einsum('bhqk,bqkd->bqhd', p.astype(vs.dtype), vs)

    out = lax.map(block, jnp.arange(S // Q_BLOCK))
    return _unblock(out, B, S)


def _dilated_group(q, k, v, slopes, window, dilation):
    B, S, H, Dh = q.shape
    scale = Dh ** -0.5
    n_keys = window // dilation + 1
    kp = jnp.pad(k, ((0, 0), (window, 0), (0, 0), (0, 0)))
    vp = jnp.pad(v, ((0, 0), (window, 0), (0, 0), (0, 0)))
    j = jnp.arange(n_keys)
    rel = jnp.arange(Q_BLOCK)[:, None] + window - j[None, :] * dilation
    dist = (j * dilation).astype(F32)

    def block(i):
        t0 = i * Q_BLOCK
        qpos = t0 + jnp.arange(Q_BLOCK)
        qb = lax.dynamic_slice_in_dim(q, t0, Q_BLOCK, 1)
        kband = lax.dynamic_slice_in_dim(kp, t0, Q_BLOCK + window, 1)
        vband = lax.dynamic_slice_in_dim(vp, t0, Q_BLOCK + window, 1)
        ks = kband[:, rel]
        vs = vband[:, rel]
        logits = jnp.einsum('bqhd,bqjhd->bhqj', qb, ks, preferred_element_type=F32) * scale
        logits = logits - slopes[None, :, None, None] * dist[None, None, None, :]
        valid = (qpos[:, None] - j[None, :] * dilation) >= 0
        logits = jnp.where(valid, logits, -jnp.inf)
        lse = jax.nn.logsumexp(logits, axis=-1)
        p = jnp.exp(logits - lse[..., None])
        o = jnp.einsum('bhqj,bqjhd->bqhd', p.astype(vs.dtype), vs)
        return o, lse.transpose(0, 2, 1)

    o, lse = lax.map(block, jnp.arange(S // Q_BLOCK))
    return _unblock(o, B, S), _unblock(lse, B, S)


def dilated_attention(q, k, v, slopes):
    hg = C_HEADS_PER_GROUP
    outs, lses = [], []
    for g, (window, dilation) in enumerate(C_GROUPS):
        hs = slice(g * hg, (g + 1) * hg)
        o, lse = _dilated_group(q[:, :, hs], k[:, :, hs], v[:, :, hs], slopes[hs], window, dilation)
        outs.append(o)
        lses.append(lse)
    alpha = jax.nn.softmax(jnp.stack(lses, axis=0), axis=0)
    return jnp.concatenate([(alpha[g][..., None] * outs[g]).astype(q.dtype) for g in range(len(C_GROUPS))], axis=2)


def hybrid_layer(x, norm_g, w_in, b_forget, g_cq, w_uq, w_uq_idx, g_ckv, w_ukv, w_a, w_b, w_c, w_o):
    B, S, _ = x.shape
    hd = HEAD_DIM
    h = rmsnorm(x, norm_g)
    proj = h @ w_in
    points = np.cumsum(IN_WIDTHS)[:-1].tolist()
    (qa, ka, va, fa, za, cq, ckv, kidx, widx, zb, qc, kc, vc, zc, ga, gb, gc) = jnp.split(proj, points, axis=-1)

    log_f = jax.nn.log_sigmoid((fa + b_forget).astype(F32))
    ya = forgetting_attention(qa.reshape(B, S, A_HEADS, hd), ka.reshape(B, S, A_HEADS, hd),
                              va.reshape(B, S, A_HEADS, hd), log_f)
    ya = ya.reshape(B, S, A_W) * jax.nn.silu(za)

    cq = rmsnorm(cq, g_cq)
    qb = (cq @ w_uq).reshape(B, S, B_HEADS, hd)
    q_idx = (cq @ w_uq_idx).reshape(B, S, IDX_HEADS, IDX_DIM)
    kv = rmsnorm(ckv, g_ckv) @ w_ukv
    yb = dsa_attention(qb, kv[..., :hd], kv[..., hd:], q_idx, kidx, widx, alibi_slopes(B_HEADS))
    yb = yb.reshape(B, S, B_W) * jax.nn.silu(zb)

    yc = dilated_attention(qc.reshape(B, S, C_HEADS, hd), kc.reshape(B, S, C_HEADS, hd),
                           vc.reshape(B, S, C_HEADS, hd), alibi_slopes(C_HEADS))
    yc = yc.reshape(B, S, C_W) * jax.nn.silu(zc)

    merged = (jax.nn.sigmoid(ga) * (ya @ w_a) + jax.nn.sigmoid(gb) * (yb @ w_b)
              + jax.nn.sigmoid(gc) * (yc @ w_c))
    return (x + merged @ w_o).astype(x.dtype)


def setup_inputs(seed: int = 0) -> dict:
    key = jax.random.key(seed)
    ks = jax.random.split(key, 16)
    nrm = lambda k, shape, fan_in: jax.random.normal(k, shape, F32) * (fan_in ** -0.5)
    return {
        "x": jax.random.normal(ks[0], (BATCH, SEQ, D_MODEL), F32),
        "norm_g": 1.0 + 0.05 * jax.random.normal(ks[1], (DEPTH, D_MODEL), F32),
        "w_in": nrm(ks[2], (DEPTH, D_MODEL, D_IN), D_MODEL),
        "b_forget": 1.0 + 0.5 * jax.random.normal(ks[3], (DEPTH, A_HEADS), F32),
        "g_cq": 1.0 + 0.05 * jax.random.normal(ks[4], (DEPTH, B_Q_RANK), F32),
        "w_uq": nrm(ks[5], (DEPTH, B_Q_RANK, B_W), B_Q_RANK),
        "w_uq_idx": nrm(ks[6], (DEPTH, B_Q_RANK, IDX_HEADS * IDX_DIM), B_Q_RANK),
        "g_ckv": 1.0 + 0.05 * jax.random.normal(ks[7], (DEPTH, B_KV_RANK), F32),
        "w_ukv": nrm(ks[8], (DEPTH, B_KV_RANK, 2 * HEAD_DIM), B_KV_RANK),
        "w_a": nrm(ks[9], (DEPTH, A_W, D_MODEL), A_W),
        "w_b": nrm(ks[10], (DEPTH, B_W, D_MODEL), B_W),
        "w_c": nrm(ks[11], (DEPTH, C_W, D_MODEL), C_W),
        "w_o": nrm(ks[12], (DEPTH, D_MODEL, D_MODEL), D_MODEL),
        "final_g": 1.0 + 0.05 * jax.random.normal(ks[13], (D_MODEL,), F32),
    }


def reference(x, norm_g, w_in, b_forget, g_cq, w_uq, w_uq_idx, g_ckv, w_ukv, w_a, w_b, w_c, w_o, final_g):
    for l in range(DEPTH):
        x = hybrid_layer(x, norm_g[l], w_in[l], b_forget[l], g_cq[l], w_uq[l], w_uq_idx[l],
                         g_ckv[l], w_ukv[l], w_a[l], w_b[l], w_c[l], w_o[l])
    return rmsnorm(x, final_g)
```

```python
import functools

import numpy as np
import jax
import jax.numpy as jnp
from jax import lax
from jax.experimental import pallas as pl
from jax.experimental.pallas import tpu as pltpu

F32 = jnp.float32
BF16 = jnp.bfloat16
I32 = jnp.int32

D_MODEL = 1024
HEAD_DIM = 64
EPS = 1e-6
A_HEADS = 8
A_W = A_HEADS * HEAD_DIM
B_HEADS = 8
B_W = B_HEADS * HEAD_DIM
B_Q_RANK = 256
B_KV_RANK = 128
IDX_HEADS = 8
IDX_DIM = 32
IDX_TOPK_MAX = 256
C_GROUPS = ((128, 1), (512, 4), (2048, 16))
C_HEADS_PER_GROUP = 4
C_HEADS = C_HEADS_PER_GROUP * len(C_GROUPS)
C_W = C_HEADS * HEAD_DIM
IN_WIDTHS = (A_W, A_W, A_W, A_HEADS, A_W,
             B_Q_RANK, B_KV_RANK, IDX_DIM, IDX_HEADS, B_W,
             C_W, C_W, C_W, C_W,
             D_MODEL, D_MODEL, D_MODEL)

LANES = 128
SCALE = HEAD_DIM ** -0.5
NEG = -1e30
INT_MIN = -2 ** 31
CLASS_WINDOW = 128
assert all(w // d == CLASS_WINDOW for w, d in C_GROUPS)

MAIN_G = 0
MAIN_ZB = 3 * D_MODEL
MAIN_ZC = MAIN_ZB + B_W
MAIN_QA = MAIN_ZC + C_W
MAIN_KA = MAIN_QA + A_W
MAIN_VA = MAIN_KA + A_W
MAIN_ZA = MAIN_VA + A_W
MAIN_N = MAIN_ZA + A_W
SIDE_CQ = 0
SIDE_MISC = B_Q_RANK + B_KV_RANK
SIDE_KIDX = 0
SIDE_WIDX = IDX_DIM
SIDE_FA = 512
SIDE_QC = SIDE_FA + LANES
SIDE_KC = SIDE_QC + C_W
SIDE_VC = SIDE_KC + C_W
SIDE_N = 3072

VMEM_LIMIT = 56 * 1024 * 1024


def _cparams(n_axes, vmem=VMEM_LIMIT):
    return pltpu.CompilerParams(dimension_semantics=("arbitrary",) * n_axes,
                                vmem_limit_bytes=vmem)


def _dot(a, b):
    return jnp.dot(a, b, preferred_element_type=F32)


def _dot_nt(a, b):
    return lax.dot_general(a, b, (((1,), (1,)), ((), ())), preferred_element_type=F32)


def _sigmoid(x):
    return 1.0 / (1.0 + jnp.exp(-x))


def _silu(x):
    return x * _sigmoid(x)


def _rms(x, g):
    return x * lax.rsqrt(jnp.mean(x * x, axis=-1, keepdims=True) + EPS) * g


def _rms_matmul_kernel(x_ref, g_ref, w_ref, o_ref, h_ref):
    @pl.when(pl.program_id(1) == 0)
    def _():
        h_ref[...] = _rms(x_ref[...], g_ref[...]).astype(BF16)

    o_ref[...] = _dot(h_ref[...], w_ref[...]).astype(o_ref.dtype)


def _rms_matmul(x2d, g, w, out_dtype, tm, tn):
    m, k = x2d.shape
    n = w.shape[1]
    return pl.pallas_call(
        _rms_matmul_kernel,
        grid=(m // tm, n // tn),
        in_specs=[pl.BlockSpec((tm, k), lambda i, j: (i, 0)),
                  pl.BlockSpec((1, k), lambda i, j: (0, 0)),
                  pl.BlockSpec((k, tn), lambda i, j: (0, j))],
        out_specs=pl.BlockSpec((tm, tn), lambda i, j: (i, j)),
        out_shape=jax.ShapeDtypeStruct((m, n), out_dtype),
        scratch_shapes=[pltpu.VMEM((tm, k), BF16)],
        compiler_params=_cparams(2),
        name="rms_in_proj",
    )(x2d, g, w)


def _cum_kernel(fa_ref, b_ref, tri_ref, ccol_ref, crow_ref):
    seq = fa_ref.shape[1]
    tri = tri_ref[...]
    carry = jnp.zeros((1, LANES), F32)
    for blk in range(seq // LANES):
        rows = slice(blk * LANES, (blk + 1) * LANES)
        x = fa_ref[0, rows, :] + b_ref[...]
        lf = jnp.minimum(x, 0.0) - jnp.log(1.0 + jnp.exp(-jnp.abs(x)))
        hi = lf.astype(BF16)
        r1 = lf - hi.astype(F32)
        mid = r1.astype(BF16)
        lo = (r1 - mid.astype(F32)).astype(BF16)
        c = _dot(tri, hi) + _dot(tri, mid) + _dot(tri, lo) + carry
        ccol_ref[0, rows, :] = c
        carry = c[LANES - 1:LANES, :]
        ct = c.T
        for h in range(A_HEADS):
            crow_ref[0, h, :, rows] = ct[h:h + 1, :]


def _cum_forget(side, b_pad, tri, batch, seq):
    return pl.pallas_call(
        _cum_kernel,
        grid=(batch,),
        in_specs=[pl.BlockSpec((1, seq, LANES), lambda b: (b, 0, SIDE_FA // LANES)),
                  pl.BlockSpec((1, LANES), lambda b: (0, 0)),
                  pl.BlockSpec((LANES, LANES), lambda b: (0, 0))],
        out_specs=[pl.BlockSpec((1, seq, LANES), lambda b: (b, 0, 0)),
                   pl.BlockSpec((1, A_HEADS, 1, seq), lambda b: (b, 0, 0, 0))],
        out_shape=[jax.ShapeDtypeStruct((batch, seq, LANES), F32),
                   jax.ShapeDtypeStruct((batch, A_HEADS, 1, seq), F32)],
        compiler_params=_cparams(1),
        name="fox_cumsum",
    )(side, b_pad, tri)


def _flash_update(carry, s, v):
    m, l, acc = carry
    m_new = jnp.maximum(m, jnp.max(s, axis=-1, keepdims=True))
    alpha = jnp.exp(m - m_new)
    p = jnp.exp(s - m_new)
    l = alpha * l + jnp.sum(p, axis=-1, keepdims=True)
    acc = alpha * acc + _dot(p.astype(BF16), v)
    return m_new, l, acc


def _flash_init(tq):
    return (jnp.full((tq, 1), NEG, F32), jnp.zeros((tq, 1), F32), jnp.zeros((tq, LANES), F32))


def _fox_kernel(q_ref, k_ref, v_ref, z_ref, ccol_ref, crow0_ref, crow1_ref, o_ref, *, tq):
    seq = q_ref.shape[1]
    hp = pl.program_id(1)
    lane = lax.broadcasted_iota(I32, (1, LANES), 1)
    causal = (lax.broadcasted_iota(I32, (tq, tq), 1) <= lax.broadcasted_iota(I32, (tq, tq), 0))
    crow_refs = (crow0_ref, crow1_ref)

    def q_block(i, _):
        r0 = pl.multiple_of(i * tq, tq)
        q2 = q_ref[0, pl.ds(r0, tq), :]
        cblk = ccol_ref[0, pl.ds(r0, tq), :]
        outs = []
        for e in range(2):
            half = (lane >= HEAD_DIM) if e else (lane < HEAD_DIM)
            qm = jnp.where(half, q2, jnp.zeros_like(q2))
            cq = jnp.sum(jnp.where(lane == 2 * hp + e, cblk, 0.0), axis=-1, keepdims=True)
            crow_ref = crow_refs[e]

            def logits(kb):
                c0 = pl.multiple_of(kb * tq, tq)
                s = _dot_nt(qm, k_ref[0, pl.ds(c0, tq), :])
                return s + cq - crow_ref[0, 0, :, pl.ds(c0, tq)], v_ref[0, pl.ds(c0, tq), :]

            def step(kb, carry):
                s, v = logits(kb)
                return _flash_update(carry, s, v)

            carry = lax.fori_loop(0, i, step, _flash_init(tq))
            s, v = logits(i)
            _, l, acc = _flash_update(carry, jnp.where(causal, s, NEG), v)
            outs.append(acc * (1.0 / l))
        out = jnp.where(lane < HEAD_DIM, outs[0], outs[1])
        out = out * _silu(z_ref[0, pl.ds(r0, tq), :].astype(F32))
        o_ref[0, pl.ds(r0, tq), :] = out.astype(o_ref.dtype)
        return 0

    lax.fori_loop(0, seq // tq, q_block, 0)


def _fox_attention(main, ccol, crow, batch, seq, tq=256):
    pairs = A_W // LANES
    blk = lambda base: pl.BlockSpec((1, seq, LANES), lambda b, p, base=base: (b, 0, base // LANES + p))
    return pl.pallas_call(
        functools.partial(_fox_kernel, tq=tq),
        grid=(batch, pairs),
        in_specs=[blk(MAIN_QA), blk(MAIN_KA), blk(MAIN_VA), blk(MAIN_ZA),
                  pl.BlockSpec((1, seq, LANES), lambda b, p: (b, 0, 0)),
                  pl.BlockSpec((1, 1, 1, seq), lambda b, p: (b, 2 * p, 0, 0)),
                  pl.BlockSpec((1, 1, 1, seq), lambda b, p: (b, 2 * p + 1, 0, 0))],
        out_specs=pl.BlockSpec((1, seq, LANES), lambda b, p: (b, 0, p)),
        out_shape=jax.ShapeDtypeStruct((batch, seq, A_W), BF16),
        compiler_params=_cparams(2),
        name="fox_attention",
    )(main, main, main, main, ccol, crow, crow)


DSA_TQ = 128
DSA_TK = 256


def _dsa_kernel(side_ref, z_ref, gcq_ref, wuq_ref, wuqi_ref, gckv_ref, wukv2_ref, tile_ref, tri_ref,
                o_ref, qb_s, qi_s, kt_s, kv_s, vk_s, qm_s, key_s, mb_s, *, top_k, slopes):
    seq = side_ref.shape[1]
    tq, tk = DSA_TQ, DSA_TK
    lane = lax.broadcasted_iota(I32, (1, LANES), 1)
    lane2 = lax.broadcasted_iota(I32, (1, 2 * LANES), 1)
    col_minus_row = (lax.broadcasted_iota(I32, (tq, tk), 1) - lax.broadcasted_iota(I32, (tq, tk), 0))

    def prep(c, _):
        r0 = pl.multiple_of(c * tk, tk)
        blk = side_ref[0, pl.ds(r0, tk), :]
        cqn = _rms(blk[:, SIDE_CQ:SIDE_CQ + B_Q_RANK], gcq_ref[...]).astype(BF16)
        qb_s[pl.ds(r0, tk), :] = _dot(cqn, wuq_ref[...]).astype(BF16)
        qi_s[pl.ds(r0, tk), :] = _dot(cqn, wuqi_ref[...]).astype(BF16)
        kvn = _rms(blk[:, B_Q_RANK:SIDE_MISC], gckv_ref[...]).astype(BF16)
        kv2 = _dot(kvn, wukv2_ref[...])
        kv_s[pl.ds(r0, tk), :] = kv2[:, :LANES].astype(BF16)
        vk_s[pl.ds(r0, tk), :] = kv2[:, LANES:].astype(BF16)
        kt_s[pl.ds(r0, tk), :] = _dot(blk[:, SIDE_MISC:].astype(BF16), tile_ref[...]).astype(BF16)
        return 0

    lax.fori_loop(0, seq // tk, prep, 0)
    key_s[...] = jnp.full(key_s.shape, INT_MIN, I32)
    mb_s[...] = jnp.full(mb_s.shape, NEG, F32)

    def q_block(i, _):
        r0 = pl.multiple_of(i * tq, tq)
        nk = i // 2 + 1
        diag_off = (i % 2) * tq

        qi = qi_s[pl.ds(r0, tq), :]
        for h in range(IDX_HEADS):
            in_head = (lane2 >= h * IDX_DIM) & (lane2 < (h + 1) * IDX_DIM)
            qm_s[h] = jnp.where(in_head, qi, jnp.zeros_like(qi))
        misc = side_ref[0, pl.ds(r0, tq), SIDE_MISC:SIDE_MISC + LANES]
        w_cols = [jnp.sum(jnp.where(lane == SIDE_WIDX + h, misc, 0.0), axis=-1, keepdims=True)
                  for h in range(IDX_HEADS)]

        def score_chunk(kc, masked):
            c0 = pl.multiple_of(kc * tk, tk)
            kt = kt_s[pl.ds(c0, tk), :]
            acc = jnp.zeros((tq, tk), F32)
            for h in range(IDX_HEADS):
                acc = acc + w_cols[h] * jnp.maximum(_dot_nt(qm_s[h], kt), 0.0)
            bits = lax.bitcast_convert_type(acc, I32)
            key = bits ^ (lax.shift_right_arithmetic(bits, 31) & 0x7FFFFFFF)
            key = jnp.where(key == -1, 0, key)
            if masked:
                key = jnp.where(col_minus_row <= diag_off, key, INT_MIN)
            key_s[:, pl.ds(c0, tk)] = key

        def score_body(kc, _):
            score_chunk(kc, False)
            return 0

        lax.fori_loop(0, nk - 1, score_body, 0)
        score_chunk(nk - 1, True)

        def count(pred):
            def body(kc, acc):
                c0 = pl.multiple_of(kc * tk, tk)
                return acc + jnp.where(pred(key_s[:, pl.ds(c0, tk)]), 1.0, 0.0)
            acc = lax.fori_loop(0, nk, body, jnp.zeros((tq, tk), F32))
            return jnp.sum(acc, axis=-1, keepdims=True)

        def search(p, t):
            cand = t ^ lax.shift_left(jnp.int32(1), 31 - p)
            cnt = count(lambda k: k >= cand)
            return jnp.where(cnt >= top_k, cand, t)

        thr = lax.fori_loop(0, 32, search, jnp.full((tq, 1), INT_MIN, I32))
        thr = jnp.maximum(thr, INT_MIN + 1)

        def mask_body(kc, _):
            c0 = pl.multiple_of(kc * tk, tk)
            mb_s[:, pl.ds(c0, tk)] = jnp.where(key_s[:, pl.ds(c0, tk)] >= thr, 0.0, NEG)
            return 0

        lax.fori_loop(0, nk, mask_body, 0)

        n_ge = count(lambda k: k >= thr)

        @pl.when(jnp.max(n_ge) > top_k)
        def _():
            room = top_k - count(lambda k: k > thr)

            def tie_body(kc, seen):
                c0 = pl.multiple_of(kc * tk, tk)
                tie = key_s[:, pl.ds(c0, tk)] == thr
                rank = seen + _dot(jnp.where(tie, 1.0, 0.0).astype(BF16), tri_ref[...])
                drop = tie & (rank > room)
                mb_s[:, pl.ds(c0, tk)] = jnp.where(drop, NEG, mb_s[:, pl.ds(c0, tk)])
                return jnp.max(rank, axis=-1, keepdims=True)

            lax.fori_loop(0, nk, tie_body, jnp.zeros((tq, 1), F32))

        for hp in range(B_W // LANES):
            q2 = qb_s[pl.ds(r0, tq), hp * LANES:(hp + 1) * LANES]
            outs = []
            for e in range(2):
                half = (lane >= HEAD_DIM) if e else (lane < HEAD_DIM)
                qm = jnp.where(half, q2, jnp.zeros_like(q2))
                k_src, v_src = (vk_s, kv_s) if e else (kv_s, vk_s)
                slope = slopes[2 * hp + e]

                def step(kc, carry):
                    c0 = pl.multiple_of(kc * tk, tk)
                    s = _dot_nt(qm, k_src[pl.ds(c0, tk), :])
                    dist = (col_minus_row + (c0 - r0)).astype(F32)
                    s = s + slope * dist + mb_s[:, pl.ds(c0, tk)]
                    return _flash_update(carry, s, v_src[pl.ds(c0, tk), :])

                _, l, acc = lax.fori_loop(0, nk, step, _flash_init(tq))
                outs.append(acc * (1.0 / l))
            out = jnp.where(lane < HEAD_DIM, outs[0], outs[1])
            out = out * _silu(z_ref[0, pl.ds(r0, tq), hp * LANES:(hp + 1) * LANES].astype(F32))
            o_ref[0, pl.ds(r0, tq), hp * LANES:(hp + 1) * LANES] = out.astype(o_ref.dtype)
        return 0

    lax.fori_loop(0, seq // tq, q_block, 0)


def _dsa_attention(side, main, g_cq, w_uq, w_uq_idx, g_ckv, w_ukv2, tile, tri, batch, seq):
    top_k = min(IDX_TOPK_MAX, seq // 4)
    slopes = tuple(float(2.0 ** (-8.0 * (h + 1) / B_HEADS)) for h in range(B_HEADS))
    const = lambda shape: pl.BlockSpec(shape, lambda b: (0,) * len(shape))
    return pl.pallas_call(
        functools.partial(_dsa_kernel, top_k=top_k, slopes=slopes),
        grid=(batch,),
        in_specs=[pl.BlockSpec((1, seq, 512), lambda b: (b, 0, 0)),
                  pl.BlockSpec((1, seq, B_W), lambda b: (b, 0, MAIN_ZB // B_W)),
                  const((1, B_Q_RANK)), const((B_Q_RANK, B_W)), const((B_Q_RANK, IDX_HEADS * IDX_DIM)),
                  const((1, B_KV_RANK)), const((B_KV_RANK, 2 * LANES)),
                  const((LANES, 2 * LANES)), const((DSA_TK, DSA_TK))],
        out_specs=pl.BlockSpec((1, seq, B_W), lambda b: (b, 0, 0)),
        out_shape=jax.ShapeDtypeStruct((batch, seq, B_W), BF16),
        scratch_shapes=[pltpu.VMEM((seq, B_W), BF16),
                        pltpu.VMEM((seq, IDX_HEADS * IDX_DIM), BF16),
                        pltpu.VMEM((seq, IDX_HEADS * IDX_DIM), BF16),
                        pltpu.VMEM((seq, LANES), BF16),
                        pltpu.VMEM((seq, LANES), BF16),
                        pltpu.VMEM((IDX_HEADS, DSA_TQ, IDX_HEADS * IDX_DIM), BF16),
                        pltpu.VMEM((DSA_TQ, seq), I32),
                        pltpu.VMEM((DSA_TQ, seq), F32)],
        compiler_params=_cparams(1),
        name="dsa_attention",
    )(side, main, g_cq, w_uq, w_uq_idx, g_ckv, w_ukv2, tile, tri)


def _dilated_kernel(q_ref, k_ref, v_ref, bias_ref, z0_ref, z1_ref, z2_ref, o_ref, o_s, l_s):
    seq = q_ref.shape[1]
    sp = pl.program_id(1)
    grp = pl.program_id(2)
    lane = lax.broadcasted_iota(I32, (1, LANES), 1)
    cw = CLASS_WINDOW

    def rows(start, d):
        return pl.ds(start, cw) if d == 1 else pl.ds(start, cw, stride=d)

    def group_body(g, d):
        per_class = seq // (d * cw)

        def blk(n, _):
            r = n // per_class
            ib = n % per_class
            start = r + ib * (cw * d)
            pstart = r + jnp.maximum(ib - 1, 0) * (cw * d)
            q2 = q_ref[0, rows(start, d), :].astype(BF16)
            kc = k_ref[0, rows(start, d), :].astype(BF16)
            vc = v_ref[0, rows(start, d), :].astype(BF16)
            kp = k_ref[0, rows(pstart, d), :].astype(BF16)
            vp = v_ref[0, rows(pstart, d), :].astype(BF16)
            outs, lses = [], []
            for e in range(2):
                half = (lane >= HEAD_DIM) if e else (lane < HEAD_DIM)
                qm = jnp.where(half, q2, jnp.zeros_like(q2))
                s_cur = _dot_nt(qm, kc) + bias_ref[0, e, :, cw:]
                s_prev = _dot_nt(qm, kp) + jnp.where(ib > 0, bias_ref[0, e, :, :cw], NEG)
                m = jnp.maximum(jnp.max(s_cur, axis=-1, keepdims=True),
                                jnp.max(s_prev, axis=-1, keepdims=True))
                p_cur = jnp.exp(s_cur - m)
                p_prev = jnp.exp(s_prev - m)
                l = jnp.sum(p_cur, axis=-1, keepdims=True) + jnp.sum(p_prev, axis=-1, keepdims=True)
                o = _dot(p_cur.astype(BF16), vc) + _dot(p_prev.astype(BF16), vp)
                outs.append(o * (1.0 / l))
                lses.append(m + jnp.log(l))
            o_s[g, rows(start, d), :] = jnp.where(lane < HEAD_DIM, outs[0], outs[1])
            l_s[g, rows(start, d), :] = jnp.where(lane < HEAD_DIM, lses[0], lses[1])
            return 0

        lax.fori_loop(0, seq // cw, blk, 0)

    for g, (_, d) in enumerate(C_GROUPS):
        pl.when(grp == g)(functools.partial(group_body, g, d))

    n_grp = len(C_GROUPS)
    z_refs = (z0_ref, z1_ref, z2_ref)

    def combine(sp_static):
        def chunk(c, _):
            r0 = pl.multiple_of(c * 256, 256)
            ls = [l_s[g, pl.ds(r0, 256), :] for g in range(n_grp)]
            m = functools.reduce(jnp.maximum, ls)
            ws = [jnp.exp(x - m) for x in ls]
            inv = 1.0 / functools.reduce(lambda a, b: a + b, ws)
            for g in range(n_grp):
                col = (g * 2 + sp_static) * LANES
                y = (ws[g] * inv) * o_s[g, pl.ds(r0, 256), :]
                y = y * _silu(z_refs[g][0, pl.ds(r0, 256), :].astype(F32))
                o_ref[0, pl.ds(r0, 256), col:col + LANES] = y.astype(o_ref.dtype)
            return 0

        lax.fori_loop(0, seq // 256, chunk, 0)

    for s in range(2):
        pl.when((grp == n_grp - 1) & (sp == s))(functools.partial(combine, s))


def _dilated_attention(side, main, bias, batch, seq):
    assert seq % (C_GROUPS[-1][1] * CLASS_WINDOW) == 0
    n_grp = len(C_GROUPS)
    blk = lambda base: pl.BlockSpec(
        (1, seq, LANES), lambda b, s, g, base=base: (b, 0, base // LANES + 2 * g + s))
    zblk = lambda g: pl.BlockSpec(
        (1, seq, LANES), lambda b, s, _, g=g: (b, 0, MAIN_ZC // LANES + 2 * g + s))
    return pl.pallas_call(
        _dilated_kernel,
        grid=(batch, 2, n_grp),
        in_specs=[blk(SIDE_QC), blk(SIDE_KC), blk(SIDE_VC),
                  pl.BlockSpec((1, 2, CLASS_WINDOW, 2 * CLASS_WINDOW), lambda b, s, g: (2 * g + s, 0, 0, 0)),
                  zblk(0), zblk(1), zblk(2)],
        out_specs=pl.BlockSpec((1, seq, C_W), lambda b, s, g: (b, 0, 0)),
        out_shape=jax.ShapeDtypeStruct((batch, seq, C_W), BF16),
        scratch_shapes=[pltpu.VMEM((n_grp, seq, LANES), F32),
                        pltpu.VMEM((n_grp, seq, LANES), F32)],
        compiler_params=_cparams(3),
        name="dilated_attention",
    )(side, side, side, bias, main, main, main)


def _dilated_bias():
    cw = CLASS_WINDOW
    slopes = np.exp2(-8.0 * (np.arange(C_HEADS, dtype=np.float32) + 1.0) / C_HEADS).astype(np.float32)
    i = np.arange(cw)[:, None]
    j = np.arange(cw)[None, :]
    out = np.empty((C_HEADS // 2, 2, cw, 2 * cw), np.float32)
    for h in range(C_HEADS):
        d = C_GROUPS[h // C_HEADS_PER_GROUP][1]
        prev = np.where(j >= i, -slopes[h] * np.float32(d) * (cw + i - j).astype(np.float32), NEG)
        cur = np.where(j <= i, -slopes[h] * np.float32(d) * (i - j).astype(np.float32), NEG)
        out[h // 2, h % 2] = np.concatenate([prev, cur], axis=1)
    return jnp.asarray(out)


def _merge_kernel(x_ref, ya_ref, yb_ref, yc_ref, g_ref, wa_ref, wb_ref, wc_ref, wo_ref, fg_ref, o_ref,
                  *, final):
    gate = lambda n: _sigmoid(g_ref[:, n * D_MODEL:(n + 1) * D_MODEL].astype(F32))
    merged = (gate(0) * _dot(ya_ref[...], wa_ref[...])
              + gate(1) * _dot(yb_ref[...], wb_ref[...])
              + gate(2) * _dot(yc_ref[...], wc_ref[...]))
    y = x_ref[...] + _dot(merged.astype(BF16), wo_ref[...])
    if final:
        y = _rms(y, fg_ref[...])
    o_ref[...] = y


def _merge(x2d, ya, yb, yc, main, w_a, w_b, w_c, w_o, final_g, final, tm=512):
    m = x2d.shape[0]
    row = lambda w: pl.BlockSpec((tm, w), lambda i: (i, 0))
    const = lambda a: pl.BlockSpec(a.shape, lambda i: (0, 0))
    return pl.pallas_call(
        functools.partial(_merge_kernel, final=final),
        grid=(m // tm,),
        in_specs=[row(D_MODEL), row(A_W), row(B_W), row(C_W), row(3 * D_MODEL),
                  const(w_a), const(w_b), const(w_c), const(w_o), const(final_g)],
        out_specs=row(D_MODEL),
        out_shape=jax.ShapeDtypeStruct((m, D_MODEL), F32),
        compiler_params=_cparams(1),
        name="gated_merge",
    )(x2d, ya, yb, yc, main, w_a, w_b, w_c, w_o, final_g)


def _layer_weights(norm_g, w_in, b_forget, g_cq, w_uq, w_uq_idx, g_ckv, w_ukv, w_a, w_b, w_c, w_o):
    points = np.cumsum(IN_WIDTHS)[:-1].tolist()
    (wqa, wka, wva, wfa, wza, wcq, wckv, wkidx, wwidx, wzb,
     wqc, wkc, wvc, wzc, wga, wgb, wgc) = jnp.split(w_in, points, axis=1)
    zeros = lambda n: jnp.zeros((D_MODEL, n), F32)
    w_main = jnp.concatenate([wga, wgb, wgc, wzb, wzc, wqa * SCALE, wka, wva, wza], axis=1)
    w_side = jnp.concatenate(
        [wcq, wckv, wkidx, wwidx, zeros(LANES - IDX_DIM - IDX_HEADS),
         wfa, zeros(LANES - A_HEADS), wqc * SCALE, wkc, wvc, zeros(SIDE_N - SIDE_VC - C_W)], axis=1)
    assert w_main.shape[1] == MAIN_N and w_side.shape[1] == SIDE_N
    wk, wv = w_ukv[:, :HEAD_DIM], w_ukv[:, HEAD_DIM:]
    return dict(
        norm_g=norm_g.reshape(1, D_MODEL),
        w_main=w_main.astype(BF16), w_side=w_side.astype(BF16),
        b_pad=jnp.zeros((1, LANES), F32).at[0, :A_HEADS].set(b_forget),
        g_cq=g_cq.reshape(1, B_Q_RANK), w_uq=(w_uq * SCALE).astype(BF16), w_uq_idx=w_uq_idx.astype(BF16),
        g_ckv=g_ckv.reshape(1, B_KV_RANK), w_ukv2=jnp.concatenate([wk, wv, wv, wk], axis=1).astype(BF16),
        w_a=w_a.astype(BF16), w_b=w_b.astype(BF16), w_c=w_c.astype(BF16), w_o=w_o.astype(BF16))


def _constants():
    r = np.arange(LANES)
    tri_low = (r[None, :] <= r[:, None]).astype(np.float32)
    r2 = np.arange(DSA_TK)
    tri_up = (r2[:, None] <= r2[None, :]).astype(np.float32)
    c = np.arange(IDX_HEADS * IDX_DIM)
    tile = ((r[:, None] == c[None, :] % IDX_DIM) & (r[:, None] < IDX_DIM)).astype(np.float32)
    return (jnp.asarray(tri_low, BF16), jnp.asarray(tri_up, BF16), jnp.asarray(tile, BF16), _dilated_bias())


def _hybrid_layer(x2d, batch, seq, w, consts, final_g, final):
    tri_low, tri_up, tile, bias_c = consts
    main = _rms_matmul(x2d, w["norm_g"], w["w_main"], BF16, tm=1024, tn=1280)
    side = _rms_matmul(x2d, w["norm_g"], w["w_side"], F32, tm=1024, tn=1536)
    main3 = main.reshape(batch, seq, MAIN_N)
    side3 = side.reshape(batch, seq, SIDE_N)
    ccol, crow = _cum_forget(side3, w["b_pad"], tri_low, batch, seq)
    ya = _fox_attention(main3, ccol, crow, batch, seq)
    yb = _dsa_attention(side3, main3, w["g_cq"], w["w_uq"], w["w_uq_idx"], w["g_ckv"], w["w_ukv2"],
                        tile, tri_up, batch, seq)
    yc = _dilated_attention(side3, main3, bias_c, batch, seq)
    m = batch * seq
    return _merge(x2d, ya.reshape(m, A_W), yb.reshape(m, B_W), yc.reshape(m, C_W), main,
                  w["w_a"], w["w_b"], w["w_c"], w["w_o"], final_g, final)


def kernel(x, norm_g, w_in, b_forget, g_cq, w_uq, w_uq_idx, g_ckv, w_ukv, w_a, w_b, w_c, w_o, final_g):
    batch, seq, d_model = x.shape
    assert d_model == D_MODEL
    depth = norm_g.shape[0]
    consts = _constants()
    fg = final_g.reshape(1, D_MODEL)
    x2d = x.reshape(batch * seq, D_MODEL)
    for l in range(depth):
        w = _layer_weights(norm_g[l], w_in[l], b_forget[l], g_cq[l], w_uq[l], w_uq_idx[l], g_ckv[l],
                           w_ukv[l], w_a[l], w_b[l], w_c[l], w_o[l])
        x2d = _hybrid_layer(x2d, batch, seq, w, consts, fg, final=(l == depth - 1))
    return x2d.reshape(batch, seq, D_MODEL)
```

```python
import functools

import numpy as np
import jax
import jax.numpy as jnp
from jax import lax
from jax.experimental import pallas as pl
from jax.experimental.pallas import tpu as pltpu

F32 = jnp.float32
BF16 = jnp.bfloat16
I32 = jnp.int32

D_MODEL = 1024
HEAD_DIM = 64
EPS = 1e-6
A_HEADS = 8
A_W = A_HEADS * HEAD_DIM
B_HEADS = 8
B_W = B_HEADS * HEAD_DIM
B_Q_RANK = 256
B_KV_RANK = 128
IDX_HEADS = 8
IDX_DIM = 32
IDX_TOPK_MAX = 256
C_GROUPS = ((128, 1), (512, 4), (2048, 16))
C_HEADS_PER_GROUP = 4
C_HEADS = C_HEADS_PER_GROUP * len(C_GROUPS)
C_W = C_HEADS * HEAD_DIM
IN_WIDTHS = (A_W, A_W, A_W, A_HEADS, A_W,
             B_Q_RANK, B_KV_RANK, IDX_DIM, IDX_HEADS, B_W,
             C_W, C_W, C_W, C_W,
             D_MODEL, D_MODEL, D_MODEL)

LANES = 128
SUBLANES = 8
SCALE = HEAD_DIM ** -0.5
NEG = -1e30
INT_MIN = -2 ** 31
CLASS_WINDOW = 128
assert all(w // d == CLASS_WINDOW for w, d in C_GROUPS)
ATT_T = 256
POS_SPLIT = 64

MAIN_G = 0
MAIN_ZB = 3 * D_MODEL
MAIN_QA = MAIN_ZB + B_W
MAIN_KA = MAIN_QA + A_W
MAIN_VA = MAIN_KA + A_W
MAIN_ZA = MAIN_VA + A_W
MAIN_ZC = MAIN_ZA + A_W
MAIN_N = MAIN_ZC + C_W
SIDE_CQ = 0
SIDE_MISC = B_Q_RANK + B_KV_RANK
SIDE_WIDX = IDX_DIM
SIDE_FA = 512
SIDE_QC = SIDE_FA + LANES
SIDE_KC = SIDE_QC + C_W
SIDE_VC = SIDE_KC + C_W
SIDE_N = 3072

VMEM_LIMIT = 56 * 1024 * 1024


def _cparams(n_axes, vmem=VMEM_LIMIT):
    return pltpu.CompilerParams(dimension_semantics=("arbitrary",) * n_axes,
                                vmem_limit_bytes=vmem)


def _dot(a, b):
    return jnp.dot(a, b, preferred_element_type=F32)


def _dot_nt(a, b):
    return lax.dot_general(a, b, (((1,), (1,)), ((), ())), preferred_element_type=F32)


def _sigmoid(x):
    return 1.0 / (1.0 + jnp.exp(-x))


def _silu(x):
    return x * _sigmoid(x)


def _rms(x, g):
    return x * lax.rsqrt(jnp.mean(x * x, axis=-1, keepdims=True) + EPS) * g


def _split3(x):
    hi = x.astype(BF16)
    r1 = x - hi.astype(F32)
    mid = r1.astype(BF16)
    lo = (r1 - mid.astype(F32)).astype(BF16)
    return hi, mid, lo


def _fold8(x, op):
    n, t = x.shape
    return op(x.reshape(n // SUBLANES, SUBLANES, t), axis=0)


def _keys_max(x):
    return jnp.max(_fold8(x, jnp.max), axis=0, keepdims=True)


def _online_update_heads(states, s_ts, v_ts):
    ps, scaled = [], []
    for (m, l8, acc), s_t in zip(states, s_ts):
        m_new = jnp.maximum(m, _keys_max(s_t))
        alpha = jnp.exp(m - m_new)
        p = jnp.exp(s_t - m_new)
        ps.append(p.astype(BF16))
        scaled.append((m_new, alpha * l8 + _fold8(p, jnp.sum), alpha * acc))
    return [(m, l8, acc + _dot(v_t, p)) for (m, l8, acc), v_t, p in zip(scaled, v_ts, ps)]


def _online_init(t):
    return (jnp.full((1, t), NEG, F32), jnp.zeros((SUBLANES, t), F32), jnp.zeros((LANES, t), F32))


def _finish_pair(state0, state1):
    row = lax.broadcasted_iota(I32, (LANES, 1), 0)
    norm = lambda st: st[2] * (1.0 / jnp.sum(st[1], axis=0, keepdims=True))
    return jnp.where(row < HEAD_DIM, norm(state0), norm(state1)).T


def _rms_matmul_kernel(x_ref, g_ref, w_ref, o_ref, h_ref):
    @pl.when(pl.program_id(1) == 0)
    def _():
        h_ref[...] = _rms(x_ref[...], g_ref[...]).astype(BF16)

    o_ref[...] = _dot(h_ref[...], w_ref[...]).astype(o_ref.dtype)


def _rms_matmul(x2d, g, w, out_dtype, tm, tn):
    m, k = x2d.shape
    n = w.shape[1]
    return pl.pallas_call(
        _rms_matmul_kernel,
        grid=(m // tm, n // tn),
        in_specs=[pl.BlockSpec((tm, k), lambda i, j: (i, 0)),
                  pl.BlockSpec((1, k), lambda i, j: (0, 0)),
                  pl.BlockSpec((k, tn), lambda i, j: (0, j))],
        out_specs=pl.BlockSpec((tm, tn), lambda i, j: (i, j)),
        out_shape=jax.ShapeDtypeStruct((m, n), out_dtype),
        scratch_shapes=[pltpu.VMEM((tm, k), BF16)],
        compiler_params=_cparams(2),
        name="rms_in_proj",
    )(x2d, g, w)


def _cum_kernel(fa_ref, b_ref, tri_ref, selq_ref, selk_ref, oneq_ref, onek_ref, qf_ref, kf_ref):
    seq = fa_ref.shape[1]
    tri = tri_ref[...]
    carry = jnp.zeros((1, LANES), F32)
    for blk in range(seq // LANES):
        rows = slice(blk * LANES, (blk + 1) * LANES)
        x = fa_ref[0, rows, :] + b_ref[...]
        lf = jnp.minimum(x, 0.0) - jnp.log(1.0 + jnp.exp(-jnp.abs(x)))
        hi, mid, lo = _split3(lf)
        c = _dot(tri, hi) + _dot(tri, mid) + _dot(tri, lo) + carry
        carry = c[LANES - 1:LANES, :]
        c3 = jnp.concatenate(_split3(c), axis=1)
        for h in range(A_HEADS):
            qf_ref[0, h, rows, :] = (_dot(c3, selq_ref[h]) + oneq_ref[...]).astype(BF16)
            kf_ref[0, h, rows, :] = (_dot(c3, selk_ref[h]) + onek_ref[...]).astype(BF16)


def _cum_forget(side, b_pad, consts, batch, seq):
    tri, selq, selk, oneq, onek = consts
    full = lambda a: pl.BlockSpec(a.shape, lambda b: (0,) * a.ndim)
    feat = jax.ShapeDtypeStruct((batch, A_HEADS, seq, LANES), BF16)
    return pl.pallas_call(
        _cum_kernel,
        grid=(batch,),
        in_specs=[pl.BlockSpec((1, seq, LANES), lambda b: (b, 0, SIDE_FA // LANES)),
                  full(b_pad), full(tri), full(selq), full(selk), full(oneq), full(onek)],
        out_specs=[pl.BlockSpec((1, A_HEADS, seq, LANES), lambda b: (b, 0, 0, 0))] * 2,
        out_shape=[feat, feat],
        compiler_params=_cparams(1),
        name="fox_cumsum",
    )(side, b_pad, tri, selq, selk, oneq, onek)


def _fox_kernel(q_ref, k_ref, v_ref, z_ref, qf_ref, kf_ref, o_ref, vt_s, qa_s, m_s, l_s, acc_s):
    seq = q_ref.shape[1]
    t = ATT_T
    n_heads = A_HEADS
    lane = lax.broadcasted_iota(I32, (1, LANES), 1)
    key_le_query = (lax.broadcasted_iota(I32, (t, t), 0) <= lax.broadcasted_iota(I32, (t, t), 1))
    pair_cols = lambda h: slice((h // 2) * LANES, (h // 2 + 1) * LANES)

    def transpose_v(c, _):
        cols = pl.ds(pl.multiple_of(c * t, t), t)
        for p in range(n_heads // 2):
            pc = slice(p * LANES, (p + 1) * LANES)
            vt_s[pc, cols] = v_ref[0, cols, pc].astype(F32).T.astype(BF16)
        return 0

    lax.fori_loop(0, seq // t, transpose_v, 0)

    def q_block(i, _):
        rows = pl.ds(pl.multiple_of(i * t, t), t)
        for h in range(n_heads):
            half = (lane >= HEAD_DIM) if h % 2 else (lane < HEAD_DIM)
            q2 = q_ref[0, rows, pair_cols(h)]
            qa_s[h] = jnp.concatenate([jnp.where(half, q2, jnp.zeros_like(q2)), qf_ref[0, h, rows, :]], axis=1)
            m_s[h], l_s[h], acc_s[h] = _online_init(t)

        def chunk(kb, diagonal):
            cols = pl.ds(pl.multiple_of(kb * t, t), t)
            states = [(m_s[h], l_s[h], acc_s[h]) for h in range(n_heads)]
            s_ts = [_dot_nt(jnp.concatenate([k_ref[0, cols, pair_cols(h)], kf_ref[0, h, cols, :]], axis=1),
                            qa_s[h]) for h in range(n_heads)]
            if diagonal:
                s_ts = [jnp.where(key_le_query, s_t, NEG) for s_t in s_ts]
            v_ts = [vt_s[pair_cols(h), cols] for h in range(n_heads)]
            for h, st in enumerate(_online_update_heads(states, s_ts, v_ts)):
                m_s[h], l_s[h], acc_s[h] = st

        def off_diagonal(kb, _):
            chunk(kb, False)
            return 0

        lax.fori_loop(0, i, off_diagonal, 0)
        chunk(i, True)
        for p in range(n_heads // 2):
            st = [(m_s[h], l_s[h], acc_s[h]) for h in (2 * p, 2 * p + 1)]
            out = _finish_pair(*st) * _silu(z_ref[0, rows, pair_cols(2 * p)].astype(F32))
            o_ref[0, rows, pair_cols(2 * p)] = out.astype(o_ref.dtype)
        return 0

    lax.fori_loop(0, seq // t, q_block, 0)


def _fox_attention(main, qf, kf, batch, seq):
    t = ATT_T
    blk = lambda base: pl.BlockSpec((1, seq, A_W), lambda b, base=base: (b, 0, base // A_W))
    feat = pl.BlockSpec((1, A_HEADS, seq, LANES), lambda b: (b, 0, 0, 0))
    return pl.pallas_call(
        _fox_kernel,
        grid=(batch,),
        in_specs=[blk(MAIN_QA), blk(MAIN_KA), blk(MAIN_VA), blk(MAIN_ZA), feat, feat],
        out_specs=pl.BlockSpec((1, seq, A_W), lambda b: (b, 0, 0)),
        out_shape=jax.ShapeDtypeStruct((batch, seq, A_W), BF16),
        scratch_shapes=[pltpu.VMEM((A_W, seq), BF16),
                        pltpu.VMEM((A_HEADS, t, 2 * LANES), BF16),
                        pltpu.VMEM((A_HEADS, 1, t), F32),
                        pltpu.VMEM((A_HEADS, SUBLANES, t), F32),
                        pltpu.VMEM((A_HEADS, LANES, t), F32)],
        compiler_params=_cparams(1),
        name="fox_attention",
    )(main, main, main, main, qf, kf)


def _pos_features(pos, lane, first):
    hi = lax.shift_right_logical(pos, 6).astype(F32)
    lo = (pos & (POS_SPLIT - 1)).astype(F32)
    return jnp.where(lane == first, hi,
                     jnp.where(lane == first + 1, lo,
                               jnp.where((lane == first + 2) | (lane == first + 3), 1.0, 0.0)))


def _dsa_kernel(side_ref, z_ref, gcq_ref, wuq_ref, wuqi_ref, gckv_ref, wkv_ref, wvk_ref, wkvt_ref, wvkt_ref,
                tile_ref, tri_ref, o_ref,
                qb_s, qi_s, kt_s, ka_s, kb_s, kvt_s, vkt_s, wt_s, qm_s, qa_s, key_s, mb_s, m_s, l_s, acc_s,
                *, top_k, slopes):
    seq = side_ref.shape[1]
    t = ATT_T
    n_heads = B_HEADS
    lane = lax.broadcasted_iota(I32, (1, LANES), 1)
    lane2 = lax.broadcasted_iota(I32, (1, 2 * LANES), 1)
    key_le_query = (lax.broadcasted_iota(I32, (t, t), 0) <= lax.broadcasted_iota(I32, (t, t), 1))
    row_iota = lax.broadcasted_iota(I32, (t, 1), 0)

    def prep(c, _):
        r0 = pl.multiple_of(c * t, t)
        rows = pl.ds(r0, t)
        blk = side_ref[0, rows, :]
        cqn = _rms(blk[:, SIDE_CQ:SIDE_CQ + B_Q_RANK], gcq_ref[...]).astype(BF16)
        qb_s[rows, :] = _dot(cqn, wuq_ref[...]).astype(BF16)
        qi_s[rows, :] = _dot(cqn, wuqi_ref[...]).astype(BF16)
        kvn = _rms(blk[:, B_Q_RANK:SIDE_MISC], gckv_ref[...]).astype(BF16)
        pos = r0 + row_iota
        ka_s[rows, :] = jnp.where(lane < HEAD_DIM, _dot(kvn, wkv_ref[...]),
                                  _pos_features(pos, lane, HEAD_DIM)).astype(BF16)
        kb_s[rows, :] = jnp.where(lane >= HEAD_DIM, _dot(kvn, wvk_ref[...]),
                                  _pos_features(pos, lane, 0)).astype(BF16)
        kvt_s[:, rows] = _dot_nt(wkvt_ref[...], kvn).astype(BF16)
        vkt_s[:, rows] = _dot_nt(wvkt_ref[...], kvn).astype(BF16)
        misc = blk[:, SIDE_MISC:]
        kt_s[rows, :] = _dot(misc.astype(BF16), tile_ref[...]).astype(BF16)
        wt_s[:, rows] = misc.T
        return 0

    lax.fori_loop(0, seq // t, prep, 0)

    def q_block(i, _):
        r0 = pl.multiple_of(i * t, t)
        rows = pl.ds(r0, t)
        nk = i + 1

        qi = qi_s[rows, :]
        for h in range(IDX_HEADS):
            in_head = (lane2 >= h * IDX_DIM) & (lane2 < (h + 1) * IDX_DIM)
            qm_s[h] = jnp.where(in_head, qi, jnp.zeros_like(qi))
        w_rows = [wt_s[SIDE_WIDX + h:SIDE_WIDX + h + 1, rows] for h in range(IDX_HEADS)]

        def score_chunk(kc, diagonal):
            c0 = pl.multiple_of(kc * t, t)
            kt = kt_s[pl.ds(c0, t), :]
            acc = jnp.zeros((t, t), F32)
            for h in range(IDX_HEADS):
                acc = acc + w_rows[h] * jnp.maximum(_dot_nt(kt, qm_s[h]), 0.0)
            bits = lax.bitcast_convert_type(acc, I32)
            key = bits ^ (lax.shift_right_arithmetic(bits, 31) & 0x7FFFFFFF)
            key = jnp.where(key == -1, 0, key)
            if diagonal:
                key = jnp.where(key_le_query, key, INT_MIN)
            key_s[pl.ds(c0, t), :] = key

        def score_body(kc, _):
            score_chunk(kc, False)
            return 0

        lax.fori_loop(0, i, score_body, 0)
        score_chunk(i, True)

        def count(pred):
            def body(kc, acc):
                c0 = pl.multiple_of(kc * t, t)
                return acc + _fold8(jnp.where(pred(key_s[pl.ds(c0, t), :]), 1.0, 0.0), jnp.sum)
            acc = lax.fori_loop(0, nk, body, jnp.zeros((SUBLANES, t), F32))
            return jnp.sum(acc, axis=0, keepdims=True)

        def search(p, thr):
            cand = thr ^ lax.shift_left(jnp.int32(1), 31 - p)
            return jnp.where(count(lambda k: k >= cand) >= top_k, cand, thr)

        thr = lax.fori_loop(0, 32, search, jnp.full((1, t), INT_MIN, I32))
        thr = jnp.maximum(thr, INT_MIN + 1)

        def mask_body(kc, _):
            c0 = pl.multiple_of(kc * t, t)
            mb_s[pl.ds(c0, t), :] = jnp.where(key_s[pl.ds(c0, t), :] >= thr, 0.0, NEG)
            return 0

        lax.fori_loop(0, nk, mask_body, 0)

        n_ge = count(lambda k: k >= thr)

        @pl.when(jnp.max(n_ge) > top_k)
        def _():
            room = top_k - count(lambda k: k > thr)

            def tie_body(kc, seen):
                c0 = pl.multiple_of(kc * t, t)
                tie = key_s[pl.ds(c0, t), :] == thr
                rank = seen + _dot(tri_ref[...], jnp.where(tie, 1.0, 0.0).astype(BF16))
                drop = tie & (rank > room)
                mb_s[pl.ds(c0, t), :] = jnp.where(drop, NEG, mb_s[pl.ds(c0, t), :])
                return rank[t - 1:t, :]

            lax.fori_loop(0, nk, tie_body, jnp.zeros((1, t), F32))

        pos = r0 + row_iota
        t_hi = lax.shift_right_logical(pos, 6).astype(F32)
        t_lo = (pos & (POS_SPLIT - 1)).astype(F32)
        for h in range(n_heads):
            e = h % 2
            first = 0 if e else HEAD_DIM
            feat = jnp.where(lane == first, float(POS_SPLIT),
                             jnp.where(lane == first + 1, 1.0,
                                       jnp.where(lane == first + 2, -float(POS_SPLIT) * t_hi,
                                                 jnp.where(lane == first + 3, -t_lo, 0.0))))
            half = (lane >= HEAD_DIM) if e else (lane < HEAD_DIM)
            q2 = qb_s[rows, (h // 2) * LANES:(h // 2 + 1) * LANES]
            qa_s[h] = jnp.where(half, q2, (slopes[h] * feat).astype(BF16))
            m_s[h] = jnp.full((1, t), NEG, F32)
            l_s[h] = jnp.zeros((SUBLANES, t), F32)
            acc_s[h] = jnp.zeros((LANES, t), F32)

        def attend(kc, _):
            c0 = pl.multiple_of(kc * t, t)
            cols = pl.ds(c0, t)
            states = [(m_s[h], l_s[h], acc_s[h]) for h in range(n_heads)]
            mask = mb_s[cols, :]
            k_ops = (ka_s[cols, :], kb_s[cols, :])
            v_ts = (vkt_s[:, cols], kvt_s[:, cols])
            s_ts = [_dot_nt(k_ops[h % 2], qa_s[h]) + mask for h in range(n_heads)]
            v_list = [v_ts[h % 2] for h in range(n_heads)]
            for h, st in enumerate(_online_update_heads(states, s_ts, v_list)):
                m_s[h], l_s[h], acc_s[h] = st
            return 0

        lax.fori_loop(0, nk, attend, 0)

        for hp in range(n_heads // 2):
            st = [(m_s[h], l_s[h], acc_s[h]) for h in (2 * hp, 2 * hp + 1)]
            cols = slice(hp * LANES, (hp + 1) * LANES)
            out = _finish_pair(*st) * _silu(z_ref[0, rows, cols].astype(F32))
            o_ref[0, rows, cols] = out.astype(o_ref.dtype)
        return 0

    lax.fori_loop(0, seq // t, q_block, 0)


def _dsa_attention(side, main, w, tile, tri, batch, seq):
    assert seq // POS_SPLIT <= 256
    top_k = min(IDX_TOPK_MAX, seq // 4)
    slopes = tuple(float(2.0 ** (-8.0 * (h + 1) / B_HEADS)) for h in range(B_HEADS))
    assert all(np.log2(s) == np.round(np.log2(s)) for s in slopes)
    t = ATT_T
    const = lambda a: pl.BlockSpec(a.shape, lambda b: (0,) * a.ndim)
    weights = [w["g_cq"], w["w_uq"], w["w_uq_idx"], w["g_ckv"], w["w_kv"], w["w_vk"], w["w_kv_t"], w["w_vk_t"],
               tile, tri]
    return pl.pallas_call(
        functools.partial(_dsa_kernel, top_k=top_k, slopes=slopes),
        grid=(batch,),
        in_specs=[pl.BlockSpec((1, seq, 512), lambda b: (b, 0, 0)),
                  pl.BlockSpec((1, seq, B_W), lambda b: (b, 0, MAIN_ZB // B_W))]
                 + [const(a) for a in weights],
        out_specs=pl.BlockSpec((1, seq, B_W), lambda b: (b, 0, 0)),
        out_shape=jax.ShapeDtypeStruct((batch, seq, B_W), BF16),
        scratch_shapes=[pltpu.VMEM((seq, B_W), BF16),
                        pltpu.VMEM((seq, IDX_HEADS * IDX_DIM), BF16),
                        pltpu.VMEM((seq, IDX_HEADS * IDX_DIM), BF16),
                        pltpu.VMEM((seq, LANES), BF16),
                        pltpu.VMEM((seq, LANES), BF16),
                        pltpu.VMEM((LANES, seq), BF16),
                        pltpu.VMEM((LANES, seq), BF16),
                        pltpu.VMEM((LANES, seq), F32),
                        pltpu.VMEM((IDX_HEADS, t, IDX_HEADS * IDX_DIM), BF16),
                        pltpu.VMEM((B_HEADS, t, LANES), BF16),
                        pltpu.VMEM((seq, t), I32),
                        pltpu.VMEM((seq, t), F32),
                        pltpu.VMEM((B_HEADS, 1, t), F32),
                        pltpu.VMEM((B_HEADS, SUBLANES, t), F32),
                        pltpu.VMEM((B_HEADS, LANES, t), F32)],
        compiler_params=_cparams(1),
        name="dsa_attention",
    )(side, main, *weights)


C_UNROLL = 4


def _dilated_kernel(q_ref, k_ref, v_ref, bias_ref, z0_ref, z1_ref, z2_ref, o_ref, o_s, l_s):
    seq = q_ref.shape[1]
    sp = pl.program_id(1)
    grp = pl.program_id(2)
    lane = lax.broadcasted_iota(I32, (1, LANES), 1)
    cw = CLASS_WINDOW

    def rows(start, d):
        return pl.ds(start, cw) if d == 1 else pl.ds(start, cw, stride=d)

    def group_body(g, d):
        per_class = seq // (d * cw)

        def load_block(n):
            r = n // per_class
            ib = n % per_class
            start = r + ib * (cw * d)
            pstart = r + jnp.maximum(ib - 1, 0) * (cw * d)
            ld = lambda ref, s0: ref[0, rows(s0, d), :].astype(BF16)
            return dict(ib=ib, start=start, q=ld(q_ref, start), kc=ld(k_ref, start), vc=ld(v_ref, start),
                        kp=ld(k_ref, pstart), vp=ld(v_ref, pstart))

        def blk(it, _):
            blocks = [load_block(it * C_UNROLL + u) for u in range(C_UNROLL)]
            logits = []
            for b in blocks:
                for e in range(2):
                    half = (lane >= HEAD_DIM) if e else (lane < HEAD_DIM)
                    qm = jnp.where(half, b["q"], jnp.zeros_like(b["q"]))
                    s_cur = _dot_nt(qm, b["kc"]) + bias_ref[0, e, :, cw:]
                    s_prev = _dot_nt(qm, b["kp"]) + jnp.where(b["ib"] > 0, bias_ref[0, e, :, :cw], NEG)
                    logits.append((s_cur, s_prev))
            probs = []
            for s_cur, s_prev in logits:
                m = jnp.max(jnp.maximum(s_cur, s_prev), axis=-1, keepdims=True)
                p_cur = jnp.exp(s_cur - m)
                p_prev = jnp.exp(s_prev - m)
                l = jnp.sum(p_cur + p_prev, axis=-1, keepdims=True)
                probs.append((p_cur.astype(BF16), p_prev.astype(BF16), 1.0 / l, m + jnp.log(l)))
            for u, b in enumerate(blocks):
                outs = []
                for e in range(2):
                    p_cur, p_prev, inv_l, _ = probs[2 * u + e]
                    outs.append((_dot(p_cur, b["vc"]) + _dot(p_prev, b["vp"])) * inv_l)
                o_s[g, rows(b["start"], d), :] = jnp.where(lane < HEAD_DIM, outs[0], outs[1])
                l_s[g, rows(b["start"], d), :] = jnp.where(lane < HEAD_DIM, probs[2 * u][3], probs[2 * u + 1][3])
            return 0

        lax.fori_loop(0, seq // (cw * C_UNROLL), blk, 0)

    for g, (_, d) in enumerate(C_GROUPS):
        pl.when(grp == g)(functools.partial(group_body, g, d))

    n_grp = len(C_GROUPS)
    z_refs = (z0_ref, z1_ref, z2_ref)

    def combine(sp_static):
        def chunk(c, _):
            r0 = pl.multiple_of(c * 256, 256)
            ls = [l_s[g, pl.ds(r0, 256), :] for g in range(n_grp)]
            m = functools.reduce(jnp.maximum, ls)
            ws = [jnp.exp(x - m) for x in ls]
            inv = 1.0 / functools.reduce(lambda a, b: a + b, ws)
            for g in range(n_grp):
                col = (g * 2 + sp_static) * LANES
                y = (ws[g] * inv) * o_s[g, pl.ds(r0, 256), :]
                y = y * _silu(z_refs[g][0, pl.ds(r0, 256), :].astype(F32))
                o_ref[0, pl.ds(r0, 256), col:col + LANES] = y.astype(o_ref.dtype)
            return 0

        lax.fori_loop(0, seq // 256, chunk, 0)

    for s in range(2):
        pl.when((grp == n_grp - 1) & (sp == s))(functools.partial(combine, s))


def _dilated_attention(side, main, bias, batch, seq):
    assert seq % (C_GROUPS[-1][1] * CLASS_WINDOW) == 0
    assert (seq // CLASS_WINDOW) % C_UNROLL == 0
    n_grp = len(C_GROUPS)
    blk = lambda base: pl.BlockSpec(
        (1, seq, LANES), lambda b, s, g, base=base: (b, 0, base // LANES + 2 * g + s))
    zblk = lambda g: pl.BlockSpec(
        (1, seq, LANES), lambda b, s, _, g=g: (b, 0, MAIN_ZC // LANES + 2 * g + s))
    return pl.pallas_call(
        _dilated_kernel,
        grid=(batch, 2, n_grp),
        in_specs=[blk(SIDE_QC), blk(SIDE_KC), blk(SIDE_VC),
                  pl.BlockSpec((1, 2, CLASS_WINDOW, 2 * CLASS_WINDOW), lambda b, s, g: (2 * g + s, 0, 0, 0)),
                  zblk(0), zblk(1), zblk(2)],
        out_specs=pl.BlockSpec((1, seq, C_W), lambda b, s, g: (b, 0, 0)),
        out_shape=jax.ShapeDtypeStruct((batch, seq, C_W), BF16),
        scratch_shapes=[pltpu.VMEM((n_grp, seq, LANES), F32),
                        pltpu.VMEM((n_grp, seq, LANES), F32)],
        compiler_params=_cparams(3),
        name="dilated_attention",
    )(side, side, side, bias, main, main, main)


def _dilated_bias():
    cw = CLASS_WINDOW
    slopes = np.exp2(-8.0 * (np.arange(C_HEADS, dtype=np.float32) + 1.0) / C_HEADS).astype(np.float32)
    i = np.arange(cw)[:, None]
    j = np.arange(cw)[None, :]
    out = np.empty((C_HEADS // 2, 2, cw, 2 * cw), np.float32)
    for h in range(C_HEADS):
        d = C_GROUPS[h // C_HEADS_PER_GROUP][1]
        prev = np.where(j >= i, -slopes[h] * np.float32(d) * (cw + i - j).astype(np.float32), NEG)
        cur = np.where(j <= i, -slopes[h] * np.float32(d) * (i - j).astype(np.float32), NEG)
        out[h // 2, h % 2] = np.concatenate([prev, cur], axis=1)
    return jnp.asarray(out)


def _merge_kernel(x_ref, ya_ref, yb_ref, yc_ref, g_ref, wa_ref, wb_ref, wc_ref, wo_ref, fg_ref, o_ref,
                  *, final):
    gate = lambda n: _sigmoid(g_ref[:, n * D_MODEL:(n + 1) * D_MODEL].astype(F32))
    merged = (gate(0) * _dot(ya_ref[...], wa_ref[...])
              + gate(1) * _dot(yb_ref[...], wb_ref[...])
              + gate(2) * _dot(yc_ref[...], wc_ref[...]))
    y = x_ref[...] + _dot(merged.astype(BF16), wo_ref[...])
    if final:
        y = _rms(y, fg_ref[...])
    o_ref[...] = y


def _merge(x2d, ya, yb, yc, main, w_a, w_b, w_c, w_o, final_g, final, tm=512):
    m = x2d.shape[0]
    row = lambda w: pl.BlockSpec((tm, w), lambda i: (i, 0))
    const = lambda a: pl.BlockSpec(a.shape, lambda i: (0, 0))
    return pl.pallas_call(
        functools.partial(_merge_kernel, final=final),
        grid=(m // tm,),
        in_specs=[row(D_MODEL), row(A_W), row(B_W), row(C_W), row(3 * D_MODEL),
                  const(w_a), const(w_b), const(w_c), const(w_o), const(final_g)],
        out_specs=row(D_MODEL),
        out_shape=jax.ShapeDtypeStruct((m, D_MODEL), F32),
        compiler_params=_cparams(1),
        name="gated_merge",
    )(x2d, ya, yb, yc, main, w_a, w_b, w_c, w_o, final_g)


def _layer_weights(norm_g, w_in, b_forget, g_cq, w_uq, w_uq_idx, g_ckv, w_ukv, w_a, w_b, w_c, w_o):
    points = np.cumsum(IN_WIDTHS)[:-1].tolist()
    (wqa, wka, wva, wfa, wza, wcq, wckv, wkidx, wwidx, wzb,
     wqc, wkc, wvc, wzc, wga, wgb, wgc) = jnp.split(w_in, points, axis=1)
    zeros = lambda n: jnp.zeros((D_MODEL, n), F32)
    w_main = jnp.concatenate([wga, wgb, wgc, wzb, wqa * SCALE, wka, wva, wza, wzc], axis=1)
    w_side = jnp.concatenate(
        [wcq, wckv, wkidx, wwidx, zeros(LANES - IDX_DIM - IDX_HEADS),
         wfa, zeros(LANES - A_HEADS), wqc * SCALE, wkc, wvc, zeros(SIDE_N - SIDE_VC - C_W)], axis=1)
    assert w_main.shape[1] == MAIN_N and w_side.shape[1] == SIDE_N
    wk, wv = w_ukv[:, :HEAD_DIM], w_ukv[:, HEAD_DIM:]
    w_kv = jnp.concatenate([wk, wv], axis=1).astype(BF16)
    w_vk = jnp.concatenate([wv, wk], axis=1).astype(BF16)
    return dict(
        norm_g=norm_g.reshape(1, D_MODEL),
        w_main=w_main.astype(BF16), w_side=w_side.astype(BF16),
        b_pad=jnp.zeros((1, LANES), F32).at[0, :A_HEADS].set(b_forget),
        g_cq=g_cq.reshape(1, B_Q_RANK), w_uq=(w_uq * SCALE).astype(BF16), w_uq_idx=w_uq_idx.astype(BF16),
        g_ckv=g_ckv.reshape(1, B_KV_RANK), w_kv=w_kv, w_vk=w_vk, w_kv_t=w_kv.T, w_vk_t=w_vk.T,
        w_a=w_a.astype(BF16), w_b=w_b.astype(BF16), w_c=w_c.astype(BF16), w_o=w_o.astype(BF16))


def _constants():
    r = np.arange(LANES)
    tri128 = (r[None, :] <= r[:, None]).astype(np.float32)
    r2 = np.arange(ATT_T)
    tri_t = (r2[None, :] <= r2[:, None]).astype(np.float32)
    c = np.arange(IDX_HEADS * IDX_DIM)
    tile = ((r[:, None] == c[None, :] % IDX_DIM) & (r[:, None] < IDX_DIM)).astype(np.float32)
    selq = np.zeros((A_HEADS, 3 * LANES, LANES), np.float32)
    selk = np.zeros((A_HEADS, 3 * LANES, LANES), np.float32)
    for h in range(A_HEADS):
        for piece in range(3):
            selq[h, piece * LANES + h, piece] = 1.0
            selk[h, piece * LANES + h, 3 + piece] = -1.0
    oneq = np.zeros((1, LANES), np.float32)
    oneq[0, 3:6] = 1.0
    onek = np.zeros((1, LANES), np.float32)
    onek[0, 0:3] = 1.0
    bf = lambda a: jnp.asarray(a, BF16)
    cum = (bf(tri128), bf(selq), bf(selk), jnp.asarray(oneq), jnp.asarray(onek))
    return cum, bf(tri_t), bf(tile), _dilated_bias()


def _hybrid_layer(x2d, batch, seq, w, consts, final_g, final):
    cum_consts, tri_t, tile, bias_c = consts
    main = _rms_matmul(x2d, w["norm_g"], w["w_main"], BF16, tm=1024, tn=1280)
    side = _rms_matmul(x2d, w["norm_g"], w["w_side"], F32, tm=1024, tn=1536)
    main3 = main.reshape(batch, seq, MAIN_N)
    side3 = side.reshape(batch, seq, SIDE_N)
    qf, kf = _cum_forget(side3, w["b_pad"], cum_consts, batch, seq)
    ya = _fox_attention(main3, qf, kf, batch, seq)
    yb = _dsa_attention(side3, main3, w, tile, tri_t, batch, seq)
    yc = _dilated_attention(side3, main3, bias_c, batch, seq)
    m = batch * seq
    return _merge(x2d, ya.reshape(m, A_W), yb.reshape(m, B_W), yc.reshape(m, C_W), main,
                  w["w_a"], w["w_b"], w["w_c"], w["w_o"], final_g, final)


def kernel(x, norm_g, w_in, b_forget, g_cq, w_uq, w_uq_idx, g_ckv, w_ukv, w_a, w_b, w_c, w_o, final_g):
    batch, seq, d_model = x.shape
    assert d_model == D_MODEL and seq % ATT_T == 0
    depth = norm_g.shape[0]
    consts = _constants()
    fg = final_g.reshape(1, D_MODEL)
    x2d = x.reshape(batch * seq, D_MODEL)
    for l in range(depth):
        w = _layer_weights(norm_g[l], w_in[l], b_forget[l], g_cq[l], w_uq[l], w_uq_idx[l], g_ckv[l],
                           w_ukv[l], w_a[l], w_b[l], w_c[l], w_o[l])
        x2d = _hybrid_layer(x2d, batch, seq, w, consts, fg, final=(l == depth - 1))
    return x2d.reshape(batch, seq, D_MODEL)
```

```python
import functools

import numpy as np
import jax
import jax.numpy as jnp
from jax import lax
from jax.experimental import pallas as pl
from jax.experimental.pallas import tpu as pltpu

F32 = jnp.float32
BF16 = jnp.bfloat16
I32 = jnp.int32

D_MODEL = 1024
HEAD_DIM = 64
EPS = 1e-6
A_HEADS = 8
A_W = A_HEADS * HEAD_DIM
B_HEADS = 8
B_W = B_HEADS * HEAD_DIM
B_Q_RANK = 256
B_KV_RANK = 128
IDX_HEADS = 8
IDX_DIM = 32
IDX_TOPK_MAX = 256
C_GROUPS = ((128, 1), (512, 4), (2048, 16))
C_HEADS_PER_GROUP = 4
C_HEADS = C_HEADS_PER_GROUP * len(C_GROUPS)
C_W = C_HEADS * HEAD_DIM
IN_WIDTHS = (A_W, A_W, A_W, A_HEADS, A_W,
             B_Q_RANK, B_KV_RANK, IDX_DIM, IDX_HEADS, B_W,
             C_W, C_W, C_W, C_W,
             D_MODEL, D_MODEL, D_MODEL)

LANES = 128
SUBLANES = 8
SCALE = HEAD_DIM ** -0.5
NEG = -1e30
INT_MIN = -2 ** 31
CLASS_WINDOW = 128
assert all(w // d == CLASS_WINDOW for w, d in C_GROUPS)
ATT_T = 256
FOX_T = 512
POS_SPLIT = 64

MAIN_G = 0
MAIN_ZB = 3 * D_MODEL
MAIN_QA = MAIN_ZB + B_W
MAIN_KA = MAIN_QA + A_W
MAIN_VA = MAIN_KA + A_W
MAIN_ZA = MAIN_VA + A_W
MAIN_ZC = MAIN_ZA + A_W
MAIN_N = MAIN_ZC + C_W
SIDE_CQ = 0
SIDE_MISC = B_Q_RANK + B_KV_RANK
SIDE_WIDX = IDX_DIM
SIDE_FA = 512
SIDE_QC = SIDE_FA + LANES
SIDE_KC = SIDE_QC + C_W
SIDE_VC = SIDE_KC + C_W
SIDE_N = 3072

VMEM_LIMIT = 56 * 1024 * 1024


def _cparams(n_axes, vmem=VMEM_LIMIT):
    return pltpu.CompilerParams(dimension_semantics=("arbitrary",) * n_axes,
                                vmem_limit_bytes=vmem)


def _dot(a, b):
    return jnp.dot(a, b, preferred_element_type=F32)


def _dot_nt(a, b):
    return lax.dot_general(a, b, (((1,), (1,)), ((), ())), preferred_element_type=F32)


def _sigmoid(x):
    return 1.0 / (1.0 + jnp.exp(-x))


def _silu(x):
    return x * _sigmoid(x)


def _rms(x, g):
    return x * lax.rsqrt(jnp.mean(x * x, axis=-1, keepdims=True) + EPS) * g


def _split3(x):
    hi = x.astype(BF16)
    r1 = x - hi.astype(F32)
    mid = r1.astype(BF16)
    lo = (r1 - mid.astype(F32)).astype(BF16)
    return hi, mid, lo


def _fold8(x, op):
    n, t = x.shape
    return op(x.reshape(n // SUBLANES, SUBLANES, t), axis=0)


def _tree_fold8(x, op):
    n, t = x.shape
    parts = [x[r:r + SUBLANES, :] for r in range(0, n, SUBLANES)]
    while len(parts) > 1:
        nxt = [op(parts[j], parts[j + 1]) for j in range(0, len(parts) - 1, 2)]
        if len(parts) % 2:
            nxt.append(parts[-1])
        parts = nxt
    return parts[0]


def _keys_max(x):
    return jnp.max(_fold8(x, jnp.max), axis=0, keepdims=True)


def _online_update_heads(states, s_ts, v_ts):
    ps, scaled = [], []
    for (m, l8, acc), s_t in zip(states, s_ts):
        m_new = jnp.maximum(m, _keys_max(s_t))
        alpha = jnp.exp(m - m_new)
        p = jnp.exp(s_t - m_new)
        ps.append(p.astype(BF16))
        scaled.append((m_new, alpha * l8 + _fold8(p, jnp.sum), alpha * acc))
    return [(m, l8, acc + _dot(v_t, p)) for (m, l8, acc), v_t, p in zip(scaled, v_ts, ps)]


def _online_init(t):
    return (jnp.full((1, t), NEG, F32), jnp.zeros((SUBLANES, t), F32), jnp.zeros((LANES, t), F32))


def _finish_pair(state0, state1):
    row = lax.broadcasted_iota(I32, (LANES, 1), 0)
    norm = lambda st: st[2] * (1.0 / jnp.sum(st[1], axis=0, keepdims=True))
    return jnp.where(row < HEAD_DIM, norm(state0), norm(state1)).T


def _rms_matmul_kernel(x_ref, g_ref, w_ref, o_ref, h_ref):
    @pl.when(pl.program_id(1) == 0)
    def _():
        h_ref[...] = _rms(x_ref[...], g_ref[...]).astype(BF16)

    o_ref[...] = _dot(h_ref[...], w_ref[...]).astype(o_ref.dtype)


def _rms_matmul(x2d, g, w, out_dtype, tm, tn):
    m, k = x2d.shape
    n = w.shape[1]
    return pl.pallas_call(
        _rms_matmul_kernel,
        grid=(m // tm, n // tn),
        in_specs=[pl.BlockSpec((tm, k), lambda i, j: (i, 0)),
                  pl.BlockSpec((1, k), lambda i, j: (0, 0)),
                  pl.BlockSpec((k, tn), lambda i, j: (0, j))],
        out_specs=pl.BlockSpec((tm, tn), lambda i, j: (i, j)),
        out_shape=jax.ShapeDtypeStruct((m, n), out_dtype),
        scratch_shapes=[pltpu.VMEM((tm, k), BF16)],
        compiler_params=_cparams(2),
        name="rms_in_proj",
    )(x2d, g, w)


def _cum_kernel(fa_ref, b_ref, tri_ref, sel_ref, one_ref, qf_ref, kf_ref):
    seq = fa_ref.shape[1]
    tri = tri_ref[...]
    lane = lax.broadcasted_iota(I32, (1, LANES), 1)
    carry = jnp.zeros((1, LANES), F32)
    for blk in range(seq // LANES):
        rows = slice(blk * LANES, (blk + 1) * LANES)
        x = fa_ref[0, rows, :] + b_ref[...]
        lf = jnp.minimum(x, 0.0) - jnp.log(1.0 + jnp.exp(-jnp.abs(x)))
        hi, mid, lo = _split3(lf)
        c = _dot(tri, hi) + _dot(tri, mid) + _dot(tri, lo) + carry
        carry = c[LANES - 1:LANES, :]
        hi, mid, lo = _split3(jnp.where(lane < A_HEADS, c, 0.0))
        c3 = (hi.astype(F32) + pltpu.roll(mid.astype(F32), A_HEADS, 1)
              + pltpu.roll(lo.astype(F32), 2 * A_HEADS, 1)).astype(BF16)
        for h in range(A_HEADS):
            feats = _dot(c3, sel_ref[h]) + one_ref[...]
            qf_ref[0, h, rows, :] = feats[:, :LANES].astype(BF16)
            kf_ref[0, h, rows, :] = feats[:, LANES:].astype(BF16)


def _cum_forget(side, b_pad, consts, batch, seq):
    tri, sel, one = consts
    full = lambda a: pl.BlockSpec(a.shape, lambda b: (0,) * a.ndim)
    feat = jax.ShapeDtypeStruct((batch, A_HEADS, seq, LANES), BF16)
    return pl.pallas_call(
        _cum_kernel,
        grid=(batch,),
        in_specs=[pl.BlockSpec((1, seq, LANES), lambda b: (b, 0, SIDE_FA // LANES)),
                  full(b_pad), full(tri), full(sel), full(one)],
        out_specs=[pl.BlockSpec((1, A_HEADS, seq, LANES), lambda b: (b, 0, 0, 0))] * 2,
        out_shape=[feat, feat],
        compiler_params=_cparams(1),
        name="fox_cumsum",
    )(side, b_pad, tri, sel, one)


def _fox_kernel(q_ref, k_ref, v_ref, z_ref, qf_ref, kf_ref, o_ref, vt_s, qa_s, m_s, l_s, acc_s):
    seq = q_ref.shape[1]
    t = FOX_T
    n_heads = A_HEADS
    lane = lax.broadcasted_iota(I32, (1, LANES), 1)
    key_le_query = (lax.broadcasted_iota(I32, (t, t), 0) <= lax.broadcasted_iota(I32, (t, t), 1))
    pair_cols = lambda h: slice((h // 2) * LANES, (h // 2 + 1) * LANES)

    def transpose_v(c, _):
        cols = pl.ds(pl.multiple_of(c * t, t), t)
        for p in range(n_heads // 2):
            pc = slice(p * LANES, (p + 1) * LANES)
            vt_s[pc, cols] = v_ref[0, cols, pc].astype(F32).T.astype(BF16)
        return 0

    lax.fori_loop(0, seq // t, transpose_v, 0)

    def q_block(i, _):
        rows = pl.ds(pl.multiple_of(i * t, t), t)
        for h in range(n_heads):
            half = (lane >= HEAD_DIM) if h % 2 else (lane < HEAD_DIM)
            q2 = q_ref[0, rows, pair_cols(h)]
            qa_s[h] = jnp.concatenate([jnp.where(half, q2, jnp.zeros_like(q2)), qf_ref[0, h, rows, :]], axis=1)
            m_s[h], l_s[h], acc_s[h] = _online_init(t)

        def chunk(kb, diagonal):
            cols = pl.ds(pl.multiple_of(kb * t, t), t)
            states = [(m_s[h], l_s[h], acc_s[h]) for h in range(n_heads)]
            s_ts = [_dot_nt(jnp.concatenate([k_ref[0, cols, pair_cols(h)], kf_ref[0, h, cols, :]], axis=1),
                            qa_s[h]) for h in range(n_heads)]
            if diagonal:
                s_ts = [jnp.where(key_le_query, s_t, NEG) for s_t in s_ts]
            v_ts = [vt_s[pair_cols(h), cols] for h in range(n_heads)]
            for h, st in enumerate(_online_update_heads(states, s_ts, v_ts)):
                m_s[h], l_s[h], acc_s[h] = st

        def off_diagonal(kb, _):
            chunk(kb, False)
            return 0

        lax.fori_loop(0, i, off_diagonal, 0)
        chunk(i, True)
        for p in range(n_heads // 2):
            st = [(m_s[h], l_s[h], acc_s[h]) for h in (2 * p, 2 * p + 1)]
            out = _finish_pair(*st) * _silu(z_ref[0, rows, pair_cols(2 * p)].astype(F32))
            o_ref[0, rows, pair_cols(2 * p)] = out.astype(o_ref.dtype)
        return 0

    lax.fori_loop(0, seq // t, q_block, 0)


def _fox_attention(main, qf, kf, batch, seq):
    t = FOX_T
    blk = lambda base: pl.BlockSpec((1, seq, A_W), lambda b, base=base: (b, 0, base // A_W))
    feat = pl.BlockSpec((1, A_HEADS, seq, LANES), lambda b: (b, 0, 0, 0))
    return pl.pallas_call(
        _fox_kernel,
        grid=(batch,),
        in_specs=[blk(MAIN_QA), blk(MAIN_KA), blk(MAIN_VA), blk(MAIN_ZA), feat, feat],
        out_specs=pl.BlockSpec((1, seq, A_W), lambda b: (b, 0, 0)),
        out_shape=jax.ShapeDtypeStruct((batch, seq, A_W), BF16),
        scratch_shapes=[pltpu.VMEM((A_W, seq), BF16),
                        pltpu.VMEM((A_HEADS, t, 2 * LANES), BF16),
                        pltpu.VMEM((A_HEADS, 1, t), F32),
                        pltpu.VMEM((A_HEADS, SUBLANES, t), F32),
                        pltpu.VMEM((A_HEADS, LANES, t), F32)],
        compiler_params=_cparams(1),
        name="fox_attention",
    )(main, main, main, main, qf, kf)


def _pos_features(pos, lane, first):
    hi = lax.shift_right_logical(pos, 6).astype(F32)
    lo = (pos & (POS_SPLIT - 1)).astype(F32)
    return jnp.where(lane == first, hi,
                     jnp.where(lane == first + 1, lo,
                               jnp.where((lane == first + 2) | (lane == first + 3), 1.0, 0.0)))


def _dsa_kernel(side_ref, z_ref, gcq_ref, wuq_ref, wuqi_ref, gckv_ref, wkv_ref, wvk_ref, wkvt_ref, wvkt_ref,
                tile_ref, tri_ref, o_ref,
                qb_s, qi_s, kt_s, ka_s, kb_s, kvt_s, vkt_s, wt_s, qm_s, qa_s, key_s, mb_s, m_s, l_s, acc_s,
                *, top_k, slopes):
    seq = side_ref.shape[1]
    t = ATT_T
    n_heads = B_HEADS
    lane = lax.broadcasted_iota(I32, (1, LANES), 1)
    lane2 = lax.broadcasted_iota(I32, (1, 2 * LANES), 1)
    key_le_query = (lax.broadcasted_iota(I32, (t, t), 0) <= lax.broadcasted_iota(I32, (t, t), 1))
    row_iota = lax.broadcasted_iota(I32, (t, 1), 0)

    def prep(c, _):
        r0 = pl.multiple_of(c * t, t)
        rows = pl.ds(r0, t)
        blk = side_ref[0, rows, :]
        cqn = _rms(blk[:, SIDE_CQ:SIDE_CQ + B_Q_RANK], gcq_ref[...]).astype(BF16)
        qb_s[rows, :] = _dot(cqn, wuq_ref[...]).astype(BF16)
        qi_s[rows, :] = _dot(cqn, wuqi_ref[...]).astype(BF16)
        kvn = _rms(blk[:, B_Q_RANK:SIDE_MISC], gckv_ref[...]).astype(BF16)
        pos = r0 + row_iota
        ka_s[rows, :] = jnp.where(lane < HEAD_DIM, _dot(kvn, wkv_ref[...]),
                                  _pos_features(pos, lane, HEAD_DIM)).astype(BF16)
        kb_s[rows, :] = jnp.where(lane >= HEAD_DIM, _dot(kvn, wvk_ref[...]),
                                  _pos_features(pos, lane, 0)).astype(BF16)
        kvt_s[:, rows] = _dot_nt(wkvt_ref[...], kvn).astype(BF16)
        vkt_s[:, rows] = _dot_nt(wvkt_ref[...], kvn).astype(BF16)
        misc = blk[:, SIDE_MISC:]
        kt_s[rows, :] = _dot(misc.astype(BF16), tile_ref[...]).astype(BF16)
        wt_s[:, rows] = misc.T
        return 0

    lax.fori_loop(0, seq // t, prep, 0)

    def q_block(i, _):
        r0 = pl.multiple_of(i * t, t)
        rows = pl.ds(r0, t)
        nk = i + 1

        qi = qi_s[rows, :]
        for h in range(IDX_HEADS):
            in_head = (lane2 >= h * IDX_DIM) & (lane2 < (h + 1) * IDX_DIM)
            qm_s[h] = jnp.where(in_head, qi, jnp.zeros_like(qi))
        w_rows = [wt_s[SIDE_WIDX + h:SIDE_WIDX + h + 1, rows] for h in range(IDX_HEADS)]

        def score_chunk(kc, diagonal):
            c0 = pl.multiple_of(kc * t, t)
            kt = kt_s[pl.ds(c0, t), :]
            acc = jnp.zeros((t, t), F32)
            for h in range(IDX_HEADS):
                acc = acc + w_rows[h] * jnp.maximum(_dot_nt(kt, qm_s[h]), 0.0)
            bits = lax.bitcast_convert_type(acc, I32)
            key = bits ^ (lax.shift_right_arithmetic(bits, 31) & 0x7FFFFFFF)
            key = jnp.where(key == -1, 0, key)
            if diagonal:
                key = jnp.where(key_le_query, key, INT_MIN)
            key_s[pl.ds(c0, t), :] = key

        def score_body(kc, _):
            score_chunk(kc, False)
            return 0

        lax.fori_loop(0, i, score_body, 0)
        score_chunk(i, True)

        def count(pred):
            def body(kc, acc):
                c0 = pl.multiple_of(kc * t, t)
                return acc + _tree_fold8(jnp.where(pred(key_s[pl.ds(c0, t), :]), 1.0, 0.0), jnp.add)
            acc = lax.fori_loop(0, nk, body, jnp.zeros((SUBLANES, t), F32))
            return jnp.sum(acc, axis=0, keepdims=True)

        def search(p, thr):
            cand = thr ^ lax.shift_left(jnp.int32(1), 31 - p)
            return jnp.where(count(lambda k: k >= cand) >= top_k, cand, thr)

        thr = lax.fori_loop(0, 32, search, jnp.full((1, t), INT_MIN, I32))
        thr = jnp.maximum(thr, INT_MIN + 1)

        def mask_body(kc, _):
            c0 = pl.multiple_of(kc * t, t)
            mb_s[pl.ds(c0, t), :] = jnp.where(key_s[pl.ds(c0, t), :] >= thr, 0.0, NEG)
            return 0

        lax.fori_loop(0, nk, mask_body, 0)

        n_ge = count(lambda k: k >= thr)

        @pl.when(jnp.max(n_ge) > top_k)
        def _():
            room = top_k - count(lambda k: k > thr)

            def tie_body(kc, seen):
                c0 = pl.multiple_of(kc * t, t)
                tie = key_s[pl.ds(c0, t), :] == thr
                rank = seen + _dot(tri_ref[...], jnp.where(tie, 1.0, 0.0).astype(BF16))
                drop = tie & (rank > room)
                mb_s[pl.ds(c0, t), :] = jnp.where(drop, NEG, mb_s[pl.ds(c0, t), :])
                return rank[t - 1:t, :]

            lax.fori_loop(0, nk, tie_body, jnp.zeros((1, t), F32))

        pos = r0 + row_iota
        t_hi = lax.shift_right_logical(pos, 6).astype(F32)
        t_lo = (pos & (POS_SPLIT - 1)).astype(F32)
        for h in range(n_heads):
            e = h % 2
            first = 0 if e else HEAD_DIM
            feat = jnp.where(lane == first, float(POS_SPLIT),
                             jnp.where(lane == first + 1, 1.0,
                                       jnp.where(lane == first + 2, -float(POS_SPLIT) * t_hi,
                                                 jnp.where(lane == first + 3, -t_lo, 0.0))))
            half = (lane >= HEAD_DIM) if e else (lane < HEAD_DIM)
            q2 = qb_s[rows, (h // 2) * LANES:(h // 2 + 1) * LANES]
            qa_s[h] = jnp.where(half, q2, (slopes[h] * feat).astype(BF16))
            m_s[h] = jnp.full((1, t), NEG, F32)
            l_s[h] = jnp.zeros((SUBLANES, t), F32)
            acc_s[h] = jnp.zeros((LANES, t), F32)

        def attend(kc, _):
            c0 = pl.multiple_of(kc * t, t)
            cols = pl.ds(c0, t)
            states = [(m_s[h], l_s[h], acc_s[h]) for h in range(n_heads)]
            mask = mb_s[cols, :]
            k_ops = (ka_s[cols, :], kb_s[cols, :])
            v_ts = (vkt_s[:, cols], kvt_s[:, cols])
            s_ts = [_dot_nt(k_ops[h % 2], qa_s[h]) + mask for h in range(n_heads)]
            v_list = [v_ts[h % 2] for h in range(n_heads)]
            for h, st in enumerate(_online_update_heads(states, s_ts, v_list)):
                m_s[h], l_s[h], acc_s[h] = st
            return 0

        lax.fori_loop(0, nk, attend, 0)

        for hp in range(n_heads // 2):
            st = [(m_s[h], l_s[h], acc_s[h]) for h in (2 * hp, 2 * hp + 1)]
            cols = slice(hp * LANES, (hp + 1) * LANES)
            out = _finish_pair(*st) * _silu(z_ref[0, rows, cols].astype(F32))
            o_ref[0, rows, cols] = out.astype(o_ref.dtype)
        return 0

    lax.fori_loop(0, seq // t, q_block, 0)


def _dsa_attention(side, main, w, tile, tri, batch, seq):
    assert seq // POS_SPLIT <= 256
    top_k = min(IDX_TOPK_MAX, seq // 4)
    slopes = tuple(float(2.0 ** (-8.0 * (h + 1) / B_HEADS)) for h in range(B_HEADS))
    assert all(np.log2(s) == np.round(np.log2(s)) for s in slopes)
    t = ATT_T
    const = lambda a: pl.BlockSpec(a.shape, lambda b: (0,) * a.ndim)
    weights = [w["g_cq"], w["w_uq"], w["w_uq_idx"], w["g_ckv"], w["w_kv"], w["w_vk"], w["w_kv_t"], w["w_vk_t"],
               tile, tri]
    return pl.pallas_call(
        functools.partial(_dsa_kernel, top_k=top_k, slopes=slopes),
        grid=(batch,),
        in_specs=[pl.BlockSpec((1, seq, 512), lambda b: (b, 0, 0)),
                  pl.BlockSpec((1, seq, B_W), lambda b: (b, 0, MAIN_ZB // B_W))]
                 + [const(a) for a in weights],
        out_specs=pl.BlockSpec((1, seq, B_W), lambda b: (b, 0, 0)),
        out_shape=jax.ShapeDtypeStruct((batch, seq, B_W), BF16),
        scratch_shapes=[pltpu.VMEM((seq, B_W), BF16),
                        pltpu.VMEM((seq, IDX_HEADS * IDX_DIM), BF16),
                        pltpu.VMEM((seq, IDX_HEADS * IDX_DIM), BF16),
                        pltpu.VMEM((seq, LANES), BF16),
                        pltpu.VMEM((seq, LANES), BF16),
                        pltpu.VMEM((LANES, seq), BF16),
                        pltpu.VMEM((LANES, seq), BF16),
                        pltpu.VMEM((LANES, seq), F32),
                        pltpu.VMEM((IDX_HEADS, t, IDX_HEADS * IDX_DIM), BF16),
                        pltpu.VMEM((B_HEADS, t, LANES), BF16),
                        pltpu.VMEM((seq, t), I32),
                        pltpu.VMEM((seq, t), F32),
                        pltpu.VMEM((B_HEADS, 1, t), F32),
                        pltpu.VMEM((B_HEADS, SUBLANES, t), F32),
                        pltpu.VMEM((B_HEADS, LANES, t), F32)],
        compiler_params=_cparams(1),
        name="dsa_attention",
    )(side, main, *weights)


C_UNROLL = 8


def _dilated_kernel(q_ref, k_ref, v_ref, bias_ref, z0_ref, z1_ref, z2_ref, o_ref, o_s, l_s):
    seq = q_ref.shape[1]
    sp = pl.program_id(1)
    grp = pl.program_id(2)
    lane = lax.broadcasted_iota(I32, (1, LANES), 1)
    cw = CLASS_WINDOW

    def rows(start, d):
        return pl.ds(start, cw) if d == 1 else pl.ds(start, cw, stride=d)

    def group_body(g, d):
        per_class = seq // (d * cw)

        def load_block(n):
            r = n // per_class
            ib = n % per_class
            start = r + ib * (cw * d)
            pstart = r + jnp.maximum(ib - 1, 0) * (cw * d)
            ld = lambda ref, s0: ref[0, rows(s0, d), :].astype(BF16)
            return dict(ib=ib, start=start, q=ld(q_ref, start), kc=ld(k_ref, start), vc=ld(v_ref, start),
                        kp=ld(k_ref, pstart), vp=ld(v_ref, pstart))

        def blk(it, _):
            blocks = [load_block(it * C_UNROLL + u) for u in range(C_UNROLL)]
            logits = []
            for b in blocks:
                for e in range(2):
                    half = (lane >= HEAD_DIM) if e else (lane < HEAD_DIM)
                    qm = jnp.where(half, b["q"], jnp.zeros_like(b["q"]))
                    s_cur = _dot_nt(qm, b["kc"]) + bias_ref[0, e, :, cw:]
                    s_prev = _dot_nt(qm, b["kp"]) + jnp.where(b["ib"] > 0, bias_ref[0, e, :, :cw], NEG)
                    logits.append((s_cur, s_prev))
            probs = []
            for s_cur, s_prev in logits:
                m = jnp.max(jnp.maximum(s_cur, s_prev), axis=-1, keepdims=True)
                p_cur = jnp.exp(s_cur - m)
                p_prev = jnp.exp(s_prev - m)
                l = jnp.sum(p_cur + p_prev, axis=-1, keepdims=True)
                probs.append((p_cur.astype(BF16), p_prev.astype(BF16), 1.0 / l, m + jnp.log(l)))
            for u, b in enumerate(blocks):
                outs = []
                for e in range(2):
                    p_cur, p_prev, inv_l, _ = probs[2 * u + e]
                    outs.append((_dot(p_cur, b["vc"]) + _dot(p_prev, b["vp"])) * inv_l)
                o_s[g, rows(b["start"], d), :] = jnp.where(lane < HEAD_DIM, outs[0], outs[1])
                l_s[g, rows(b["start"], d), :] = jnp.where(lane < HEAD_DIM, probs[2 * u][3], probs[2 * u + 1][3])
            return 0

        lax.fori_loop(0, seq // (cw * C_UNROLL), blk, 0)

    for g, (_, d) in enumerate(C_GROUPS):
        pl.when(grp == g)(functools.partial(group_body, g, d))

    n_grp = len(C_GROUPS)
    z_refs = (z0_ref, z1_ref, z2_ref)

    def combine(sp_static):
        def chunk(c, _):
            r0 = pl.multiple_of(c * 256, 256)
            ls = [l_s[g, pl.ds(r0, 256), :] for g in range(n_grp)]
            m = functools.reduce(jnp.maximum, ls)
            ws = [jnp.exp(x - m) for x in ls]
            inv = 1.0 / functools.reduce(lambda a, b: a + b, ws)
            for g in range(n_grp):
                col = (g * 2 + sp_static) * LANES
                y = (ws[g] * inv) * o_s[g, pl.ds(r0, 256), :]
                y = y * _silu(z_refs[g][0, pl.ds(r0, 256), :].astype(F32))
                o_ref[0, pl.ds(r0, 256), col:col + LANES] = y.astype(o_ref.dtype)
            return 0

        lax.fori_loop(0, seq // 256, chunk, 0)

    for s in range(2):
        pl.when((grp == n_grp - 1) & (sp == s))(functools.partial(combine, s))


def _dilated_attention(side, main, bias, batch, seq):
    assert seq % (C_GROUPS[-1][1] * CLASS_WINDOW) == 0
    assert (seq // CLASS_WINDOW) % C_UNROLL == 0
    n_grp = len(C_GROUPS)
    blk = lambda base: pl.BlockSpec(
        (1, seq, LANES), lambda b, s, g, base=base: (b, 0, base // LANES + 2 * g + s))
    zblk = lambda g: pl.BlockSpec(
        (1, seq, LANES), lambda b, s, _, g=g: (b, 0, MAIN_ZC // LANES + 2 * g + s))
    return pl.pallas_call(
        _dilated_kernel,
        grid=(batch, 2, n_grp),
        in_specs=[blk(SIDE_QC), blk(SIDE_KC), blk(SIDE_VC),
                  pl.BlockSpec((1, 2, CLASS_WINDOW, 2 * CLASS_WINDOW), lambda b, s, g: (2 * g + s, 0, 0, 0)),
                  zblk(0), zblk(1), zblk(2)],
        out_specs=pl.BlockSpec((1, seq, C_W), lambda b, s, g: (b, 0, 0)),
        out_shape=jax.ShapeDtypeStruct((batch, seq, C_W), BF16),
        scratch_shapes=[pltpu.VMEM((n_grp, seq, LANES), F32),
                        pltpu.VMEM((n_grp, seq, LANES), F32)],
        compiler_params=_cparams(3),
        name="dilated_attention",
    )(side, side, side, bias, main, main, main)


def _dilated_bias():
    cw = CLASS_WINDOW
    slopes = np.exp2(-8.0 * (np.arange(C_HEADS, dtype=np.float32) + 1.0) / C_HEADS).astype(np.float32)
    i = np.arange(cw)[:, None]
    j = np.arange(cw)[None, :]
    out = np.empty((C_HEADS // 2, 2, cw, 2 * cw), np.float32)
    for h in range(C_HEADS):
        d = C_GROUPS[h // C_HEADS_PER_GROUP][1]
        prev = np.where(j >= i, -slopes[h] * np.float32(d) * (cw + i - j).astype(np.float32), NEG)
        cur = np.where(j <= i, -slopes[h] * np.float32(d) * (i - j).astype(np.float32), NEG)
        out[h // 2, h % 2] = np.concatenate([prev, cur], axis=1)
    return jnp.asarray(out)


def _merge_kernel(x_ref, ya_ref, yb_ref, yc_ref, g_ref, wa_ref, wb_ref, wc_ref, wo_ref, fg_ref, o_ref,
                  *, final):
    gate = lambda n: _sigmoid(g_ref[:, n * D_MODEL:(n + 1) * D_MODEL].astype(F32))
    merged = (gate(0) * _dot(ya_ref[...], wa_ref[...])
              + gate(1) * _dot(yb_ref[...], wb_ref[...])
              + gate(2) * _dot(yc_ref[...], wc_ref[...]))
    y = x_ref[...] + _dot(merged.astype(BF16), wo_ref[...])
    if final:
        y = _rms(y, fg_ref[...])
    o_ref[...] = y


def _merge(x2d, ya, yb, yc, main, w_a, w_b, w_c, w_o, final_g, final, tm=512):
    m = x2d.shape[0]
    row = lambda w: pl.BlockSpec((tm, w), lambda i: (i, 0))
    const = lambda a: pl.BlockSpec(a.shape, lambda i: (0, 0))
    return pl.pallas_call(
        functools.partial(_merge_kernel, final=final),
        grid=(m // tm,),
        in_specs=[row(D_MODEL), row(A_W), row(B_W), row(C_W), row(3 * D_MODEL),
                  const(w_a), const(w_b), const(w_c), const(w_o), const(final_g)],
        out_specs=row(D_MODEL),
        out_shape=jax.ShapeDtypeStruct((m, D_MODEL), F32),
        compiler_params=_cparams(1),
        name="gated_merge",
    )(x2d, ya, yb, yc, main, w_a, w_b, w_c, w_o, final_g)


def _layer_weights(norm_g, w_in, b_forget, g_cq, w_uq, w_uq_idx, g_ckv, w_ukv, w_a, w_b, w_c, w_o):
    points = np.cumsum(IN_WIDTHS)[:-1].tolist()
    (wqa, wka, wva, wfa, wza, wcq, wckv, wkidx, wwidx, wzb,
     wqc, wkc, wvc, wzc, wga, wgb, wgc) = jnp.split(w_in, points, axis=1)
    zeros = lambda n: jnp.zeros((D_MODEL, n), F32)
    w_main = jnp.concatenate([wga, wgb, wgc, wzb, wqa * SCALE, wka, wva, wza, wzc], axis=1)
    w_side = jnp.concatenate(
        [wcq, wckv, wkidx, wwidx, zeros(LANES - IDX_DIM - IDX_HEADS),
         wfa, zeros(LANES - A_HEADS), wqc * SCALE, wkc, wvc, zeros(SIDE_N - SIDE_VC - C_W)], axis=1)
    assert w_main.shape[1] == MAIN_N and w_side.shape[1] == SIDE_N
    wk, wv = w_ukv[:, :HEAD_DIM], w_ukv[:, HEAD_DIM:]
    w_kv = jnp.concatenate([wk, wv], axis=1).astype(BF16)
    w_vk = jnp.concatenate([wv, wk], axis=1).astype(BF16)
    return dict(
        norm_g=norm_g.reshape(1, D_MODEL),
        w_main=w_main.astype(BF16), w_side=w_side.astype(BF16),
        b_pad=jnp.zeros((1, LANES), F32).at[0, :A_HEADS].set(b_forget),
        g_cq=g_cq.reshape(1, B_Q_RANK), w_uq=(w_uq * SCALE).astype(BF16), w_uq_idx=w_uq_idx.astype(BF16),
        g_ckv=g_ckv.reshape(1, B_KV_RANK), w_kv=w_kv, w_vk=w_vk, w_kv_t=w_kv.T, w_vk_t=w_vk.T,
        w_a=w_a.astype(BF16), w_b=w_b.astype(BF16), w_c=w_c.astype(BF16), w_o=w_o.astype(BF16))


def _constants():
    r = np.arange(LANES)
    tri128 = (r[None, :] <= r[:, None]).astype(np.float32)
    r2 = np.arange(ATT_T)
    tri_t = (r2[None, :] <= r2[:, None]).astype(np.float32)
    c = np.arange(IDX_HEADS * IDX_DIM)
    tile = ((r[:, None] == c[None, :] % IDX_DIM) & (r[:, None] < IDX_DIM)).astype(np.float32)
    sel = np.zeros((A_HEADS, LANES, 2 * LANES), np.float32)
    for h in range(A_HEADS):
        for piece in range(3):
            sel[h, piece * A_HEADS + h, piece] = 1.0
            sel[h, piece * A_HEADS + h, LANES + 3 + piece] = -1.0
    one = np.zeros((1, 2 * LANES), np.float32)
    one[0, 3:6] = 1.0
    one[0, LANES:LANES + 3] = 1.0
    bf = lambda a: jnp.asarray(a, BF16)
    cum = (bf(tri128), bf(sel), jnp.asarray(one))
    return cum, bf(tri_t), bf(tile), _dilated_bias()


def _hybrid_layer(x2d, batch, seq, w, consts, final_g, final):
    cum_consts, tri_t, tile, bias_c = consts
    main = _rms_matmul(x2d, w["norm_g"], w["w_main"], BF16, tm=1024, tn=1280)
    side = _rms_matmul(x2d, w["norm_g"], w["w_side"], F32, tm=1024, tn=1536)
    main3 = main.reshape(batch, seq, MAIN_N)
    side3 = side.reshape(batch, seq, SIDE_N)
    qf, kf = _cum_forget(side3, w["b_pad"], cum_consts, batch, seq)
    ya = _fox_attention(main3, qf, kf, batch, seq)
    yb = _dsa_attention(side3, main3, w, tile, tri_t, batch, seq)
    yc = _dilated_attention(side3, main3, bias_c, batch, seq)
    m = batch * seq
    return _merge(x2d, ya.reshape(m, A_W), yb.reshape(m, B_W), yc.reshape(m, C_W), main,
                  w["w_a"], w["w_b"], w["w_c"], w["w_o"], final_g, final)


def kernel(x, norm_g, w_in, b_forget, g_cq, w_uq, w_uq_idx, g_ckv, w_ukv, w_a, w_b, w_c, w_o, final_g):
    batch, seq, d_model = x.shape
    assert d_model == D_MODEL and seq % (2 * ATT_T) == 0 and seq % FOX_T == 0
    depth = norm_g.shape[0]
    consts = _constants()
    fg = final_g.reshape(1, D_MODEL)
    x2d = x.reshape(batch * seq, D_MODEL)
    for l in range(depth):
        w = _layer_weights(norm_g[l], w_in[l], b_forget[l], g_cq[l], w_uq[l], w_uq_idx[l], g_ckv[l],
                           w_ukv[l], w_a[l], w_b[l], w_c[l], w_o[l])
        x2d = _hybrid_layer(x2d, batch, seq, w, consts, fg, final=(l == depth - 1))
    return x2d.reshape(batch, seq, D_MODEL)
```

```python
import functools

import numpy as np
import jax
import jax.numpy as jnp
from jax import lax
from jax.experimental import pallas as pl
from jax.experimental.pallas import tpu as pltpu

F32 = jnp.float32
BF16 = jnp.bfloat16
I32 = jnp.int32
I16 = jnp.int16

D_MODEL = 1024
HEAD_DIM = 64
EPS = 1e-6
A_HEADS = 8
A_W = A_HEADS * HEAD_DIM
B_HEADS = 8
B_W = B_HEADS * HEAD_DIM
B_Q_RANK = 256
B_KV_RANK = 128
IDX_HEADS = 8
IDX_DIM = 32
IDX_TOPK_MAX = 256
C_GROUPS = ((128, 1), (512, 4), (2048, 16))
C_HEADS_PER_GROUP = 4
C_HEADS = C_HEADS_PER_GROUP * len(C_GROUPS)
C_W = C_HEADS * HEAD_DIM
IN_WIDTHS = (A_W, A_W, A_W, A_HEADS, A_W,
             B_Q_RANK, B_KV_RANK, IDX_DIM, IDX_HEADS, B_W,
             C_W, C_W, C_W, C_W,
             D_MODEL, D_MODEL, D_MODEL)

LANES = 128
SUBLANES = 8
PACKED_ROWS = 16
SCALE = HEAD_DIM ** -0.5
LOG2E = float(np.log2(np.e))
ONES_ROWS = 16
NEG = -1e30
INT_MIN = -2 ** 31
INT16_MIN, INT16_MAX = -2 ** 15, 2 ** 15 - 1
CLASS_WINDOW = 128
assert all(w // d == CLASS_WINDOW for w, d in C_GROUPS)
ATT_T = 256
FOX_T = 512
POS_SPLIT = 64

MAIN_G = 0
MAIN_ZB = 3 * D_MODEL
MAIN_QA = MAIN_ZB + B_W
MAIN_KA = MAIN_QA + A_W
MAIN_VA = MAIN_KA + A_W
MAIN_ZA = MAIN_VA + A_W
MAIN_ZC = MAIN_ZA + A_W
MAIN_N = MAIN_ZC + C_W
SIDE_CQ = 0
SIDE_MISC = B_Q_RANK + B_KV_RANK
SIDE_WIDX = IDX_DIM
SIDE_FA = 512
SIDE_QC = SIDE_FA + LANES
SIDE_KC = SIDE_QC + C_W
SIDE_VC = SIDE_KC + C_W
SIDE_N = 3072

VMEM_LIMIT = 56 * 1024 * 1024


def _cparams(n_axes, vmem=VMEM_LIMIT):
    return pltpu.CompilerParams(dimension_semantics=("arbitrary",) * n_axes,
                                vmem_limit_bytes=vmem)


def _dot(a, b):
    return jnp.dot(a, b, preferred_element_type=F32)


def _dot_nt(a, b):
    return lax.dot_general(a, b, (((1,), (1,)), ((), ())), preferred_element_type=F32)


def _sigmoid(x):
    return 1.0 / (1.0 + jnp.exp(-x))


def _silu(x):
    return x * _sigmoid(x)


def _rms(x, g):
    return x * lax.rsqrt(jnp.mean(x * x, axis=-1, keepdims=True) + EPS) * g


def _split3(x):
    hi = x.astype(BF16)
    r1 = x - hi.astype(F32)
    mid = r1.astype(BF16)
    lo = (r1 - mid.astype(F32)).astype(BF16)
    return hi, mid, lo


def _fold8(x, op):
    n, t = x.shape
    return op(x.reshape(n // SUBLANES, SUBLANES, t), axis=0)


def _tree_fold(x, op, rows=SUBLANES):
    n, t = x.shape
    parts = [x[r:r + rows, :] for r in range(0, n, rows)]
    while len(parts) > 1:
        nxt = [op(parts[j], parts[j + 1]) for j in range(0, len(parts) - 1, 2)]
        if len(parts) % 2:
            nxt.append(parts[-1])
        parts = nxt
    return parts[0]


def _keys_max(x):
    return jnp.max(_fold8(x, jnp.max), axis=0, keepdims=True)


def _online_update_heads(states, s_ts, v_ts):
    ps, scaled = [], []
    for (m, acc), s_t in zip(states, s_ts):
        m_new = jnp.maximum(m, _keys_max(s_t))
        alpha = jnp.exp2(m - m_new)
        ps.append(jnp.exp2(s_t - m_new).astype(BF16))
        scaled.append((m_new, alpha * acc))
    return [(m, acc + _dot(v_t, p)) for (m, acc), v_t, p in zip(scaled, v_ts, ps)]


def _finish_pair(acc0, acc1, l0, l1):
    row = lax.broadcasted_iota(I32, (LANES, 1), 0)
    first = row < HEAD_DIM
    return (jnp.where(first, acc0, acc1) * (1.0 / jnp.where(first, l0, l1))).T


def _rms_matmul_kernel(x_ref, g_ref, w_ref, o_ref, h_ref):
    @pl.when(pl.program_id(1) == 0)
    def _():
        h_ref[...] = _rms(x_ref[...], g_ref[...]).astype(BF16)

    o_ref[...] = _dot(h_ref[...], w_ref[...]).astype(o_ref.dtype)


def _rms_matmul(x2d, g, w, out_dtype, tm, tn):
    m, k = x2d.shape
    n = w.shape[1]
    return pl.pallas_call(
        _rms_matmul_kernel,
        grid=(m // tm, n // tn),
        in_specs=[pl.BlockSpec((tm, k), lambda i, j: (i, 0)),
                  pl.BlockSpec((1, k), lambda i, j: (0, 0)),
                  pl.BlockSpec((k, tn), lambda i, j: (0, j))],
        out_specs=pl.BlockSpec((tm, tn), lambda i, j: (i, j)),
        out_shape=jax.ShapeDtypeStruct((m, n), out_dtype),
        scratch_shapes=[pltpu.VMEM((tm, k), BF16)],
        compiler_params=_cparams(2),
        name="rms_in_proj",
    )(x2d, g, w)


def _cum_kernel(fa_ref, b_ref, tri_ref, sel_ref, one_ref, qf_ref, kf_ref):
    seq = fa_ref.shape[1]
    tri = tri_ref[...]
    lane = lax.broadcasted_iota(I32, (1, LANES), 1)
    carry = jnp.zeros((1, LANES), F32)
    for blk in range(seq // LANES):
        rows = slice(blk * LANES, (blk + 1) * LANES)
        x = fa_ref[0, rows, :] + b_ref[...]
        lf = jnp.minimum(x, 0.0) - jnp.log(1.0 + jnp.exp(-jnp.abs(x)))
        hi, mid, lo = _split3(lf)
        c = _dot(tri, hi) + _dot(tri, mid) + _dot(tri, lo) + carry
        carry = c[LANES - 1:LANES, :]
        hi, mid, lo = _split3(jnp.where(lane < A_HEADS, c * LOG2E, 0.0))
        c3 = (hi.astype(F32) + pltpu.roll(mid.astype(F32), A_HEADS, 1)
              + pltpu.roll(lo.astype(F32), 2 * A_HEADS, 1)).astype(BF16)
        for h in range(A_HEADS):
            feats = _dot(c3, sel_ref[h]) + one_ref[...]
            qf_ref[0, h, rows, :] = feats[:, :LANES].astype(BF16)
            kf_ref[0, h, rows, :] = feats[:, LANES:].astype(BF16)


def _cum_forget(side, b_pad, consts, batch, seq):
    tri, sel, one = consts
    full = lambda a: pl.BlockSpec(a.shape, lambda b: (0,) * a.ndim)
    feat = jax.ShapeDtypeStruct((batch, A_HEADS, seq, LANES), BF16)
    return pl.pallas_call(
        _cum_kernel,
        grid=(batch,),
        in_specs=[pl.BlockSpec((1, seq, LANES), lambda b: (b, 0, SIDE_FA // LANES)),
                  full(b_pad), full(tri), full(sel), full(one)],
        out_specs=[pl.BlockSpec((1, A_HEADS, seq, LANES), lambda b: (b, 0, 0, 0))] * 2,
        out_shape=[feat, feat],
        compiler_params=_cparams(1),
        name="fox_cumsum",
    )(side, b_pad, tri, sel, one)


def _fox_kernel(q_ref, k_ref, v_ref, z_ref, qf_ref, kf_ref, o_ref, vt_s, qa_s, m_s, acc_s):
    seq = q_ref.shape[1]
    t = FOX_T
    n_heads = A_HEADS
    lane = lax.broadcasted_iota(I32, (1, LANES), 1)
    key_le_query = (lax.broadcasted_iota(I32, (t, t), 0) <= lax.broadcasted_iota(I32, (t, t), 1))
    pair_cols = lambda h: slice((h // 2) * LANES, (h // 2 + 1) * LANES)

    def transpose_v(c, _):
        cols = pl.ds(pl.multiple_of(c * t, t), t)
        for p in range(n_heads // 2):
            pc = slice(p * LANES, (p + 1) * LANES)
            vt_s[p, :LANES, cols] = v_ref[0, cols, pc].astype(F32).T.astype(BF16)
            vt_s[p, LANES:, cols] = jnp.ones((ONES_ROWS, t), BF16)
        return 0

    lax.fori_loop(0, seq // t, transpose_v, 0)

    def q_block(i, _):
        rows = pl.ds(pl.multiple_of(i * t, t), t)
        for h in range(n_heads):
            half = (lane >= HEAD_DIM) if h % 2 else (lane < HEAD_DIM)
            q2 = q_ref[0, rows, pair_cols(h)]
            qa_s[h] = jnp.concatenate([jnp.where(half, q2, jnp.zeros_like(q2)), qf_ref[0, h, rows, :]], axis=1)
            m_s[h] = jnp.full((1, t), NEG, F32)
            acc_s[h] = jnp.zeros((LANES + ONES_ROWS, t), F32)

        def chunk(kb, diagonal):
            cols = pl.ds(pl.multiple_of(kb * t, t), t)
            states = [(m_s[h], acc_s[h]) for h in range(n_heads)]
            s_ts = [_dot_nt(jnp.concatenate([k_ref[0, cols, pair_cols(h)], kf_ref[0, h, cols, :]], axis=1),
                            qa_s[h]) for h in range(n_heads)]
            if diagonal:
                s_ts = [jnp.where(key_le_query, s_t, NEG) for s_t in s_ts]
            v_ts = [vt_s[h // 2, :, cols] for h in range(n_heads)]
            for h, st in enumerate(_online_update_heads(states, s_ts, v_ts)):
                m_s[h], acc_s[h] = st

        def off_diagonal(kb, _):
            chunk(kb, False)
            return 0

        lax.fori_loop(0, i, off_diagonal, 0)
        chunk(i, True)
        for p in range(n_heads // 2):
            h0, h1 = 2 * p, 2 * p + 1
            out = _finish_pair(acc_s[h0, :LANES, :], acc_s[h1, :LANES, :],
                               acc_s[h0, LANES:LANES + 1, :], acc_s[h1, LANES:LANES + 1, :])
            out = out * _silu(z_ref[0, rows, pair_cols(h0)].astype(F32))
            o_ref[0, rows, pair_cols(h0)] = out.astype(o_ref.dtype)
        return 0

    lax.fori_loop(0, seq // t, q_block, 0)


def _fox_attention(main, qf, kf, batch, seq):
    t = FOX_T
    blk = lambda base: pl.BlockSpec((1, seq, A_W), lambda b, base=base: (b, 0, base // A_W))
    feat = pl.BlockSpec((1, A_HEADS, seq, LANES), lambda b: (b, 0, 0, 0))
    return pl.pallas_call(
        _fox_kernel,
        grid=(batch,),
        in_specs=[blk(MAIN_QA), blk(MAIN_KA), blk(MAIN_VA), blk(MAIN_ZA), feat, feat],
        out_specs=pl.BlockSpec((1, seq, A_W), lambda b: (b, 0, 0)),
        out_shape=jax.ShapeDtypeStruct((batch, seq, A_W), BF16),
        scratch_shapes=[pltpu.VMEM((A_HEADS // 2, LANES + ONES_ROWS, seq), BF16),
                        pltpu.VMEM((A_HEADS, t, 2 * LANES), BF16),
                        pltpu.VMEM((A_HEADS, 1, t), F32),
                        pltpu.VMEM((A_HEADS, LANES + ONES_ROWS, t), F32)],
        compiler_params=_cparams(1),
        name="fox_attention",
    )(main, main, main, main, qf, kf)


def _lane_group(lane, first):
    rel = lane - first
    return jnp.where((rel >= 0) & (rel < 9), rel // 3, -1)


def _key_pos_features(pos, lane, first):
    hi = lax.shift_right_logical(pos, 6).astype(F32)
    lo = (pos & (POS_SPLIT - 1)).astype(F32)
    grp = _lane_group(lane, first)
    return jnp.where(grp == 0, hi, jnp.where(grp == 1, lo, jnp.where(grp == 2, 1.0, 0.0)))


def _query_pos_features(pos, lane, first, slope):
    grp = _lane_group(lane, first)
    x = jnp.where(grp == 0, POS_SPLIT * slope,
                  jnp.where(grp == 1, slope, jnp.where(grp == 2, -slope * pos.astype(F32), 0.0)))
    hi, mid, lo = _split3(x)
    piece = (lane - first) % 3
    return jnp.where(piece == 0, hi, jnp.where(piece == 1, mid, lo))


def _dsa_kernel(side_ref, z_ref, gcq_ref, wuq_ref, wuqi_ref, gckv_ref, wkv_ref, wvk_ref, wkvt_ref, wvkt_ref,
                tile_ref, tri_ref, qfeat_ref, o_ref,
                qb_s, qi_s, kt_s, ka_s, kb_s, kvt_s, vkt_s, wt_s, qm_s, qa_s, key_s, hi_s, lo_s, mb_s, m_s, acc_s,
                *, top_k):
    seq = side_ref.shape[1]
    t = ATT_T
    n_heads = B_HEADS
    lane = lax.broadcasted_iota(I32, (1, LANES), 1)
    lane2 = lax.broadcasted_iota(I32, (1, 2 * LANES), 1)
    key_le_query = (lax.broadcasted_iota(I32, (t, t), 0) <= lax.broadcasted_iota(I32, (t, t), 1))
    row_iota = lax.broadcasted_iota(I32, (t, 1), 0)
    half_rows = lax.broadcasted_iota(I32, (LANES, 1), 0)

    def prep(c, _):
        r0 = pl.multiple_of(c * t, t)
        rows = pl.ds(r0, t)
        blk = side_ref[0, rows, :]
        cqn = _rms(blk[:, SIDE_CQ:SIDE_CQ + B_Q_RANK], gcq_ref[...]).astype(BF16)
        qb_s[rows, :] = _dot(cqn, wuq_ref[...]).astype(BF16)
        qi_s[rows, :] = _dot(cqn, wuqi_ref[...]).astype(BF16)
        kvn = _rms(blk[:, B_Q_RANK:SIDE_MISC], gckv_ref[...]).astype(BF16)
        pos = r0 + row_iota
        ka_s[rows, :] = jnp.where(lane < HEAD_DIM, _dot(kvn, wkv_ref[...]),
                                  _key_pos_features(pos, lane, HEAD_DIM)).astype(BF16)
        kb_s[rows, :] = jnp.where(lane >= HEAD_DIM, _dot(kvn, wvk_ref[...]),
                                  _key_pos_features(pos, lane, 0)).astype(BF16)
        kvt_s[:, rows] = jnp.where(half_rows < HEAD_DIM, 1.0, _dot_nt(wkvt_ref[...], kvn)).astype(BF16)
        vkt_s[:, rows] = jnp.where(half_rows >= HEAD_DIM, 1.0, _dot_nt(wvkt_ref[...], kvn)).astype(BF16)
        misc = blk[:, SIDE_MISC:]
        kt_s[rows, :] = _dot(misc.astype(BF16), tile_ref[...]).astype(BF16)
        wt_s[:, rows] = misc.T
        return 0

    lax.fori_loop(0, seq // t, prep, 0)

    def q_block(i, _):
        r0 = pl.multiple_of(i * t, t)
        rows = pl.ds(r0, t)
        nk = i + 1

        qi = qi_s[rows, :]
        for h in range(IDX_HEADS):
            in_head = (lane2 >= h * IDX_DIM) & (lane2 < (h + 1) * IDX_DIM)
            qm_s[h] = jnp.where(in_head, qi, jnp.zeros_like(qi))
        w_rows = [wt_s[SIDE_WIDX + h:SIDE_WIDX + h + 1, rows] for h in range(IDX_HEADS)]

        def score_chunk(kc, diagonal):
            c0 = pl.multiple_of(kc * t, t)
            kt = kt_s[pl.ds(c0, t), :]
            acc = jnp.zeros((t, t), F32)
            for h in range(IDX_HEADS):
                acc = acc + w_rows[h] * jnp.maximum(_dot_nt(kt, qm_s[h]), 0.0)
            bits = lax.bitcast_convert_type(acc, I32)
            key = bits ^ (lax.shift_right_arithmetic(bits, 31) & 0x7FFFFFFF)
            key = jnp.where(key == -1, 0, key)
            if diagonal:
                key = jnp.where(key_le_query, key, INT_MIN)
            key_s[pl.ds(c0, t), :] = key
            hi_s[pl.ds(c0, t), :] = lax.shift_right_arithmetic(key, 16).astype(I16)

        def score_body(kc, _):
            score_chunk(kc, False)
            return 0

        lax.fori_loop(0, i, score_body, 0)
        score_chunk(i, True)

        def count(pred):
            def body(kc, acc):
                c0 = pl.multiple_of(kc * t, t)
                return acc + _tree_fold(jnp.where(pred(key_s[pl.ds(c0, t), :]), 1.0, 0.0), jnp.add)
            acc = lax.fori_loop(0, nk, body, jnp.zeros((SUBLANES, t), F32))
            return jnp.sum(acc, axis=0, keepdims=True)

        def count16(src, cand):
            def body(kc, acc):
                c0 = pl.multiple_of(kc * t, t)
                hit = jnp.where(src[pl.ds(c0, t), :] >= cand, jnp.ones((), BF16), jnp.zeros((), BF16))
                return acc + _tree_fold(hit, jnp.add, PACKED_ROWS)
            acc = lax.fori_loop(0, nk, body, jnp.zeros((PACKED_ROWS, t), BF16))
            return jnp.sum(acc.astype(F32), axis=0, keepdims=True)

        def search16(src):
            def step(p, thr):
                cand = thr + lax.shift_left(jnp.int32(1), 15 - p)
                return jnp.where(count16(src, cand.astype(I16)) >= top_k, cand, thr)
            return lax.fori_loop(0, 16, step, jnp.full((1, t), INT16_MIN, I32))

        t_hi = search16(hi_s)

        def low_body(kc, _):
            c0 = pl.multiple_of(kc * t, t)
            key = key_s[pl.ds(c0, t), :]
            hi = lax.shift_right_arithmetic(key, 16)
            lo = (key & 0xFFFF) + INT16_MIN
            lo_s[pl.ds(c0, t), :] = jnp.where(hi > t_hi, INT16_MAX, jnp.where(hi == t_hi, lo, INT16_MIN)).astype(I16)
            return 0

        lax.fori_loop(0, nk, low_body, 0)
        t_lo = search16(lo_s)
        thr = lax.shift_left(t_hi, 16) | ((t_lo - INT16_MIN) & 0xFFFF)
        thr = jnp.maximum(thr, INT_MIN + 1)

        def mask_body(kc, _):
            c0 = pl.multiple_of(kc * t, t)
            mb_s[pl.ds(c0, t), :] = jnp.where(key_s[pl.ds(c0, t), :] >= thr, 0.0, NEG)
            return 0

        lax.fori_loop(0, nk, mask_body, 0)

        n_ge = count(lambda k: k >= thr)

        @pl.when(jnp.max(n_ge) > top_k)
        def _():
            room = top_k - count(lambda k: k > thr)

            def tie_body(kc, seen):
                c0 = pl.multiple_of(kc * t, t)
                tie = key_s[pl.ds(c0, t), :] == thr
                rank = seen + _dot(tri_ref[...], jnp.where(tie, 1.0, 0.0).astype(BF16))
                drop = tie & (rank > room)
                mb_s[pl.ds(c0, t), :] = jnp.where(drop, NEG, mb_s[pl.ds(c0, t), :])
                return rank[t - 1:t, :]

            lax.fori_loop(0, nk, tie_body, jnp.zeros((1, t), F32))

        for h in range(n_heads):
            half = (lane >= HEAD_DIM) if h % 2 else (lane < HEAD_DIM)
            q2 = qb_s[rows, (h // 2) * LANES:(h // 2 + 1) * LANES]
            qa_s[h] = jnp.where(half, q2, qfeat_ref[h, rows, :])
            m_s[h] = jnp.full((1, t), NEG, F32)
            acc_s[h] = jnp.zeros((LANES, t), F32)

        def attend(kc, _):
            c0 = pl.multiple_of(kc * t, t)
            cols = pl.ds(c0, t)
            states = [(m_s[h], acc_s[h]) for h in range(n_heads)]
            mask = mb_s[cols, :]
            k_ops = (ka_s[cols, :], kb_s[cols, :])
            v_ts = (vkt_s[:, cols], kvt_s[:, cols])
            s_ts = [_dot_nt(k_ops[h % 2], qa_s[h]) + mask for h in range(n_heads)]
            v_list = [v_ts[h % 2] for h in range(n_heads)]
            for h, st in enumerate(_online_update_heads(states, s_ts, v_list)):
                m_s[h], acc_s[h] = st
            return 0

        lax.fori_loop(0, nk, attend, 0)

        for hp in range(n_heads // 2):
            acc0, acc1 = acc_s[2 * hp], acc_s[2 * hp + 1]
            cols = slice(hp * LANES, (hp + 1) * LANES)
            out = _finish_pair(acc0, acc1, acc0[HEAD_DIM:HEAD_DIM + 1, :], acc1[0:1, :])
            out = out * _silu(z_ref[0, rows, cols].astype(F32))
            o_ref[0, rows, cols] = out.astype(o_ref.dtype)
        return 0

    lax.fori_loop(0, seq // t, q_block, 0)


def _dsa_attention(side, main, w, tile, tri, batch, seq):
    assert seq // POS_SPLIT <= 256
    assert seq // PACKED_ROWS <= 256
    top_k = min(IDX_TOPK_MAX, seq // 4)
    slopes = tuple(float(2.0 ** (-8.0 * (h + 1) / B_HEADS)) * LOG2E for h in range(B_HEADS))
    t = ATT_T
    const = lambda a: pl.BlockSpec(a.shape, lambda b: (0,) * a.ndim)
    pos = jnp.arange(seq, dtype=I32).reshape(seq, 1)
    lane = jnp.arange(LANES, dtype=I32).reshape(1, LANES)
    qfeat = jnp.stack([_query_pos_features(pos, lane, 0 if h % 2 else HEAD_DIM, slopes[h])
                       for h in range(B_HEADS)]).astype(BF16)
    weights = [w["g_cq"], w["w_uq"], w["w_uq_idx"], w["g_ckv"], w["w_kv"], w["w_vk"], w["w_kv_t"], w["w_vk_t"],
               tile, tri, qfeat]
    return pl.pallas_call(
        functools.partial(_dsa_kernel, top_k=top_k),
        grid=(batch,),
        in_specs=[pl.BlockSpec((1, seq, 512), lambda b: (b, 0, 0)),
                  pl.BlockSpec((1, seq, B_W), lambda b: (b, 0, MAIN_ZB // B_W))]
                 + [const(a) for a in weights],
        out_specs=pl.BlockSpec((1, seq, B_W), lambda b: (b, 0, 0)),
        out_shape=jax.ShapeDtypeStruct((batch, seq, B_W), BF16),
        scratch_shapes=[pltpu.VMEM((seq, B_W), BF16),
                        pltpu.VMEM((seq, IDX_HEADS * IDX_DIM), BF16),
                        pltpu.VMEM((seq, IDX_HEADS * IDX_DIM), BF16),
                        pltpu.VMEM((seq, LANES), BF16),
                        pltpu.VMEM((seq, LANES), BF16),
                        pltpu.VMEM((LANES, seq), BF16),
                        pltpu.VMEM((LANES, seq), BF16),
                        pltpu.VMEM((LANES, seq), F32),
                        pltpu.VMEM((IDX_HEADS, t, IDX_HEADS * IDX_DIM), BF16),
                        pltpu.VMEM((B_HEADS, t, LANES), BF16),
                        pltpu.VMEM((seq, t), I32),
                        pltpu.VMEM((seq, t), I16),
                        pltpu.VMEM((seq, t), I16),
                        pltpu.VMEM((seq, t), F32),
                        pltpu.VMEM((B_HEADS, 1, t), F32),
                        pltpu.VMEM((B_HEADS, LANES, t), F32)],
        compiler_params=_cparams(1),
        name="dsa_attention",
    )(side, main, *weights)


C_UNROLL = 8


def _dilated_kernel(q_ref, k_ref, v_ref, bias_ref, z0_ref, z1_ref, z2_ref, o_ref, o_s, l_s):
    seq = q_ref.shape[1]
    sp = pl.program_id(1)
    grp = pl.program_id(2)
    lane = lax.broadcasted_iota(I32, (1, LANES), 1)
    cw = CLASS_WINDOW

    def rows(start, d):
        return pl.ds(start, cw) if d == 1 else pl.ds(start, cw, stride=d)

    def group_body(g, d):
        per_class = seq // (d * cw)

        def load_block(n):
            r = n // per_class
            ib = n % per_class
            start = r + ib * (cw * d)
            pstart = r + jnp.maximum(ib - 1, 0) * (cw * d)
            ld = lambda ref, s0: ref[0, rows(s0, d), :].astype(BF16)
            return dict(ib=ib, start=start, q=ld(q_ref, start), kc=ld(k_ref, start), vc=ld(v_ref, start),
                        kp=ld(k_ref, pstart), vp=ld(v_ref, pstart))

        def blk(it, _):
            blocks = [load_block(it * C_UNROLL + u) for u in range(C_UNROLL)]
            logits = []
            for b in blocks:
                for e in range(2):
                    half = (lane >= HEAD_DIM) if e else (lane < HEAD_DIM)
                    qm = jnp.where(half, b["q"], jnp.zeros_like(b["q"]))
                    s_cur = _dot_nt(qm, b["kc"]) + bias_ref[0, e, :, cw:]
                    s_prev = _dot_nt(qm, b["kp"]) + jnp.where(b["ib"] > 0, bias_ref[0, e, :, :cw], NEG)
                    logits.append((s_cur, s_prev))
            probs = []
            for s_cur, s_prev in logits:
                m = jnp.max(jnp.maximum(s_cur, s_prev), axis=-1, keepdims=True)
                p_cur = jnp.exp(s_cur - m)
                p_prev = jnp.exp(s_prev - m)
                l = jnp.sum(p_cur + p_prev, axis=-1, keepdims=True)
                probs.append((p_cur.astype(BF16), p_prev.astype(BF16), 1.0 / l, m + jnp.log(l)))
            for u, b in enumerate(blocks):
                outs = []
                for e in range(2):
                    p_cur, p_prev, inv_l, _ = probs[2 * u + e]
                    outs.append((_dot(p_cur, b["vc"]) + _dot(p_prev, b["vp"])) * inv_l)
                o_s[g, rows(b["start"], d), :] = jnp.where(lane < HEAD_DIM, outs[0], outs[1])
                l_s[g, rows(b["start"], d), :] = jnp.where(lane < HEAD_DIM, probs[2 * u][3], probs[2 * u + 1][3])
            return 0

        lax.fori_loop(0, seq // (cw * C_UNROLL), blk, 0)

    for g, (_, d) in enumerate(C_GROUPS):
        pl.when(grp == g)(functools.partial(group_body, g, d))

    n_grp = len(C_GROUPS)
    z_refs = (z0_ref, z1_ref, z2_ref)

    def combine(sp_static):
        def chunk(c, _):
            r0 = pl.multiple_of(c * 256, 256)
            ls = [l_s[g, pl.ds(r0, 256), :] for g in range(n_grp)]
            m = functools.reduce(jnp.maximum, ls)
            ws = [jnp.exp(x - m) for x in ls]
            inv = 1.0 / functools.reduce(lambda a, b: a + b, ws)
            for g in range(n_grp):
                col = (g * 2 + sp_static) * LANES
                y = (ws[g] * inv) * o_s[g, pl.ds(r0, 256), :]
                y = y * _silu(z_refs[g][0, pl.ds(r0, 256), :].astype(F32))
                o_ref[0, pl.ds(r0, 256), col:col + LANES] = y.astype(o_ref.dtype)
            return 0

        lax.fori_loop(0, seq // 256, chunk, 0)

    for s in range(2):
        pl.when((grp == n_grp - 1) & (sp == s))(functools.partial(combine, s))


def _dilated_attention(side, main, bias, batch, seq):
    assert seq % (C_GROUPS[-1][1] * CLASS_WINDOW) == 0
    assert (seq // CLASS_WINDOW) % C_UNROLL == 0
    n_grp = len(C_GROUPS)
    blk = lambda base: pl.BlockSpec(
        (1, seq, LANES), lambda b, s, g, base=base: (b, 0, base // LANES + 2 * g + s))
    zblk = lambda g: pl.BlockSpec(
        (1, seq, LANES), lambda b, s, _, g=g: (b, 0, MAIN_ZC // LANES + 2 * g + s))
    return pl.pallas_call(
        _dilated_kernel,
        grid=(batch, 2, n_grp),
        in_specs=[blk(SIDE_QC), blk(SIDE_KC), blk(SIDE_VC),
                  pl.BlockSpec((1, 2, CLASS_WINDOW, 2 * CLASS_WINDOW), lambda b, s, g: (2 * g + s, 0, 0, 0)),
                  zblk(0), zblk(1), zblk(2)],
        out_specs=pl.BlockSpec((1, seq, C_W), lambda b, s, g: (b, 0, 0)),
        out_shape=jax.ShapeDtypeStruct((batch, seq, C_W), BF16),
        scratch_shapes=[pltpu.VMEM((n_grp, seq, LANES), F32),
                        pltpu.VMEM((n_grp, seq, LANES), F32)],
        compiler_params=_cparams(3),
        name="dilated_attention",
    )(side, side, side, bias, main, main, main)


def _dilated_bias():
    cw = CLASS_WINDOW
    slopes = np.exp2(-8.0 * (np.arange(C_HEADS, dtype=np.float32) + 1.0) / C_HEADS).astype(np.float32)
    i = np.arange(cw)[:, None]
    j = np.arange(cw)[None, :]
    out = np.empty((C_HEADS // 2, 2, cw, 2 * cw), np.float32)
    for h in range(C_HEADS):
        d = C_GROUPS[h // C_HEADS_PER_GROUP][1]
        prev = np.where(j >= i, -slopes[h] * np.float32(d) * (cw + i - j).astype(np.float32), NEG)
        cur = np.where(j <= i, -slopes[h] * np.float32(d) * (i - j).astype(np.float32), NEG)
        out[h // 2, h % 2] = np.concatenate([prev, cur], axis=1)
    return jnp.asarray(out)


def _merge_kernel(x_ref, ya_ref, yb_ref, yc_ref, g_ref, wa_ref, wb_ref, wc_ref, wo_ref, fg_ref, o_ref,
                  *, final):
    gate = lambda n: _sigmoid(g_ref[:, n * D_MODEL:(n + 1) * D_MODEL].astype(F32))
    merged = (gate(0) * _dot(ya_ref[...], wa_ref[...])
              + gate(1) * _dot(yb_ref[...], wb_ref[...])
              + gate(2) * _dot(yc_ref[...], wc_ref[...]))
    y = x_ref[...] + _dot(merged.astype(BF16), wo_ref[...])
    if final:
        y = _rms(y, fg_ref[...])
    o_ref[...] = y


def _merge(x2d, ya, yb, yc, main, w_a, w_b, w_c, w_o, final_g, final, tm=512):
    m = x2d.shape[0]
    row = lambda w: pl.BlockSpec((tm, w), lambda i: (i, 0))
    const = lambda a: pl.BlockSpec(a.shape, lambda i: (0, 0))
    return pl.pallas_call(
        functools.partial(_merge_kernel, final=final),
        grid=(m // tm,),
        in_specs=[row(D_MODEL), row(A_W), row(B_W), row(C_W), row(3 * D_MODEL),
                  const(w_a), const(w_b), const(w_c), const(w_o), const(final_g)],
        out_specs=row(D_MODEL),
        out_shape=jax.ShapeDtypeStruct((m, D_MODEL), F32),
        compiler_params=_cparams(1),
        name="gated_merge",
    )(x2d, ya, yb, yc, main, w_a, w_b, w_c, w_o, final_g)


def _layer_weights(norm_g, w_in, b_forget, g_cq, w_uq, w_uq_idx, g_ckv, w_ukv, w_a, w_b, w_c, w_o):
    points = np.cumsum(IN_WIDTHS)[:-1].tolist()
    (wqa, wka, wva, wfa, wza, wcq, wckv, wkidx, wwidx, wzb,
     wqc, wkc, wvc, wzc, wga, wgb, wgc) = jnp.split(w_in, points, axis=1)
    zeros = lambda n: jnp.zeros((D_MODEL, n), F32)
    w_main = jnp.concatenate([wga, wgb, wgc, wzb, wqa * (SCALE * LOG2E), wka, wva, wza, wzc], axis=1)
    w_side = jnp.concatenate(
        [wcq, wckv, wkidx, wwidx, zeros(LANES - IDX_DIM - IDX_HEADS),
         wfa, zeros(LANES - A_HEADS), wqc * SCALE, wkc, wvc, zeros(SIDE_N - SIDE_VC - C_W)], axis=1)
    assert w_main.shape[1] == MAIN_N and w_side.shape[1] == SIDE_N
    wk, wv = w_ukv[:, :HEAD_DIM], w_ukv[:, HEAD_DIM:]
    w_kv = jnp.concatenate([wk, wv], axis=1).astype(BF16)
    w_vk = jnp.concatenate([wv, wk], axis=1).astype(BF16)
    return dict(
        norm_g=norm_g.reshape(1, D_MODEL),
        w_main=w_main.astype(BF16), w_side=w_side.astype(BF16),
        b_pad=jnp.zeros((1, LANES), F32).at[0, :A_HEADS].set(b_forget),
        g_cq=g_cq.reshape(1, B_Q_RANK), w_uq=(w_uq * (SCALE * LOG2E)).astype(BF16), w_uq_idx=w_uq_idx.astype(BF16),
        g_ckv=g_ckv.reshape(1, B_KV_RANK), w_kv=w_kv, w_vk=w_vk, w_kv_t=w_kv.T, w_vk_t=w_vk.T,
        w_a=w_a.astype(BF16), w_b=w_b.astype(BF16), w_c=w_c.astype(BF16), w_o=w_o.astype(BF16))


def _constants():
    r = np.arange(LANES)
    tri128 = (r[None, :] <= r[:, None]).astype(np.float32)
    r2 = np.arange(ATT_T)
    tri_t = (r2[None, :] <= r2[:, None]).astype(np.float32)
    c = np.arange(IDX_HEADS * IDX_DIM)
    tile = ((r[:, None] == c[None, :] % IDX_DIM) & (r[:, None] < IDX_DIM)).astype(np.float32)
    sel = np.zeros((A_HEADS, LANES, 2 * LANES), np.float32)
    for h in range(A_HEADS):
        for piece in range(3):
            sel[h, piece * A_HEADS + h, piece] = 1.0
            sel[h, piece * A_HEADS + h, LANES + 3 + piece] = -1.0
    one = np.zeros((1, 2 * LANES), np.float32)
    one[0, 3:6] = 1.0
    one[0, LANES:LANES + 3] = 1.0
    bf = lambda a: jnp.asarray(a, BF16)
    cum = (bf(tri128), bf(sel), jnp.asarray(one))
    return cum, bf(tri_t), bf(tile), _dilated_bias()


def _hybrid_layer(x2d, batch, seq, w, consts, final_g, final):
    cum_consts, tri_t, tile, bias_c = consts
    main = _rms_matmul(x2d, w["norm_g"], w["w_main"], BF16, tm=1024, tn=1280)
    side = _rms_matmul(x2d, w["norm_g"], w["w_side"], F32, tm=1024, tn=1536)
    main3 = main.reshape(batch, seq, MAIN_N)
    side3 = side.reshape(batch, seq, SIDE_N)
    qf, kf = _cum_forget(side3, w["b_pad"], cum_consts, batch, seq)
    ya = _fox_attention(main3, qf, kf, batch, seq)
    yb = _dsa_attention(side3, main3, w, tile, tri_t, batch, seq)
    yc = _dilated_attention(side3, main3, bias_c, batch, seq)
    m = batch * seq
    return _merge(x2d, ya.reshape(m, A_W), yb.reshape(m, B_W), yc.reshape(m, C_W), main,
                  w["w_a"], w["w_b"], w["w_c"], w["w_o"], final_g, final)


def kernel(x, norm_g, w_in, b_forget, g_cq, w_uq, w_uq_idx, g_ckv, w_ukv, w_a, w_b, w_c, w_o, final_g):
    batch, seq, d_model = x.shape
    assert d_model == D_MODEL and seq % ATT_T == 0 and seq % FOX_T == 0
    depth = norm_g.shape[0]
    consts = _constants()
    fg = final_g.reshape(1, D_MODEL)
    x2d = x.reshape(batch * seq, D_MODEL)
    for l in range(depth):
        w = _layer_weights(norm_g[l], w_in[l], b_forget[l], g_cq[l], w_uq[l], w_uq_idx[l], g_ckv[l],
                           w_ukv[l], w_a[l], w_b[l], w_c[l], w_o[l])
        x2d = _hybrid_layer(x2d, batch, seq, w, consts, fg, final=(l == depth - 1))
    return x2d.reshape(batch, seq, D_MODEL)
```

```python
import functools

import numpy as np
import jax
import jax.numpy as jnp
from jax import lax
from jax.experimental import pallas as pl
from jax.experimental.pallas import tpu as pltpu

F32 = jnp.float32
BF16 = jnp.bfloat16
I32 = jnp.int32
I16 = jnp.int16

D_MODEL = 1024
HEAD_DIM = 64
EPS = 1e-6
A_HEADS = 8
A_W = A_HEADS * HEAD_DIM
B_HEADS = 8
B_W = B_HEADS * HEAD_DIM
B_Q_RANK = 256
B_KV_RANK = 128
IDX_HEADS = 8
IDX_DIM = 32
IDX_TOPK_MAX = 256
C_GROUPS = ((128, 1), (512, 4), (2048, 16))
C_HEADS_PER_GROUP = 4
C_HEADS = C_HEADS_PER_GROUP * len(C_GROUPS)
C_W = C_HEADS * HEAD_DIM
IN_WIDTHS = (A_W, A_W, A_W, A_HEADS, A_W,
             B_Q_RANK, B_KV_RANK, IDX_DIM, IDX_HEADS, B_W,
             C_W, C_W, C_W, C_W,
             D_MODEL, D_MODEL, D_MODEL)

LANES = 128
SUBLANES = 8
PACKED_ROWS = 16
SCALE = HEAD_DIM ** -0.5
LOG2E = float(np.log2(np.e))
ONES_ROWS = 16
NEG = -1e30
INT_MIN = -2 ** 31
INT16_MIN, INT16_MAX = -2 ** 15, 2 ** 15 - 1
CLASS_WINDOW = 128
assert all(w // d == CLASS_WINDOW for w, d in C_GROUPS)
ATT_T = 256
FOX_T = 512
POS_SPLIT = 64

MAIN_G = 0
MAIN_ZB = 3 * D_MODEL
MAIN_QA = MAIN_ZB + B_W
MAIN_KA = MAIN_QA + A_W
MAIN_VA = MAIN_KA + A_W
MAIN_ZA = MAIN_VA + A_W
MAIN_ZC = MAIN_ZA + A_W
MAIN_N = MAIN_ZC + C_W
SIDE_CQ = 0
SIDE_MISC = B_Q_RANK + B_KV_RANK
SIDE_WIDX = IDX_DIM
SIDE_FA = 512
SIDE_QC = SIDE_FA + LANES
SIDE_KC = SIDE_QC + C_W
SIDE_VC = SIDE_KC + C_W
SIDE_N = 3072

VMEM_LIMIT = 56 * 1024 * 1024


def _cparams(n_axes, vmem=VMEM_LIMIT):
    return pltpu.CompilerParams(dimension_semantics=("arbitrary",) * n_axes,
                                vmem_limit_bytes=vmem)


def _dot(a, b):
    return jnp.dot(a, b, preferred_element_type=F32)


def _dot_nt(a, b):
    return lax.dot_general(a, b, (((1,), (1,)), ((), ())), preferred_element_type=F32)


def _sigmoid(x):
    return 1.0 / (1.0 + jnp.exp(-x))


def _silu(x):
    return x * _sigmoid(x)


def _rms(x, g):
    return x * lax.rsqrt(jnp.mean(x * x, axis=-1, keepdims=True) + EPS) * g


def _split3(x):
    hi = x.astype(BF16)
    r1 = x - hi.astype(F32)
    mid = r1.astype(BF16)
    lo = (r1 - mid.astype(F32)).astype(BF16)
    return hi, mid, lo


def _fold8(x, op):
    n, t = x.shape
    return op(x.reshape(n // SUBLANES, SUBLANES, t), axis=0)


def _tree_fold(x, op, rows=SUBLANES):
    n, t = x.shape
    parts = [x[r:r + rows, :] for r in range(0, n, rows)]
    while len(parts) > 1:
        nxt = [op(parts[j], parts[j + 1]) for j in range(0, len(parts) - 1, 2)]
        if len(parts) % 2:
            nxt.append(parts[-1])
        parts = nxt
    return parts[0]


def _keys_max(x):
    return jnp.max(_fold8(x, jnp.max), axis=0, keepdims=True)


def _online_update_heads(states, s_ts, v_ts):
    ps, scaled = [], []
    for (m, acc), s_t in zip(states, s_ts):
        m_new = jnp.maximum(m, _keys_max(s_t))
        alpha = jnp.exp2(m - m_new)
        ps.append(jnp.exp2(s_t - m_new).astype(BF16))
        scaled.append((m_new, alpha * acc))
    return [(m, acc + _dot(v_t, p)) for (m, acc), v_t, p in zip(scaled, v_ts, ps)]


def _finish_pair(acc0, acc1):
    norm = lambda acc: acc[:HEAD_DIM, :] * (1.0 / acc[HEAD_DIM:HEAD_DIM + 1, :])
    return jnp.concatenate([norm(acc0), norm(acc1)], axis=0).T


def _rms_matmul_kernel(x_ref, g_ref, w_ref, o_ref, h_ref):
    @pl.when(pl.program_id(1) == 0)
    def _():
        h_ref[...] = _rms(x_ref[...], g_ref[...]).astype(BF16)

    o_ref[...] = _dot(h_ref[...], w_ref[...]).astype(o_ref.dtype)


def _rms_matmul(x2d, g, w, out_dtype, tm, tn):
    m, k = x2d.shape
    n = w.shape[1]
    return pl.pallas_call(
        _rms_matmul_kernel,
        grid=(m // tm, n // tn),
        in_specs=[pl.BlockSpec((tm, k), lambda i, j: (i, 0)),
                  pl.BlockSpec((1, k), lambda i, j: (0, 0)),
                  pl.BlockSpec((k, tn), lambda i, j: (0, j))],
        out_specs=pl.BlockSpec((tm, tn), lambda i, j: (i, j)),
        out_shape=jax.ShapeDtypeStruct((m, n), out_dtype),
        scratch_shapes=[pltpu.VMEM((tm, k), BF16)],
        compiler_params=_cparams(2),
        name="rms_in_proj",
    )(x2d, g, w)


def _cum_kernel(fa_ref, b_ref, tri_ref, sel_ref, one_ref, qf_ref, kf_ref):
    seq = fa_ref.shape[1]
    tri = tri_ref[...]
    lane = lax.broadcasted_iota(I32, (1, LANES), 1)
    carry = jnp.zeros((1, LANES), F32)
    for blk in range(seq // LANES):
        rows = slice(blk * LANES, (blk + 1) * LANES)
        x = fa_ref[0, rows, :] + b_ref[...]
        lf = jnp.minimum(x, 0.0) - jnp.log(1.0 + jnp.exp(-jnp.abs(x)))
        hi, mid, lo = _split3(lf)
        c = _dot(tri, hi) + _dot(tri, mid) + _dot(tri, lo) + carry
        carry = c[LANES - 1:LANES, :]
        hi, mid, lo = _split3(jnp.where(lane < A_HEADS, c * LOG2E, 0.0))
        c3 = (hi.astype(F32) + pltpu.roll(mid.astype(F32), A_HEADS, 1)
              + pltpu.roll(lo.astype(F32), 2 * A_HEADS, 1)).astype(BF16)
        for h in range(A_HEADS):
            feats = _dot(c3, sel_ref[h]) + one_ref[...]
            qf_ref[0, h, rows, :] = feats[:, :LANES].astype(BF16)
            kf_ref[0, h, rows, :] = feats[:, LANES:].astype(BF16)


def _cum_forget(side, b_pad, consts, batch, seq):
    tri, sel, one = consts
    full = lambda a: pl.BlockSpec(a.shape, lambda b: (0,) * a.ndim)
    feat = jax.ShapeDtypeStruct((batch, A_HEADS, seq, LANES), BF16)
    return pl.pallas_call(
        _cum_kernel,
        grid=(batch,),
        in_specs=[pl.BlockSpec((1, seq, LANES), lambda b: (b, 0, SIDE_FA // LANES)),
                  full(b_pad), full(tri), full(sel), full(one)],
        out_specs=[pl.BlockSpec((1, A_HEADS, seq, LANES), lambda b: (b, 0, 0, 0))] * 2,
        out_shape=[feat, feat],
        compiler_params=_cparams(1),
        name="fox_cumsum",
    )(side, b_pad, tri, sel, one)


def _fox_kernel(q_ref, k_ref, v_ref, z_ref, qf_ref, kf_ref, o_ref, vt_s, qa_s, m_s, acc_s):
    seq = q_ref.shape[1]
    t = FOX_T
    n_heads = A_HEADS
    lane = lax.broadcasted_iota(I32, (1, LANES), 1)
    key_le_query = (lax.broadcasted_iota(I32, (t, t), 0) <= lax.broadcasted_iota(I32, (t, t), 1))
    pair_cols = lambda h: slice((h // 2) * LANES, (h // 2 + 1) * LANES)

    def transpose_v(c, _):
        cols = pl.ds(pl.multiple_of(c * t, t), t)
        for p in range(n_heads // 2):
            pc = slice(p * LANES, (p + 1) * LANES)
            v_t = v_ref[0, cols, pc].astype(F32).T.astype(BF16)
            for e in range(2):
                vt_s[2 * p + e, :HEAD_DIM, cols] = v_t[e * HEAD_DIM:(e + 1) * HEAD_DIM, :]
                vt_s[2 * p + e, HEAD_DIM:, cols] = jnp.ones((ONES_ROWS, t), BF16)
        return 0

    lax.fori_loop(0, seq // t, transpose_v, 0)

    def q_block(i, _):
        rows = pl.ds(pl.multiple_of(i * t, t), t)
        for h in range(n_heads):
            half = (lane >= HEAD_DIM) if h % 2 else (lane < HEAD_DIM)
            q2 = q_ref[0, rows, pair_cols(h)]
            qa_s[h] = jnp.concatenate([jnp.where(half, q2, jnp.zeros_like(q2)), qf_ref[0, h, rows, :]], axis=1)
            m_s[h] = jnp.full((1, t), NEG, F32)
            acc_s[h] = jnp.zeros((HEAD_DIM + ONES_ROWS, t), F32)

        def chunk(kb, diagonal):
            cols = pl.ds(pl.multiple_of(kb * t, t), t)
            states = [(m_s[h], acc_s[h]) for h in range(n_heads)]
            s_ts = [_dot_nt(jnp.concatenate([k_ref[0, cols, pair_cols(h)], kf_ref[0, h, cols, :]], axis=1),
                            qa_s[h]) for h in range(n_heads)]
            if diagonal:
                s_ts = [jnp.where(key_le_query, s_t, NEG) for s_t in s_ts]
            v_ts = [vt_s[h, :, cols] for h in range(n_heads)]
            for h, st in enumerate(_online_update_heads(states, s_ts, v_ts)):
                m_s[h], acc_s[h] = st

        def off_diagonal(kb, _):
            chunk(kb, False)
            return 0

        lax.fori_loop(0, i, off_diagonal, 0)
        chunk(i, True)
        for p in range(n_heads // 2):
            h0, h1 = 2 * p, 2 * p + 1
            out = _finish_pair(acc_s[h0], acc_s[h1]) * _silu(z_ref[0, rows, pair_cols(h0)].astype(F32))
            o_ref[0, rows, pair_cols(h0)] = out.astype(o_ref.dtype)
        return 0

    lax.fori_loop(0, seq // t, q_block, 0)


def _fox_attention(main, qf, kf, batch, seq):
    t = FOX_T
    blk = lambda base: pl.BlockSpec((1, seq, A_W), lambda b, base=base: (b, 0, base // A_W))
    feat = pl.BlockSpec((1, A_HEADS, seq, LANES), lambda b: (b, 0, 0, 0))
    return pl.pallas_call(
        _fox_kernel,
        grid=(batch,),
        in_specs=[blk(MAIN_QA), blk(MAIN_KA), blk(MAIN_VA), blk(MAIN_ZA), feat, feat],
        out_specs=pl.BlockSpec((1, seq, A_W), lambda b: (b, 0, 0)),
        out_shape=jax.ShapeDtypeStruct((batch, seq, A_W), BF16),
        scratch_shapes=[pltpu.VMEM((A_HEADS, HEAD_DIM + ONES_ROWS, seq), BF16),
                        pltpu.VMEM((A_HEADS, t, 2 * LANES), BF16),
                        pltpu.VMEM((A_HEADS, 1, t), F32),
                        pltpu.VMEM((A_HEADS, HEAD_DIM + ONES_ROWS, t), F32)],
        compiler_params=_cparams(1),
        name="fox_attention",
    )(main, main, main, main, qf, kf)


def _lane_group(lane, first):
    rel = lane - first
    return jnp.where((rel >= 0) & (rel < 9), rel // 3, -1)


def _key_pos_features(pos, lane, first):
    hi = lax.shift_right_logical(pos, 6).astype(F32)
    lo = (pos & (POS_SPLIT - 1)).astype(F32)
    grp = _lane_group(lane, first)
    return jnp.where(grp == 0, hi, jnp.where(grp == 1, lo, jnp.where(grp == 2, 1.0, 0.0)))


def _query_pos_features(pos, lane, first, slope):
    grp = _lane_group(lane, first)
    x = jnp.where(grp == 0, POS_SPLIT * slope,
                  jnp.where(grp == 1, slope, jnp.where(grp == 2, -slope * pos.astype(F32), 0.0)))
    hi, mid, lo = _split3(x)
    piece = (lane - first) % 3
    return jnp.where(piece == 0, hi, jnp.where(piece == 1, mid, lo))


def _dsa_kernel(side_ref, z_ref, gcq_ref, wuq_ref, wuqi_ref, gckv_ref, wkv_ref, wvk_ref, wvt_ref,
                tile_ref, tri_ref, qfeat_ref, o_ref,
                qb_s, qi_s, kt_s, ka_s, kb_s, vt_s, wt_s, qm_s, qa_s, key_s, hi_s, lo_s, mb_s, m_s, acc_s,
                *, top_k):
    seq = side_ref.shape[1]
    t = ATT_T
    n_heads = B_HEADS
    lane = lax.broadcasted_iota(I32, (1, LANES), 1)
    lane2 = lax.broadcasted_iota(I32, (1, 2 * LANES), 1)
    key_le_query = (lax.broadcasted_iota(I32, (t, t), 0) <= lax.broadcasted_iota(I32, (t, t), 1))
    row_iota = lax.broadcasted_iota(I32, (t, 1), 0)

    def prep(c, _):
        r0 = pl.multiple_of(c * t, t)
        rows = pl.ds(r0, t)
        blk = side_ref[0, rows, :]
        cqn = _rms(blk[:, SIDE_CQ:SIDE_CQ + B_Q_RANK], gcq_ref[...]).astype(BF16)
        qb_s[rows, :] = _dot(cqn, wuq_ref[...]).astype(BF16)
        qi_s[rows, :] = _dot(cqn, wuqi_ref[...]).astype(BF16)
        kvn = _rms(blk[:, B_Q_RANK:SIDE_MISC], gckv_ref[...]).astype(BF16)
        pos = r0 + row_iota
        ka_s[rows, :] = jnp.where(lane < HEAD_DIM, _dot(kvn, wkv_ref[...]),
                                  _key_pos_features(pos, lane, HEAD_DIM)).astype(BF16)
        kb_s[rows, :] = jnp.where(lane >= HEAD_DIM, _dot(kvn, wvk_ref[...]),
                                  _key_pos_features(pos, lane, 0)).astype(BF16)
        vt_s[:HEAD_DIM, rows] = _dot_nt(wvt_ref[...], kvn).astype(BF16)
        vt_s[HEAD_DIM:, rows] = jnp.ones((ONES_ROWS, t), BF16)
        misc = blk[:, SIDE_MISC:]
        kt_s[rows, :] = _dot(misc.astype(BF16), tile_ref[...]).astype(BF16)
        wt_s[:, rows] = misc.T
        return 0

    lax.fori_loop(0, seq // t, prep, 0)

    def q_block(i, _):
        r0 = pl.multiple_of(i * t, t)
        rows = pl.ds(r0, t)
        nk = i + 1

        qi = qi_s[rows, :]
        for h in range(IDX_HEADS):
            in_head = (lane2 >= h * IDX_DIM) & (lane2 < (h + 1) * IDX_DIM)
            qm_s[h] = jnp.where(in_head, qi, jnp.zeros_like(qi))
        w_rows = [wt_s[SIDE_WIDX + h:SIDE_WIDX + h + 1, rows] for h in range(IDX_HEADS)]

        def score_chunk(kc, diagonal):
            c0 = pl.multiple_of(kc * t, t)
            kt = kt_s[pl.ds(c0, t), :]
            acc = jnp.zeros((t, t), F32)
            for h in range(IDX_HEADS):
                acc = acc + w_rows[h] * jnp.maximum(_dot_nt(kt, qm_s[h]), 0.0)
            bits = lax.bitcast_convert_type(acc, I32)
            key = bits ^ (lax.shift_right_arithmetic(bits, 31) & 0x7FFFFFFF)
            key = jnp.where(key == -1, 0, key)
            if diagonal:
                key = jnp.where(key_le_query, key, INT_MIN)
            key_s[pl.ds(c0, t), :] = key
            hi_s[pl.ds(c0, t), :] = lax.shift_right_arithmetic(key, 16).astype(I16)

        def score_body(kc, _):
            score_chunk(kc, False)
            return 0

        lax.fori_loop(0, i, score_body, 0)
        score_chunk(i, True)

        def count(pred):
            def body(kc, acc):
                c0 = pl.multiple_of(kc * t, t)
                return acc + _tree_fold(jnp.where(pred(key_s[pl.ds(c0, t), :]), 1.0, 0.0), jnp.add)
            acc = lax.fori_loop(0, nk, body, jnp.zeros((SUBLANES, t), F32))
            return jnp.sum(acc, axis=0, keepdims=True)

        def count16(src, cand):
            def body(kc, acc):
                c0 = pl.multiple_of(kc * t, t)
                hit = jnp.where(src[pl.ds(c0, t), :] >= cand, jnp.ones((), BF16), jnp.zeros((), BF16))
                return acc + _tree_fold(hit, jnp.add, PACKED_ROWS)
            acc = lax.fori_loop(0, nk, body, jnp.zeros((PACKED_ROWS, t), BF16))
            return jnp.sum(acc.astype(F32), axis=0, keepdims=True)

        def search16(src):
            def step(p, thr):
                cand = thr + lax.shift_left(jnp.int32(1), 15 - p)
                return jnp.where(count16(src, cand.astype(I16)) >= top_k, cand, thr)
            return lax.fori_loop(0, 16, step, jnp.full((1, t), INT16_MIN, I32))

        t_hi = search16(hi_s)

        def low_body(kc, _):
            c0 = pl.multiple_of(kc * t, t)
            key = key_s[pl.ds(c0, t), :]
            hi = lax.shift_right_arithmetic(key, 16)
            lo = (key & 0xFFFF) + INT16_MIN
            lo_s[pl.ds(c0, t), :] = jnp.where(hi > t_hi, INT16_MAX, jnp.where(hi == t_hi, lo, INT16_MIN)).astype(I16)
            return 0

        lax.fori_loop(0, nk, low_body, 0)
        t_lo = search16(lo_s)
        thr = lax.shift_left(t_hi, 16) | ((t_lo - INT16_MIN) & 0xFFFF)
        thr = jnp.maximum(thr, INT_MIN + 1)

        def mask_body(kc, _):
            c0 = pl.multiple_of(kc * t, t)
            mb_s[pl.ds(c0, t), :] = jnp.where(key_s[pl.ds(c0, t), :] >= thr, 0.0, NEG)
            return 0

        lax.fori_loop(0, nk, mask_body, 0)

        n_ge = count(lambda k: k >= thr)

        @pl.when(jnp.max(n_ge) > top_k)
        def _():
            room = top_k - count(lambda k: k > thr)

            def tie_body(kc, seen):
                c0 = pl.multiple_of(kc * t, t)
                tie = key_s[pl.ds(c0, t), :] == thr
                rank = seen + _dot(tri_ref[...], jnp.where(tie, 1.0, 0.0).astype(BF16))
                drop = tie & (rank > room)
                mb_s[pl.ds(c0, t), :] = jnp.where(drop, NEG, mb_s[pl.ds(c0, t), :])
                return rank[t - 1:t, :]

            lax.fori_loop(0, nk, tie_body, jnp.zeros((1, t), F32))

        for h in range(n_heads):
            half = (lane >= HEAD_DIM) if h % 2 else (lane < HEAD_DIM)
            q2 = qb_s[rows, (h // 2) * LANES:(h // 2 + 1) * LANES]
            qa_s[h] = jnp.where(half, q2, qfeat_ref[h, rows, :])
            m_s[h] = jnp.full((1, t), NEG, F32)
            acc_s[h] = jnp.zeros((HEAD_DIM + ONES_ROWS, t), F32)

        def attend(kc, _):
            c0 = pl.multiple_of(kc * t, t)
            cols = pl.ds(c0, t)
            states = [(m_s[h], acc_s[h]) for h in range(n_heads)]
            mask = mb_s[cols, :]
            k_ops = (ka_s[cols, :], kb_s[cols, :])
            s_ts = [_dot_nt(k_ops[h % 2], qa_s[h]) + mask for h in range(n_heads)]
            v_list = [vt_s[:, cols]] * n_heads
            for h, st in enumerate(_online_update_heads(states, s_ts, v_list)):
                m_s[h], acc_s[h] = st
            return 0

        lax.fori_loop(0, nk, attend, 0)

        for hp in range(n_heads // 2):
            cols = slice(hp * LANES, (hp + 1) * LANES)
            out = _finish_pair(acc_s[2 * hp], acc_s[2 * hp + 1]) * _silu(z_ref[0, rows, cols].astype(F32))
            o_ref[0, rows, cols] = out.astype(o_ref.dtype)
        return 0

    lax.fori_loop(0, seq // t, q_block, 0)


def _dsa_attention(side, main, w, tile, tri, batch, seq):
    assert seq // POS_SPLIT <= 256
    assert seq // PACKED_ROWS <= 256
    top_k = min(IDX_TOPK_MAX, seq // 4)
    slopes = tuple(float(2.0 ** (-8.0 * (h + 1) / B_HEADS)) * LOG2E for h in range(B_HEADS))
    t = ATT_T
    const = lambda a: pl.BlockSpec(a.shape, lambda b: (0,) * a.ndim)
    pos = jnp.arange(seq, dtype=I32).reshape(seq, 1)
    lane = jnp.arange(LANES, dtype=I32).reshape(1, LANES)
    qfeat = jnp.stack([_query_pos_features(pos, lane, 0 if h % 2 else HEAD_DIM, slopes[h])
                       for h in range(B_HEADS)]).astype(BF16)
    weights = [w["g_cq"], w["w_uq"], w["w_uq_idx"], w["g_ckv"], w["w_kv"], w["w_vk"], w["w_v_t"], tile, tri, qfeat]
    return pl.pallas_call(
        functools.partial(_dsa_kernel, top_k=top_k),
        grid=(batch,),
        in_specs=[pl.BlockSpec((1, seq, 512), lambda b: (b, 0, 0)),
                  pl.BlockSpec((1, seq, B_W), lambda b: (b, 0, MAIN_ZB // B_W))]
                 + [const(a) for a in weights],
        out_specs=pl.BlockSpec((1, seq, B_W), lambda b: (b, 0, 0)),
        out_shape=jax.ShapeDtypeStruct((batch, seq, B_W), BF16),
        scratch_shapes=[pltpu.VMEM((seq, B_W), BF16),
                        pltpu.VMEM((seq, IDX_HEADS * IDX_DIM), BF16),
                        pltpu.VMEM((seq, IDX_HEADS * IDX_DIM), BF16),
                        pltpu.VMEM((seq, LANES), BF16),
                        pltpu.VMEM((seq, LANES), BF16),
                        pltpu.VMEM((HEAD_DIM + ONES_ROWS, seq), BF16),
                        pltpu.VMEM((LANES, seq), F32),
                        pltpu.VMEM((IDX_HEADS, t, IDX_HEADS * IDX_DIM), BF16),
                        pltpu.VMEM((B_HEADS, t, LANES), BF16),
                        pltpu.VMEM((seq, t), I32),
                        pltpu.VMEM((seq, t), I16),
                        pltpu.VMEM((seq, t), I16),
                        pltpu.VMEM((seq, t), F32),
                        pltpu.VMEM((B_HEADS, 1, t), F32),
                        pltpu.VMEM((B_HEADS, HEAD_DIM + ONES_ROWS, t), F32)],
        compiler_params=_cparams(1),
        name="dsa_attention",
    )(side, main, *weights)


C_UNROLL = 8


def _dilated_kernel(q_ref, k_ref, v_ref, bias_ref, z0_ref, z1_ref, z2_ref, o_ref, o_s, l_s):
    seq = q_ref.shape[1]
    sp = pl.program_id(1)
    grp = pl.program_id(2)
    lane = lax.broadcasted_iota(I32, (1, LANES), 1)
    cw = CLASS_WINDOW

    def rows(start, d):
        return pl.ds(start, cw) if d == 1 else pl.ds(start, cw, stride=d)

    def group_body(g, d):
        per_class = seq // (d * cw)

        ld = lambda ref, s0: ref[0, rows(s0, d), :].astype(BF16)

        def load_block(n, u, before):
            r = n // per_class
            ib = n % per_class
            start = r + ib * (cw * d)
            b = dict(start=start, q=ld(q_ref, start), kc=ld(k_ref, start), vc=ld(v_ref, start), prev=None)
            if per_class <= C_UNROLL:
                if u % per_class:
                    b.update(prev="static", kp=before["kc"], vp=before["vc"])
            elif u:
                b.update(prev="static", kp=before["kc"], vp=before["vc"])
            else:
                pstart = r + jnp.maximum(ib - 1, 0) * (cw * d)
                b.update(prev="dynamic", kp=ld(k_ref, pstart), vp=ld(v_ref, pstart), has_prev=ib > 0)
            return b

        def blk(it, _):
            blocks = []
            for u in range(C_UNROLL):
                blocks.append(load_block(it * C_UNROLL + u, u, blocks[-1] if blocks else None))
            logits = []
            for b in blocks:
                for e in range(2):
                    half = (lane >= HEAD_DIM) if e else (lane < HEAD_DIM)
                    qm = jnp.where(half, b["q"], jnp.zeros_like(b["q"]))
                    s_cur = _dot_nt(qm, b["kc"]) + bias_ref[0, e, :, cw:]
                    s_prev = None
                    if b["prev"] == "static":
                        s_prev = _dot_nt(qm, b["kp"]) + bias_ref[0, e, :, :cw]
                    elif b["prev"] == "dynamic":
                        s_prev = _dot_nt(qm, b["kp"]) + jnp.where(b["has_prev"], bias_ref[0, e, :, :cw], NEG)
                    logits.append((s_cur, s_prev))
            probs = []
            for s_cur, s_prev in logits:
                if s_prev is None:
                    m = jnp.max(s_cur, axis=-1, keepdims=True)
                    p_cur, p_prev = jnp.exp(s_cur - m), None
                    l = jnp.sum(p_cur, axis=-1, keepdims=True)
                else:
                    m = jnp.max(jnp.maximum(s_cur, s_prev), axis=-1, keepdims=True)
                    p_cur, p_prev = jnp.exp(s_cur - m), jnp.exp(s_prev - m)
                    l = jnp.sum(p_cur + p_prev, axis=-1, keepdims=True)
                    p_prev = p_prev.astype(BF16)
                probs.append((p_cur.astype(BF16), p_prev, 1.0 / l, m + jnp.log(l)))
            for u, b in enumerate(blocks):
                outs = []
                for e in range(2):
                    p_cur, p_prev, inv_l, _ = probs[2 * u + e]
                    o = _dot(p_cur, b["vc"])
                    if p_prev is not None:
                        o = o + _dot(p_prev, b["vp"])
                    outs.append(o * inv_l)
                o_s[g, rows(b["start"], d), :] = jnp.where(lane < HEAD_DIM, outs[0], outs[1])
                l_s[g, rows(b["start"], d), :] = jnp.where(lane < HEAD_DIM, probs[2 * u][3], probs[2 * u + 1][3])
            return 0

        lax.fori_loop(0, seq // (cw * C_UNROLL), blk, 0)

    for g, (_, d) in enumerate(C_GROUPS):
        pl.when(grp == g)(functools.partial(group_body, g, d))

    n_grp = len(C_GROUPS)
    z_refs = (z0_ref, z1_ref, z2_ref)

    def combine(sp_static):
        def chunk(c, _):
            r0 = pl.multiple_of(c * 256, 256)
            ls = [l_s[g, pl.ds(r0, 256), :] for g in range(n_grp)]
            m = functools.reduce(jnp.maximum, ls)
            ws = [jnp.exp(x - m) for x in ls]
            inv = 1.0 / functools.reduce(lambda a, b: a + b, ws)
            for g in range(n_grp):
                col = (g * 2 + sp_static) * LANES
                y = (ws[g] * inv) * o_s[g, pl.ds(r0, 256), :]
                y = y * _silu(z_refs[g][0, pl.ds(r0, 256), :].astype(F32))
                o_ref[0, pl.ds(r0, 256), col:col + LANES] = y.astype(o_ref.dtype)
            return 0

        lax.fori_loop(0, seq // 256, chunk, 0)

    for s in range(2):
        pl.when((grp == n_grp - 1) & (sp == s))(functools.partial(combine, s))


def _dilated_attention(side, main, bias, batch, seq):
    assert seq % (C_GROUPS[-1][1] * CLASS_WINDOW) == 0
    assert (seq // CLASS_WINDOW) % C_UNROLL == 0
    n_grp = len(C_GROUPS)
    blk = lambda base: pl.BlockSpec(
        (1, seq, LANES), lambda b, s, g, base=base: (b, 0, base // LANES + 2 * g + s))
    zblk = lambda g: pl.BlockSpec(
        (1, seq, LANES), lambda b, s, _, g=g: (b, 0, MAIN_ZC // LANES + 2 * g + s))
    return pl.pallas_call(
        _dilated_kernel,
        grid=(batch, 2, n_grp),
        in_specs=[blk(SIDE_QC), blk(SIDE_KC), blk(SIDE_VC),
                  pl.BlockSpec((1, 2, CLASS_WINDOW, 2 * CLASS_WINDOW), lambda b, s, g: (2 * g + s, 0, 0, 0)),
                  zblk(0), zblk(1), zblk(2)],
        out_specs=pl.BlockSpec((1, seq, C_W), lambda b, s, g: (b, 0, 0)),
        out_shape=jax.ShapeDtypeStruct((batch, seq, C_W), BF16),
        scratch_shapes=[pltpu.VMEM((n_grp, seq, LANES), F32),
                        pltpu.VMEM((n_grp, seq, LANES), F32)],
        compiler_params=_cparams(3),
        name="dilated_attention",
    )(side, side, side, bias, main, main, main)


def _dilated_bias():
    cw = CLASS_WINDOW
    slopes = np.exp2(-8.0 * (np.arange(C_HEADS, dtype=np.float32) + 1.0) / C_HEADS).astype(np.float32)
    i = np.arange(cw)[:, None]
    j = np.arange(cw)[None, :]
    out = np.empty((C_HEADS // 2, 2, cw, 2 * cw), np.float32)
    for h in range(C_HEADS):
        d = C_GROUPS[h // C_HEADS_PER_GROUP][1]
        prev = np.where(j >= i, -slopes[h] * np.float32(d) * (cw + i - j).astype(np.float32), NEG)
        cur = np.where(j <= i, -slopes[h] * np.float32(d) * (i - j).astype(np.float32), NEG)
        out[h // 2, h % 2] = np.concatenate([prev, cur], axis=1)
    return jnp.asarray(out)


def _merge_kernel(x_ref, ya_ref, yb_ref, yc_ref, g_ref, wa_ref, wb_ref, wc_ref, wo_ref, fg_ref, o_ref,
                  *, final):
    gate = lambda n: _sigmoid(g_ref[:, n * D_MODEL:(n + 1) * D_MODEL].astype(F32))
    merged = (gate(0) * _dot(ya_ref[...], wa_ref[...])
              + gate(1) * _dot(yb_ref[...], wb_ref[...])
              + gate(2) * _dot(yc_ref[...], wc_ref[...]))
    y = x_ref[...] + _dot(merged.astype(BF16), wo_ref[...])
    if final:
        y = _rms(y, fg_ref[...])
    o_ref[...] = y


def _merge(x2d, ya, yb, yc, main, w_a, w_b, w_c, w_o, final_g, final, tm=512):
    m = x2d.shape[0]
    row = lambda w: pl.BlockSpec((tm, w), lambda i: (i, 0))
    const = lambda a: pl.BlockSpec(a.shape, lambda i: (0, 0))
    return pl.pallas_call(
        functools.partial(_merge_kernel, final=final),
        grid=(m // tm,),
        in_specs=[row(D_MODEL), row(A_W), row(B_W), row(C_W), row(3 * D_MODEL),
                  const(w_a), const(w_b), const(w_c), const(w_o), const(final_g)],
        out_specs=row(D_MODEL),
        out_shape=jax.ShapeDtypeStruct((m, D_MODEL), F32),
        compiler_params=_cparams(1),
        name="gated_merge",
    )(x2d, ya, yb, yc, main, w_a, w_b, w_c, w_o, final_g)


def _layer_weights(norm_g, w_in, b_forget, g_cq, w_uq, w_uq_idx, g_ckv, w_ukv, w_a, w_b, w_c, w_o):
    points = np.cumsum(IN_WIDTHS)[:-1].tolist()
    (wqa, wka, wva, wfa, wza, wcq, wckv, wkidx, wwidx, wzb,
     wqc, wkc, wvc, wzc, wga, wgb, wgc) = jnp.split(w_in, points, axis=1)
    zeros = lambda n: jnp.zeros((D_MODEL, n), F32)
    w_main = jnp.concatenate([wga, wgb, wgc, wzb, wqa * (SCALE * LOG2E), wka, wva, wza, wzc], axis=1)
    w_side = jnp.concatenate(
        [wcq, wckv, wkidx, wwidx, zeros(LANES - IDX_DIM - IDX_HEADS),
         wfa, zeros(LANES - A_HEADS), wqc * SCALE, wkc, wvc, zeros(SIDE_N - SIDE_VC - C_W)], axis=1)
    assert w_main.shape[1] == MAIN_N and w_side.shape[1] == SIDE_N
    wk, wv = w_ukv[:, :HEAD_DIM], w_ukv[:, HEAD_DIM:]
    w_kv = jnp.concatenate([wk, wv], axis=1).astype(BF16)
    w_vk = jnp.concatenate([wv, wk], axis=1).astype(BF16)
    return dict(
        norm_g=norm_g.reshape(1, D_MODEL),
        w_main=w_main.astype(BF16), w_side=w_side.astype(BF16),
        b_pad=jnp.zeros((1, LANES), F32).at[0, :A_HEADS].set(b_forget),
        g_cq=g_cq.reshape(1, B_Q_RANK), w_uq=(w_uq * (SCALE * LOG2E)).astype(BF16), w_uq_idx=w_uq_idx.astype(BF16),
        g_ckv=g_ckv.reshape(1, B_KV_RANK), w_kv=w_kv, w_vk=w_vk, w_v_t=wv.T.astype(BF16),
        w_a=w_a.astype(BF16), w_b=w_b.astype(BF16), w_c=w_c.astype(BF16), w_o=w_o.astype(BF16))


def _constants():
    r = np.arange(LANES)
    tri128 = (r[None, :] <= r[:, None]).astype(np.float32)
    r2 = np.arange(ATT_T)
    tri_t = (r2[None, :] <= r2[:, None]).astype(np.float32)
    c = np.arange(IDX_HEADS * IDX_DIM)
    tile = ((r[:, None] == c[None, :] % IDX_DIM) & (r[:, None] < IDX_DIM)).astype(np.float32)
    sel = np.zeros((A_HEADS, LANES, 2 * LANES), np.float32)
    for h in range(A_HEADS):
        for piece in range(3):
            sel[h, piece * A_HEADS + h, piece] = 1.0
            sel[h, piece * A_HEADS + h, LANES + 3 + piece] = -1.0
    one = np.zeros((1, 2 * LANES), np.float32)
    one[0, 3:6] = 1.0
    one[0, LANES:LANES + 3] = 1.0
    bf = lambda a: jnp.asarray(a, BF16)
    cum = (bf(tri128), bf(sel), jnp.asarray(one))
    return cum, bf(tri_t), bf(tile), _dilated_bias()


def _hybrid_layer(x2d, batch, seq, w, consts, final_g, final):
    cum_consts, tri_t, tile, bias_c = consts
    main = _rms_matmul(x2d, w["norm_g"], w["w_main"], BF16, tm=1024, tn=1280)
    side = _rms_matmul(x2d, w["norm_g"], w["w_side"], F32, tm=1024, tn=1536)
    main3 = main.reshape(batch, seq, MAIN_N)
    side3 = side.reshape(batch, seq, SIDE_N)
    qf, kf = _cum_forget(side3, w["b_pad"], cum_consts, batch, seq)
    ya = _fox_attention(main3, qf, kf, batch, seq)
    yb = _dsa_attention(side3, main3, w, tile, tri_t, batch, seq)
    yc = _dilated_attention(side3, main3, bias_c, batch, seq)
    m = batch * seq
    return _merge(x2d, ya.reshape(m, A_W), yb.reshape(m, B_W), yc.reshape(m, C_W), main,
                  w["w_a"], w["w_b"], w["w_c"], w["w_o"], final_g, final)


def kernel(x, norm_g, w_in, b_forget, g_cq, w_uq, w_uq_idx, g_ckv, w_ukv, w_a, w_b, w_c, w_o, final_g):
    batch, seq, d_model = x.shape
    assert d_model == D_MODEL and seq % ATT_T == 0 and seq % FOX_T == 0
    depth = norm_g.shape[0]
    consts = _constants()
    fg = final_g.reshape(1, D_MODEL)
    x2d = x.reshape(batch * seq, D_MODEL)
    for l in range(depth):
        w = _layer_weights(norm_g[l], w_in[l], b_forget[l], g_cq[l], w_uq[l], w_uq_idx[l], g_ckv[l],
                           w_ukv[l], w_a[l], w_b[l], w_c[l], w_o[l])
        x2d = _hybrid_layer(x2d, batch, seq, w, consts, fg, final=(l == depth - 1))
    return x2d.reshape(batch, seq, D_MODEL)
```

```python
import functools

import numpy as np
import jax
import jax.numpy as jnp
from jax import lax
from jax.experimental import pallas as pl
from jax.experimental.pallas import tpu as pltpu

F32 = jnp.float32
BF16 = jnp.bfloat16
I32 = jnp.int32
I16 = jnp.int16

D_MODEL = 1024
HEAD_DIM = 64
EPS = 1e-6
A_HEADS = 8
A_W = A_HEADS * HEAD_DIM
B_HEADS = 8
B_W = B_HEADS * HEAD_DIM
B_Q_RANK = 256
B_KV_RANK = 128
IDX_HEADS = 8
IDX_DIM = 32
IDX_TOPK_MAX = 256
C_GROUPS = ((128, 1), (512, 4), (2048, 16))
C_HEADS_PER_GROUP = 4
C_HEADS = C_HEADS_PER_GROUP * len(C_GROUPS)
C_W = C_HEADS * HEAD_DIM
IN_WIDTHS = (A_W, A_W, A_W, A_HEADS, A_W,
             B_Q_RANK, B_KV_RANK, IDX_DIM, IDX_HEADS, B_W,
             C_W, C_W, C_W, C_W,
             D_MODEL, D_MODEL, D_MODEL)

LANES = 128
SUBLANES = 8
PACKED_ROWS = 16
SCALE = HEAD_DIM ** -0.5
LOG2E = float(np.log2(np.e))
ONES_ROWS = 16
NEG = -1e30
INT_MIN = -2 ** 31
INT16_MIN, INT16_MAX = -2 ** 15, 2 ** 15 - 1
CLASS_WINDOW = 128
assert all(w // d == CLASS_WINDOW for w, d in C_GROUPS)
ATT_T = 256
FOX_T = 512
POS_SPLIT = 64

MAIN_G = 0
MAIN_ZB = 3 * D_MODEL
MAIN_QA = MAIN_ZB + B_W
MAIN_KA = MAIN_QA + A_W
MAIN_VA = MAIN_KA + A_W
MAIN_ZA = MAIN_VA + A_W
MAIN_ZC = MAIN_ZA + A_W
MAIN_N = MAIN_ZC + C_W
SIDE_CQ = 0
SIDE_MISC = B_Q_RANK + B_KV_RANK
SIDE_WIDX = IDX_DIM
SIDE_FA = 512
SIDE_QC = SIDE_FA + LANES
SIDE_KC = SIDE_QC + C_W
SIDE_VC = SIDE_KC + C_W
SIDE_N = 3072

VMEM_LIMIT = 56 * 1024 * 1024


def _cparams(n_axes, vmem=VMEM_LIMIT):
    return pltpu.CompilerParams(dimension_semantics=("arbitrary",) * n_axes,
                                vmem_limit_bytes=vmem)


def _dot(a, b):
    return jnp.dot(a, b, preferred_element_type=F32)


def _dot_nt(a, b):
    return lax.dot_general(a, b, (((1,), (1,)), ((), ())), preferred_element_type=F32)


def _sigmoid(x):
    return 1.0 / (1.0 + jnp.exp(-x))


def _silu(x):
    return x * _sigmoid(x)


def _rms(x, g):
    return x * lax.rsqrt(jnp.mean(x * x, axis=-1, keepdims=True) + EPS) * g


def _split3(x):
    hi = x.astype(BF16)
    r1 = x - hi.astype(F32)
    mid = r1.astype(BF16)
    lo = (r1 - mid.astype(F32)).astype(BF16)
    return hi, mid, lo


def _fold8(x, op):
    n, t = x.shape
    return op(x.reshape(n // SUBLANES, SUBLANES, t), axis=0)


def _tree_fold(x, op, rows=SUBLANES):
    n, t = x.shape
    parts = [x[r:r + rows, :] for r in range(0, n, rows)]
    while len(parts) > 1:
        nxt = [op(parts[j], parts[j + 1]) for j in range(0, len(parts) - 1, 2)]
        if len(parts) % 2:
            nxt.append(parts[-1])
        parts = nxt
    return parts[0]


def _keys_max(x):
    return jnp.max(_fold8(x, jnp.max), axis=0, keepdims=True)


def _online_update_heads(states, s_ts, v_ts):
    ps, scaled = [], []
    for (m, acc), s_t in zip(states, s_ts):
        m_new = jnp.maximum(m, _keys_max(s_t))
        alpha = jnp.exp2(m - m_new)
        ps.append(jnp.exp2(s_t - m_new).astype(BF16))
        scaled.append((m_new, alpha * acc))
    return [(m, acc + _dot(v_t, p)) for (m, acc), v_t, p in zip(scaled, v_ts, ps)]


def _finish_pair(acc0, acc1):
    norm = lambda acc: acc[:HEAD_DIM, :] * (1.0 / acc[HEAD_DIM:HEAD_DIM + 1, :])
    return jnp.concatenate([norm(acc0), norm(acc1)], axis=0).T


def _rms_matmul_kernel(x_ref, g_ref, w_ref, o_ref, h_ref):
    @pl.when(pl.program_id(1) == 0)
    def _():
        h_ref[...] = _rms(x_ref[...], g_ref[...]).astype(BF16)

    o_ref[...] = _dot(h_ref[...], w_ref[...]).astype(o_ref.dtype)


def _rms_matmul(x2d, g, w, out_dtype, tm, tn):
    m, k = x2d.shape
    n = w.shape[1]
    return pl.pallas_call(
        _rms_matmul_kernel,
        grid=(m // tm, n // tn),
        in_specs=[pl.BlockSpec((tm, k), lambda i, j: (i, 0)),
                  pl.BlockSpec((1, k), lambda i, j: (0, 0)),
                  pl.BlockSpec((k, tn), lambda i, j: (0, j))],
        out_specs=pl.BlockSpec((tm, tn), lambda i, j: (i, j)),
        out_shape=jax.ShapeDtypeStruct((m, n), out_dtype),
        scratch_shapes=[pltpu.VMEM((tm, k), BF16)],
        compiler_params=_cparams(2),
        name="rms_in_proj",
    )(x2d, g, w)


def _cum_kernel(fa_ref, b_ref, tri_ref, sel_ref, one_ref, qf_ref, kf_ref):
    seq = fa_ref.shape[1]
    tri = tri_ref[...]
    lane = lax.broadcasted_iota(I32, (1, LANES), 1)
    carry = jnp.zeros((1, LANES), F32)
    for blk in range(seq // LANES):
        rows = slice(blk * LANES, (blk + 1) * LANES)
        x = fa_ref[0, rows, :] + b_ref[...]
        lf = jnp.minimum(x, 0.0) - jnp.log(1.0 + jnp.exp(-jnp.abs(x)))
        hi, mid, lo = _split3(lf)
        c = _dot(tri, hi) + _dot(tri, mid) + _dot(tri, lo) + carry
        carry = c[LANES - 1:LANES, :]
        hi, mid, lo = _split3(jnp.where(lane < A_HEADS, c * LOG2E, 0.0))
        c3 = (hi.astype(F32) + pltpu.roll(mid.astype(F32), A_HEADS, 1)
              + pltpu.roll(lo.astype(F32), 2 * A_HEADS, 1)).astype(BF16)
        for h in range(A_HEADS):
            feats = _dot(c3, sel_ref[h]) + one_ref[...]
            qf_ref[0, h, rows, :] = feats[:, :LANES].astype(BF16)
            kf_ref[0, h, rows, :] = feats[:, LANES:].astype(BF16)


def _cum_forget(side, b_pad, consts, batch, seq):
    tri, sel, one = consts
    full = lambda a: pl.BlockSpec(a.shape, lambda b: (0,) * a.ndim)
    feat = jax.ShapeDtypeStruct((batch, A_HEADS, seq, LANES), BF16)
    return pl.pallas_call(
        _cum_kernel,
        grid=(batch,),
        in_specs=[pl.BlockSpec((1, seq, LANES), lambda b: (b, 0, SIDE_FA // LANES)),
                  full(b_pad), full(tri), full(sel), full(one)],
        out_specs=[pl.BlockSpec((1, A_HEADS, seq, LANES), lambda b: (b, 0, 0, 0))] * 2,
        out_shape=[feat, feat],
        compiler_params=_cparams(1),
        name="fox_cumsum",
    )(side, b_pad, tri, sel, one)


def _fox_kernel(q_ref, k_ref, v_ref, z_ref, qf_ref, kf_ref, o_ref, vt_s, qa_s, m_s, acc_s):
    seq = q_ref.shape[1]
    t = FOX_T
    n_heads = A_HEADS
    lane = lax.broadcasted_iota(I32, (1, LANES), 1)
    key_le_query = (lax.broadcasted_iota(I32, (t, t), 0) <= lax.broadcasted_iota(I32, (t, t), 1))
    pair_cols = lambda h: slice((h // 2) * LANES, (h // 2 + 1) * LANES)

    def transpose_v(c, _):
        cols = pl.ds(pl.multiple_of(c * t, t), t)
        for p in range(n_heads // 2):
            pc = slice(p * LANES, (p + 1) * LANES)
            v_t = v_ref[0, cols, pc].astype(F32).T.astype(BF16)
            for e in range(2):
                vt_s[2 * p + e, :HEAD_DIM, cols] = v_t[e * HEAD_DIM:(e + 1) * HEAD_DIM, :]
                vt_s[2 * p + e, HEAD_DIM:, cols] = jnp.ones((ONES_ROWS, t), BF16)
        return 0

    lax.fori_loop(0, seq // t, transpose_v, 0)

    def q_block(i, _):
        rows = pl.ds(pl.multiple_of(i * t, t), t)
        for h in range(n_heads):
            half = (lane >= HEAD_DIM) if h % 2 else (lane < HEAD_DIM)
            q2 = q_ref[0, rows, pair_cols(h)]
            qa_s[h] = jnp.concatenate([jnp.where(half, q2, jnp.zeros_like(q2)), qf_ref[0, h, rows, :]], axis=1)
            m_s[h] = jnp.full((1, t), NEG, F32)
            acc_s[h] = jnp.zeros((HEAD_DIM + ONES_ROWS, t), F32)

        def chunk(kb, diagonal):
            cols = pl.ds(pl.multiple_of(kb * t, t), t)
            states = [(m_s[h], acc_s[h]) for h in range(n_heads)]
            s_ts = [_dot_nt(jnp.concatenate([k_ref[0, cols, pair_cols(h)], kf_ref[0, h, cols, :]], axis=1),
                            qa_s[h]) for h in range(n_heads)]
            if diagonal:
                s_ts = [jnp.where(key_le_query, s_t, NEG) for s_t in s_ts]
            v_ts = [vt_s[h, :, cols] for h in range(n_heads)]
            for h, st in enumerate(_online_update_heads(states, s_ts, v_ts)):
                m_s[h], acc_s[h] = st

        def off_diagonal(kb, _):
            chunk(kb, False)
            return 0

        lax.fori_loop(0, i, off_diagonal, 0)
        chunk(i, True)
        for p in range(n_heads // 2):
            h0, h1 = 2 * p, 2 * p + 1
            out = _finish_pair(acc_s[h0], acc_s[h1]) * _silu(z_ref[0, rows, pair_cols(h0)].astype(F32))
            o_ref[0, rows, pair_cols(h0)] = out.astype(o_ref.dtype)
        return 0

    lax.fori_loop(0, seq // t, q_block, 0)


def _fox_attention(main, qf, kf, batch, seq):
    t = FOX_T
    blk = lambda base: pl.BlockSpec((1, seq, A_W), lambda b, base=base: (b, 0, base // A_W))
    feat = pl.BlockSpec((1, A_HEADS, seq, LANES), lambda b: (b, 0, 0, 0))
    return pl.pallas_call(
        _fox_kernel,
        grid=(batch,),
        in_specs=[blk(MAIN_QA), blk(MAIN_KA), blk(MAIN_VA), blk(MAIN_ZA), feat, feat],
        out_specs=pl.BlockSpec((1, seq, A_W), lambda b: (b, 0, 0)),
        out_shape=jax.ShapeDtypeStruct((batch, seq, A_W), BF16),
        scratch_shapes=[pltpu.VMEM((A_HEADS, HEAD_DIM + ONES_ROWS, seq), BF16),
                        pltpu.VMEM((A_HEADS, t, 2 * LANES), BF16),
                        pltpu.VMEM((A_HEADS, 1, t), F32),
                        pltpu.VMEM((A_HEADS, HEAD_DIM + ONES_ROWS, t), F32)],
        compiler_params=_cparams(1),
        name="fox_attention",
    )(main, main, main, main, qf, kf)


def _lane_group(lane, first):
    rel = lane - first
    return jnp.where((rel >= 0) & (rel < 9), rel // 3, -1)


def _key_pos_features(pos, lane, first):
    hi = lax.shift_right_logical(pos, 6).astype(F32)
    lo = (pos & (POS_SPLIT - 1)).astype(F32)
    grp = _lane_group(lane, first)
    return jnp.where(grp == 0, hi, jnp.where(grp == 1, lo, jnp.where(grp == 2, 1.0, 0.0)))


def _query_pos_features(pos, lane, first, slope):
    grp = _lane_group(lane, first)
    x = jnp.where(grp == 0, POS_SPLIT * slope,
                  jnp.where(grp == 1, slope, jnp.where(grp == 2, -slope * pos.astype(F32), 0.0)))
    hi, mid, lo = _split3(x)
    piece = (lane - first) % 3
    return jnp.where(piece == 0, hi, jnp.where(piece == 1, mid, lo))


def _dsa_kernel(side_ref, z_ref, gcq_ref, wuq_ref, wuqi_ref, gckv_ref, wkv_ref, wvk_ref, wvt_ref,
                tile_ref, tri_ref, qfeat_ref, o_ref,
                qb_s, qi_s, kt_s, ka_s, kb_s, vt_s, wt_s, qm_s, qa_s, key_s, hi_s, lo_s, mb_s, m_s, acc_s,
                *, top_k):
    seq = side_ref.shape[1]
    t = ATT_T
    n_heads = B_HEADS
    lane = lax.broadcasted_iota(I32, (1, LANES), 1)
    lane2 = lax.broadcasted_iota(I32, (1, 2 * LANES), 1)
    key_le_query = (lax.broadcasted_iota(I32, (t, t), 0) <= lax.broadcasted_iota(I32, (t, t), 1))
    row_iota = lax.broadcasted_iota(I32, (t, 1), 0)

    def prep(c, _):
        r0 = pl.multiple_of(c * t, t)
        rows = pl.ds(r0, t)
        blk = side_ref[0, rows, :]
        cqn = _rms(blk[:, SIDE_CQ:SIDE_CQ + B_Q_RANK], gcq_ref[...]).astype(BF16)
        qb_s[rows, :] = _dot(cqn, wuq_ref[...]).astype(BF16)
        qi_s[rows, :] = _dot(cqn, wuqi_ref[...]).astype(BF16)
        kvn = _rms(blk[:, B_Q_RANK:SIDE_MISC], gckv_ref[...]).astype(BF16)
        pos = r0 + row_iota
        ka_s[rows, :] = jnp.where(lane < HEAD_DIM, _dot(kvn, wkv_ref[...]),
                                  _key_pos_features(pos, lane, HEAD_DIM)).astype(BF16)
        kb_s[rows, :] = jnp.where(lane >= HEAD_DIM, _dot(kvn, wvk_ref[...]),
                                  _key_pos_features(pos, lane, 0)).astype(BF16)
        vt_s[:HEAD_DIM, rows] = _dot_nt(wvt_ref[...], kvn).astype(BF16)
        vt_s[HEAD_DIM:, rows] = jnp.ones((ONES_ROWS, t), BF16)
        misc = blk[:, SIDE_MISC:]
        kt_s[rows, :] = _dot(misc.astype(BF16), tile_ref[...]).astype(BF16)
        wt_s[:, rows] = misc.T
        return 0

    lax.fori_loop(0, seq // t, prep, 0)

    def q_block(i, _):
        r0 = pl.multiple_of(i * t, t)
        rows = pl.ds(r0, t)
        nk = i + 1

        qi = qi_s[rows, :]
        for h in range(IDX_HEADS):
            in_head = (lane2 >= h * IDX_DIM) & (lane2 < (h + 1) * IDX_DIM)
            qm_s[h] = jnp.where(in_head, qi, jnp.zeros_like(qi))
        w_rows = [wt_s[SIDE_WIDX + h:SIDE_WIDX + h + 1, rows] for h in range(IDX_HEADS)]

        def score_chunk(kc, diagonal):
            c0 = pl.multiple_of(kc * t, t)
            kt = kt_s[pl.ds(c0, t), :]
            acc = jnp.zeros((t, t), F32)
            for h in range(IDX_HEADS):
                acc = acc + w_rows[h] * jnp.maximum(_dot_nt(kt, qm_s[h]), 0.0)
            bits = lax.bitcast_convert_type(acc, I32)
            key = bits ^ (lax.shift_right_arithmetic(bits, 31) & 0x7FFFFFFF)
            key = jnp.where(key == -1, 0, key)
            if diagonal:
                key = jnp.where(key_le_query, key, INT_MIN)
            key_s[pl.ds(c0, t), :] = key
            hi_s[pl.ds(c0, t), :] = lax.shift_right_arithmetic(key, 16).astype(I16)

        def score_pair(kp, _):
            score_chunk(2 * kp, False)
            score_chunk(2 * kp + 1, False)
            return 0

        lax.fori_loop(0, i // 2, score_pair, 0)

        @pl.when(i % 2 == 1)
        def _():
            score_chunk(i - 1, False)

        score_chunk(i, True)

        def count(pred):
            def body(kc, acc):
                c0 = pl.multiple_of(kc * t, t)
                return acc + _tree_fold(jnp.where(pred(key_s[pl.ds(c0, t), :]), 1.0, 0.0), jnp.add)
            acc = lax.fori_loop(0, nk, body, jnp.zeros((SUBLANES, t), F32))
            return jnp.sum(acc, axis=0, keepdims=True)

        def count16(src, cand):
            def body(kc, acc):
                c0 = pl.multiple_of(kc * t, t)
                hit = jnp.where(src[pl.ds(c0, t), :] >= cand, jnp.ones((), BF16), jnp.zeros((), BF16))
                return acc + _tree_fold(hit, jnp.add, PACKED_ROWS)
            acc = lax.fori_loop(0, nk, body, jnp.zeros((PACKED_ROWS, t), BF16))
            return jnp.sum(acc.astype(F32), axis=0, keepdims=True)

        def search16(src):
            def step(p, thr):
                cand = thr + lax.shift_left(jnp.int32(1), 15 - p)
                return jnp.where(count16(src, cand.astype(I16)) >= top_k, cand, thr)
            return lax.fori_loop(0, 16, step, jnp.full((1, t), INT16_MIN, I32))

        t_hi = search16(hi_s)

        def low_body(kc, _):
            c0 = pl.multiple_of(kc * t, t)
            key = key_s[pl.ds(c0, t), :]
            hi = lax.shift_right_arithmetic(key, 16)
            lo = (key & 0xFFFF) + INT16_MIN
            lo_s[pl.ds(c0, t), :] = jnp.where(hi > t_hi, INT16_MAX, jnp.where(hi == t_hi, lo, INT16_MIN)).astype(I16)
            return 0

        lax.fori_loop(0, nk, low_body, 0)
        t_lo = search16(lo_s)
        thr = lax.shift_left(t_hi, 16) | ((t_lo - INT16_MIN) & 0xFFFF)
        thr = jnp.maximum(thr, INT_MIN + 1)

        def mask_body(kc, _):
            c0 = pl.multiple_of(kc * t, t)
            mb_s[pl.ds(c0, t), :] = jnp.where(key_s[pl.ds(c0, t), :] >= thr, 0.0, NEG)
            return 0

        lax.fori_loop(0, nk, mask_body, 0)

        n_ge = count(lambda k: k >= thr)

        @pl.when(jnp.max(n_ge) > top_k)
        def _():
            room = top_k - count(lambda k: k > thr)

            def tie_body(kc, seen):
                c0 = pl.multiple_of(kc * t, t)
                tie = key_s[pl.ds(c0, t), :] == thr
                rank = seen + _dot(tri_ref[...], jnp.where(tie, 1.0, 0.0).astype(BF16))
                drop = tie & (rank > room)
                mb_s[pl.ds(c0, t), :] = jnp.where(drop, NEG, mb_s[pl.ds(c0, t), :])
                return rank[t - 1:t, :]

            lax.fori_loop(0, nk, tie_body, jnp.zeros((1, t), F32))

        for h in range(n_heads):
            half = (lane >= HEAD_DIM) if h % 2 else (lane < HEAD_DIM)
            q2 = qb_s[rows, (h // 2) * LANES:(h // 2 + 1) * LANES]
            qa_s[h] = jnp.where(half, q2, qfeat_ref[h, rows, :])
            m_s[h] = jnp.full((1, t), NEG, F32)
            acc_s[h] = jnp.zeros((HEAD_DIM + ONES_ROWS, t), F32)

        def attend(kc, _):
            c0 = pl.multiple_of(kc * t, t)
            cols = pl.ds(c0, t)
            states = [(m_s[h], acc_s[h]) for h in range(n_heads)]
            mask = mb_s[cols, :]
            k_ops = (ka_s[cols, :], kb_s[cols, :])
            s_ts = [_dot_nt(k_ops[h % 2], qa_s[h]) + mask for h in range(n_heads)]
            v_list = [vt_s[:, cols]] * n_heads
            for h, st in enumerate(_online_update_heads(states, s_ts, v_list)):
                m_s[h], acc_s[h] = st
            return 0

        def attend_pair(kp, _):
            attend(2 * kp, 0)
            attend(2 * kp + 1, 0)
            return 0

        lax.fori_loop(0, nk // 2, attend_pair, 0)

        @pl.when(nk % 2 == 1)
        def _():
            attend(nk - 1, 0)

        for hp in range(n_heads // 2):
            cols = slice(hp * LANES, (hp + 1) * LANES)
            out = _finish_pair(acc_s[2 * hp], acc_s[2 * hp + 1]) * _silu(z_ref[0, rows, cols].astype(F32))
            o_ref[0, rows, cols] = out.astype(o_ref.dtype)
        return 0

    lax.fori_loop(0, seq // t, q_block, 0)


def _dsa_attention(side, main, w, tile, tri, batch, seq):
    assert seq // POS_SPLIT <= 256
    assert seq // PACKED_ROWS <= 256
    top_k = min(IDX_TOPK_MAX, seq // 4)
    slopes = tuple(float(2.0 ** (-8.0 * (h + 1) / B_HEADS)) * LOG2E for h in range(B_HEADS))
    t = ATT_T
    const = lambda a: pl.BlockSpec(a.shape, lambda b: (0,) * a.ndim)
    pos = jnp.arange(seq, dtype=I32).reshape(seq, 1)
    lane = jnp.arange(LANES, dtype=I32).reshape(1, LANES)
    qfeat = jnp.stack([_query_pos_features(pos, lane, 0 if h % 2 else HEAD_DIM, slopes[h])
                       for h in range(B_HEADS)]).astype(BF16)
    weights = [w["g_cq"], w["w_uq"], w["w_uq_idx"], w["g_ckv"], w["w_kv"], w["w_vk"], w["w_v_t"], tile, tri, qfeat]
    return pl.pallas_call(
        functools.partial(_dsa_kernel, top_k=top_k),
        grid=(batch,),
        in_specs=[pl.BlockSpec((1, seq, 512), lambda b: (b, 0, 0)),
                  pl.BlockSpec((1, seq, B_W), lambda b: (b, 0, MAIN_ZB // B_W))]
                 + [const(a) for a in weights],
        out_specs=pl.BlockSpec((1, seq, B_W), lambda b: (b, 0, 0)),
        out_shape=jax.ShapeDtypeStruct((batch, seq, B_W), BF16),
        scratch_shapes=[pltpu.VMEM((seq, B_W), BF16),
                        pltpu.VMEM((seq, IDX_HEADS * IDX_DIM), BF16),
                        pltpu.VMEM((seq, IDX_HEADS * IDX_DIM), BF16),
                        pltpu.VMEM((seq, LANES), BF16),
                        pltpu.VMEM((seq, LANES), BF16),
                        pltpu.VMEM((HEAD_DIM + ONES_ROWS, seq), BF16),
                        pltpu.VMEM((LANES, seq), F32),
                        pltpu.VMEM((IDX_HEADS, t, IDX_HEADS * IDX_DIM), BF16),
                        pltpu.VMEM((B_HEADS, t, LANES), BF16),
                        pltpu.VMEM((seq, t), I32),
                        pltpu.VMEM((seq, t), I16),
                        pltpu.VMEM((seq, t), I16),
                        pltpu.VMEM((seq, t), F32),
                        pltpu.VMEM((B_HEADS, 1, t), F32),
                        pltpu.VMEM((B_HEADS, HEAD_DIM + ONES_ROWS, t), F32)],
        compiler_params=_cparams(1),
        name="dsa_attention",
    )(side, main, *weights)


C_UNROLL = 8


def _dilated_kernel(q_ref, k_ref, v_ref, bias_ref, z0_ref, z1_ref, z2_ref, o_ref, o_s, l_s):
    seq = q_ref.shape[1]
    sp = pl.program_id(1)
    grp = pl.program_id(2)
    lane = lax.broadcasted_iota(I32, (1, LANES), 1)
    cw = CLASS_WINDOW

    def rows(start, d):
        return pl.ds(start, cw) if d == 1 else pl.ds(start, cw, stride=d)

    def group_body(g, d):
        per_class = seq // (d * cw)

        ld = lambda ref, s0: ref[0, rows(s0, d), :].astype(BF16)

        def load_block(n, u, before):
            r = n // per_class
            ib = n % per_class
            start = r + ib * (cw * d)
            b = dict(start=start, q=ld(q_ref, start), kc=ld(k_ref, start), vc=ld(v_ref, start), prev=None)
            if per_class <= C_UNROLL:
                if u % per_class:
                    b.update(prev="static", kp=before["kc"], vp=before["vc"])
            elif u:
                b.update(prev="static", kp=before["kc"], vp=before["vc"])
            else:
                pstart = r + jnp.maximum(ib - 1, 0) * (cw * d)
                b.update(prev="dynamic", kp=ld(k_ref, pstart), vp=ld(v_ref, pstart), has_prev=ib > 0)
            return b

        def blk(it, _):
            blocks = []
            for u in range(C_UNROLL):
                blocks.append(load_block(it * C_UNROLL + u, u, blocks[-1] if blocks else None))
            logits = []
            for b in blocks:
                for e in range(2):
                    half = (lane >= HEAD_DIM) if e else (lane < HEAD_DIM)
                    qm = jnp.where(half, b["q"], jnp.zeros_like(b["q"]))
                    s_cur = _dot_nt(qm, b["kc"]) + bias_ref[0, e, :, cw:]
                    s_prev = None
                    if b["prev"] == "static":
                        s_prev = _dot_nt(qm, b["kp"]) + bias_ref[0, e, :, :cw]
                    elif b["prev"] == "dynamic":
                        s_prev = _dot_nt(qm, b["kp"]) + jnp.where(b["has_prev"], bias_ref[0, e, :, :cw], NEG)
                    logits.append((s_cur, s_prev))
            probs = []
            for s_cur, s_prev in logits:
                if s_prev is None:
                    m = jnp.max(s_cur, axis=-1, keepdims=True)
                    p_cur, p_prev = jnp.exp(s_cur - m), None
                    l = jnp.sum(p_cur, axis=-1, keepdims=True)
                else:
                    m = jnp.max(jnp.maximum(s_cur, s_prev), axis=-1, keepdims=True)
                    p_cur, p_prev = jnp.exp(s_cur - m), jnp.exp(s_prev - m)
                    l = jnp.sum(p_cur + p_prev, axis=-1, keepdims=True)
                    p_prev = p_prev.astype(BF16)
                probs.append((p_cur.astype(BF16), p_prev, 1.0 / l, m + jnp.log(l)))
            for u, b in enumerate(blocks):
                outs = []
                for e in range(2):
                    p_cur, p_prev, inv_l, _ = probs[2 * u + e]
                    o = _dot(p_cur, b["vc"])
                    if p_prev is not None:
                        o = o + _dot(p_prev, b["vp"])
                    outs.append(o * inv_l)
                o_s[g, rows(b["start"], d), :] = jnp.where(lane < HEAD_DIM, outs[0], outs[1])
                l_s[g, rows(b["start"], d), :] = jnp.where(lane < HEAD_DIM, probs[2 * u][3], probs[2 * u + 1][3])
            return 0

        lax.fori_loop(0, seq // (cw * C_UNROLL), blk, 0)

    for g, (_, d) in enumerate(C_GROUPS):
        pl.when(grp == g)(functools.partial(group_body, g, d))

    n_grp = len(C_GROUPS)
    z_refs = (z0_ref, z1_ref, z2_ref)

    def combine(sp_static):
        def chunk(c, _):
            r0 = pl.multiple_of(c * 256, 256)
            ls = [l_s[g, pl.ds(r0, 256), :] for g in range(n_grp)]
            m = functools.reduce(jnp.maximum, ls)
            ws = [jnp.exp(x - m) for x in ls]
            inv = 1.0 / functools.reduce(lambda a, b: a + b, ws)
            for g in range(n_grp):
                col = (g * 2 + sp_static) * LANES
                y = (ws[g] * inv) * o_s[g, pl.ds(r0, 256), :]
                y = y * _silu(z_refs[g][0, pl.ds(r0, 256), :].astype(F32))
                o_ref[0, pl.ds(r0, 256), col:col + LANES] = y.astype(o_ref.dtype)
            return 0

        lax.fori_loop(0, seq // 256, chunk, 0)

    for s in range(2):
        pl.when((grp == n_grp - 1) & (sp == s))(functools.partial(combine, s))


def _dilated_attention(side, main, bias, batch, seq):
    assert seq % (C_GROUPS[-1][1] * CLASS_WINDOW) == 0
    assert (seq // CLASS_WINDOW) % C_UNROLL == 0
    n_grp = len(C_GROUPS)
    blk = lambda base: pl.BlockSpec(
        (1, seq, LANES), lambda b, s, g, base=base: (b, 0, base // LANES + 2 * g + s))
    zblk = lambda g: pl.BlockSpec(
        (1, seq, LANES), lambda b, s, _, g=g: (b, 0, MAIN_ZC // LANES + 2 * g + s))
    return pl.pallas_call(
        _dilated_kernel,
        grid=(batch, 2, n_grp),
        in_specs=[blk(SIDE_QC), blk(SIDE_KC), blk(SIDE_VC),
                  pl.BlockSpec((1, 2, CLASS_WINDOW, 2 * CLASS_WINDOW), lambda b, s, g: (2 * g + s, 0, 0, 0)),
                  zblk(0), zblk(1), zblk(2)],
        out_specs=pl.BlockSpec((1, seq, C_W), lambda b, s, g: (b, 0, 0)),
        out_shape=jax.ShapeDtypeStruct((batch, seq, C_W), BF16),
        scratch_shapes=[pltpu.VMEM((n_grp, seq, LANES), F32),
                        pltpu.VMEM((n_grp, seq, LANES), F32)],
        compiler_params=_cparams(3),
        name="dilated_attention",
    )(side, side, side, bias, main, main, main)


def _dilated_bias():
    cw = CLASS_WINDOW
    slopes = np.exp2(-8.0 * (np.arange(C_HEADS, dtype=np.float32) + 1.0) / C_HEADS).astype(np.float32)
    i = np.arange(cw)[:, None]
    j = np.arange(cw)[None, :]
    out = np.empty((C_HEADS // 2, 2, cw, 2 * cw), np.float32)
    for h in range(C_HEADS):
        d = C_GROUPS[h // C_HEADS_PER_GROUP][1]
        prev = np.where(j >= i, -slopes[h] * np.float32(d) * (cw + i - j).astype(np.float32), NEG)
        cur = np.where(j <= i, -slopes[h] * np.float32(d) * (i - j).astype(np.float32), NEG)
        out[h // 2, h % 2] = np.concatenate([prev, cur], axis=1)
    return jnp.asarray(out)


def _merge_kernel(x_ref, ya_ref, yb_ref, yc_ref, g_ref, wa_ref, wb_ref, wc_ref, wo_ref, fg_ref, o_ref,
                  *, final):
    gate = lambda n: _sigmoid(g_ref[:, n * D_MODEL:(n + 1) * D_MODEL].astype(F32))
    merged = (gate(0) * _dot(ya_ref[...], wa_ref[...])
              + gate(1) * _dot(yb_ref[...], wb_ref[...])
              + gate(2) * _dot(yc_ref[...], wc_ref[...]))
    y = x_ref[...] + _dot(merged.astype(BF16), wo_ref[...])
    if final:
        y = _rms(y, fg_ref[...])
    o_ref[...] = y


def _merge(x2d, ya, yb, yc, main, w_a, w_b, w_c, w_o, final_g, final, tm=512):
    m = x2d.shape[0]
    row = lambda w: pl.BlockSpec((tm, w), lambda i: (i, 0))
    const = lambda a: pl.BlockSpec(a.shape, lambda i: (0, 0))
    return pl.pallas_call(
        functools.partial(_merge_kernel, final=final),
        grid=(m // tm,),
        in_specs=[row(D_MODEL), row(A_W), row(B_W), row(C_W), row(3 * D_MODEL),
                  const(w_a), const(w_b), const(w_c), const(w_o), const(final_g)],
        out_specs=row(D_MODEL),
        out_shape=jax.ShapeDtypeStruct((m, D_MODEL), F32),
        compiler_params=_cparams(1),
        name="gated_merge",
    )(x2d, ya, yb, yc, main, w_a, w_b, w_c, w_o, final_g)


def _layer_weights(norm_g, w_in, b_forget, g_cq, w_uq, w_uq_idx, g_ckv, w_ukv, w_a, w_b, w_c, w_o):
    points = np.cumsum(IN_WIDTHS)[:-1].tolist()
    (wqa, wka, wva, wfa, wza, wcq, wckv, wkidx, wwidx, wzb,
     wqc, wkc, wvc, wzc, wga, wgb, wgc) = jnp.split(w_in, points, axis=1)
    zeros = lambda n: jnp.zeros((D_MODEL, n), F32)
    w_main = jnp.concatenate([wga, wgb, wgc, wzb, wqa * (SCALE * LOG2E), wka, wva, wza, wzc], axis=1)
    w_side = jnp.concatenate(
        [wcq, wckv, wkidx, wwidx, zeros(LANES - IDX_DIM - IDX_HEADS),
         wfa, zeros(LANES - A_HEADS), wqc * SCALE, wkc, wvc, zeros(SIDE_N - SIDE_VC - C_W)], axis=1)
    assert w_main.shape[1] == MAIN_N and w_side.shape[1] == SIDE_N
    wk, wv = w_ukv[:, :HEAD_DIM], w_ukv[:, HEAD_DIM:]
    w_kv = jnp.concatenate([wk, wv], axis=1).astype(BF16)
    w_vk = jnp.concatenate([wv, wk], axis=1).astype(BF16)
    return dict(
        norm_g=norm_g.reshape(1, D_MODEL),
        w_main=w_main.astype(BF16), w_side=w_side.astype(BF16),
        b_pad=jnp.zeros((1, LANES), F32).at[0, :A_HEADS].set(b_forget),
        g_cq=g_cq.reshape(1, B_Q_RANK), w_uq=(w_uq * (SCALE * LOG2E)).astype(BF16), w_uq_idx=w_uq_idx.astype(BF16),
        g_ckv=g_ckv.reshape(1, B_KV_RANK), w_kv=w_kv, w_vk=w_vk, w_v_t=wv.T.astype(BF16),
        w_a=w_a.astype(BF16), w_b=w_b.astype(BF16), w_c=w_c.astype(BF16), w_o=w_o.astype(BF16))


def _constants():
    r = np.arange(LANES)
    tri128 = (r[None, :] <= r[:, None]).astype(np.float32)
    r2 = np.arange(ATT_T)
    tri_t = (r2[None, :] <= r2[:, None]).astype(np.float32)
    c = np.arange(IDX_HEADS * IDX_DIM)
    tile = ((r[:, None] == c[None, :] % IDX_DIM) & (r[:, None] < IDX_DIM)).astype(np.float32)
    sel = np.zeros((A_HEADS, LANES, 2 * LANES), np.float32)
    for h in range(A_HEADS):
        for piece in range(3):
            sel[h, piece * A_HEADS + h, piece] = 1.0
            sel[h, piece * A_HEADS + h, LANES + 3 + piece] = -1.0
    one = np.zeros((1, 2 * LANES), np.float32)
    one[0, 3:6] = 1.0
    one[0, LANES:LANES + 3] = 1.0
    bf = lambda a: jnp.asarray(a, BF16)
    cum = (bf(tri128), bf(sel), jnp.asarray(one))
    return cum, bf(tri_t), bf(tile), _dilated_bias()


def _hybrid_layer(x2d, batch, seq, w, consts, final_g, final):
    cum_consts, tri_t, tile, bias_c = consts
    main = _rms_matmul(x2d, w["norm_g"], w["w_main"], BF16, tm=1024, tn=1280)
    side = _rms_matmul(x2d, w["norm_g"], w["w_side"], F32, tm=1024, tn=1536)
    main3 = main.reshape(batch, seq, MAIN_N)
    side3 = side.reshape(batch, seq, SIDE_N)
    qf, kf = _cum_forget(side3, w["b_pad"], cum_consts, batch, seq)
    ya = _fox_attention(main3, qf, kf, batch, seq)
    yb = _dsa_attention(side3, main3, w, tile, tri_t, batch, seq)
    yc = _dilated_attention(side3, main3, bias_c, batch, seq)
    m = batch * seq
    return _merge(x2d, ya.reshape(m, A_W), yb.reshape(m, B_W), yc.reshape(m, C_W), main,
                  w["w_a"], w["w_b"], w["w_c"], w["w_o"], final_g, final)


def kernel(x, norm_g, w_in, b_forget, g_cq, w_uq, w_uq_idx, g_ckv, w_ukv, w_a, w_b, w_c, w_o, final_g):
    batch, seq, d_model = x.shape
    assert d_model == D_MODEL and seq % ATT_T == 0 and seq % FOX_T == 0
    depth = norm_g.shape[0]
    consts = _constants()
    fg = final_g.reshape(1, D_MODEL)
    x2d = x.reshape(batch * seq, D_MODEL)
    for l in range(depth):
        w = _layer_weights(norm_g[l], w_in[l], b_forget[l], g_cq[l], w_uq[l], w_uq_idx[l], g_ckv[l],
                           w_ukv[l], w_a[l], w_b[l], w_c[l], w_o[l])
        x2d = _hybrid_layer(x2d, batch, seq, w, consts, fg, final=(l == depth - 1))
    return x2d.reshape(batch, seq, D_MODEL)
```

```python
import functools

import numpy as np
import jax
import jax.numpy as jnp
from jax import lax
from jax.experimental import pallas as pl
from jax.experimental.pallas import tpu as pltpu

F32 = jnp.float32
BF16 = jnp.bfloat16
I32 = jnp.int32
I16 = jnp.int16

D_MODEL = 1024
HEAD_DIM = 64
EPS = 1e-6
A_HEADS = 8
A_W = A_HEADS * HEAD_DIM
B_HEADS = 8
B_W = B_HEADS * HEAD_DIM
B_Q_RANK = 256
B_KV_RANK = 128
IDX_HEADS = 8
IDX_DIM = 32
IDX_TOPK_MAX = 256
C_GROUPS = ((128, 1), (512, 4), (2048, 16))
C_HEADS_PER_GROUP = 4
C_HEADS = C_HEADS_PER_GROUP * len(C_GROUPS)
C_W = C_HEADS * HEAD_DIM
IN_WIDTHS = (A_W, A_W, A_W, A_HEADS, A_W,
             B_Q_RANK, B_KV_RANK, IDX_DIM, IDX_HEADS, B_W,
             C_W, C_W, C_W, C_W,
             D_MODEL, D_MODEL, D_MODEL)

LANES = 128
SUBLANES = 8
PACKED_ROWS = 16
SCALE = HEAD_DIM ** -0.5
LOG2E = float(np.log2(np.e))
ONES_ROWS = 16
NEG = -1e30
INT_MIN = -2 ** 31
INT16_MIN, INT16_MAX = -2 ** 15, 2 ** 15 - 1
CLASS_WINDOW = 128
assert all(w // d == CLASS_WINDOW for w, d in C_GROUPS)
ATT_T = 256
FOX_T = 512
POS_SPLIT = 64

MAIN_G = 0
MAIN_ZB = 3 * D_MODEL
MAIN_QA = MAIN_ZB + B_W
MAIN_KA = MAIN_QA + A_W
MAIN_VA = MAIN_KA + A_W
MAIN_ZA = MAIN_VA + A_W
MAIN_ZC = MAIN_ZA + A_W
MAIN_N = MAIN_ZC + C_W
SIDE_CQ = 0
SIDE_MISC = B_Q_RANK + B_KV_RANK
SIDE_WIDX = IDX_DIM
SIDE_FA = 512
SIDE_QC = SIDE_FA + LANES
SIDE_KC = SIDE_QC + C_W
SIDE_VC = SIDE_KC + C_W
SIDE_N = 3072

VMEM_LIMIT = 56 * 1024 * 1024


def _cparams(n_axes, vmem=VMEM_LIMIT):
    return pltpu.CompilerParams(dimension_semantics=("arbitrary",) * n_axes,
                                vmem_limit_bytes=vmem)


def _dot(a, b):
    return jnp.dot(a, b, preferred_element_type=F32)


def _dot_nt(a, b):
    return lax.dot_general(a, b, (((1,), (1,)), ((), ())), preferred_element_type=F32)


def _sigmoid(x):
    return 1.0 / (1.0 + jnp.exp(-x))


def _silu(x):
    return x * _sigmoid(x)


def _rms(x, g):
    return x * lax.rsqrt(jnp.mean(x * x, axis=-1, keepdims=True) + EPS) * g


def _split3(x):
    hi = x.astype(BF16)
    r1 = x - hi.astype(F32)
    mid = r1.astype(BF16)
    lo = (r1 - mid.astype(F32)).astype(BF16)
    return hi, mid, lo


def _fold8(x, op):
    n, t = x.shape
    return op(x.reshape(n // SUBLANES, SUBLANES, t), axis=0)


def _tree_fold(x, op, rows=SUBLANES):
    n, t = x.shape
    parts = [x[r:r + rows, :] for r in range(0, n, rows)]
    while len(parts) > 1:
        nxt = [op(parts[j], parts[j + 1]) for j in range(0, len(parts) - 1, 2)]
        if len(parts) % 2:
            nxt.append(parts[-1])
        parts = nxt
    return parts[0]


def _keys_max(x):
    return jnp.max(_fold8(x, jnp.max), axis=0, keepdims=True)


def _online_update_heads(states, s_ts, v_ts):
    ps, scaled = [], []
    for (m, acc), s_t in zip(states, s_ts):
        m_new = jnp.maximum(m, _keys_max(s_t))
        alpha = jnp.exp2(m - m_new)
        ps.append(jnp.exp2(s_t - m_new).astype(BF16))
        scaled.append((m_new, alpha * acc))
    return [(m, acc + _dot(v_t, p)) for (m, acc), v_t, p in zip(scaled, v_ts, ps)]


def _finish_pair(acc0, acc1):
    norm = lambda acc: acc[:HEAD_DIM, :] * (1.0 / acc[HEAD_DIM:HEAD_DIM + 1, :])
    return jnp.concatenate([norm(acc0), norm(acc1)], axis=0).T


def _rms_matmul_kernel(x_ref, g_ref, w_ref, o_ref, h_ref):
    @pl.when(pl.program_id(1) == 0)
    def _():
        h_ref[...] = _rms(x_ref[...], g_ref[...]).astype(BF16)

    o_ref[...] = _dot(h_ref[...], w_ref[...]).astype(o_ref.dtype)


def _rms_matmul(x2d, g, w, out_dtype, tm, tn):
    m, k = x2d.shape
    n = w.shape[1]
    return pl.pallas_call(
        _rms_matmul_kernel,
        grid=(m // tm, n // tn),
        in_specs=[pl.BlockSpec((tm, k), lambda i, j: (i, 0)),
                  pl.BlockSpec((1, k), lambda i, j: (0, 0)),
                  pl.BlockSpec((k, tn), lambda i, j: (0, j))],
        out_specs=pl.BlockSpec((tm, tn), lambda i, j: (i, j)),
        out_shape=jax.ShapeDtypeStruct((m, n), out_dtype),
        scratch_shapes=[pltpu.VMEM((tm, k), BF16)],
        compiler_params=_cparams(2),
        name="rms_in_proj",
    )(x2d, g, w)


def _cum_kernel(fa_ref, b_ref, tri_ref, sel_ref, one_ref, qf_ref, kf_ref, c3_s):
    seq = fa_ref.shape[1]
    tri = tri_ref[...]
    lane = lax.broadcasted_iota(I32, (1, LANES), 1)
    carry = jnp.zeros((1, LANES), F32)
    for blk in range(seq // LANES):
        rows = slice(blk * LANES, (blk + 1) * LANES)
        x = fa_ref[0, rows, :] + b_ref[...]
        lf = jnp.minimum(x, 0.0) - jnp.log(1.0 + jnp.exp(-jnp.abs(x)))
        hi, mid, lo = _split3(lf)
        c = _dot(tri, hi) + _dot(tri, mid) + _dot(tri, lo) + carry
        carry = c[LANES - 1:LANES, :]
        hi, mid, lo = _split3(jnp.where(lane < A_HEADS, c * LOG2E, 0.0))
        c3_s[rows, :] = (hi.astype(F32) + pltpu.roll(mid.astype(F32), A_HEADS, 1)
                         + pltpu.roll(lo.astype(F32), 2 * A_HEADS, 1)).astype(BF16)
    half = seq // 2
    for h in range(A_HEADS):
        for rows in (slice(0, half), slice(half, seq)):
            feats = _dot(c3_s[rows, :], sel_ref[h]) + one_ref[...]
            qf_ref[0, h, rows, :] = feats[:, :LANES].astype(BF16)
            kf_ref[0, h, rows, :] = feats[:, LANES:].astype(BF16)


def _cum_forget(side, b_pad, consts, batch, seq):
    tri, sel, one = consts
    full = lambda a: pl.BlockSpec(a.shape, lambda b: (0,) * a.ndim)
    feat = jax.ShapeDtypeStruct((batch, A_HEADS, seq, LANES), BF16)
    return pl.pallas_call(
        _cum_kernel,
        grid=(batch,),
        in_specs=[pl.BlockSpec((1, seq, LANES), lambda b: (b, 0, SIDE_FA // LANES)),
                  full(b_pad), full(tri), full(sel), full(one)],
        out_specs=[pl.BlockSpec((1, A_HEADS, seq, LANES), lambda b: (b, 0, 0, 0))] * 2,
        out_shape=[feat, feat],
        scratch_shapes=[pltpu.VMEM((seq, LANES), BF16)],
        compiler_params=_cparams(1),
        name="fox_cumsum",
    )(side, b_pad, tri, sel, one)


def _fox_kernel(q_ref, k_ref, v_ref, z_ref, qf_ref, kf_ref, o_ref, vt_s, qa_s, m_s, acc_s):
    seq = q_ref.shape[1]
    t = FOX_T
    n_heads = A_HEADS
    lane = lax.broadcasted_iota(I32, (1, LANES), 1)
    key_le_query = (lax.broadcasted_iota(I32, (t, t), 0) <= lax.broadcasted_iota(I32, (t, t), 1))
    pair_cols = lambda h: slice((h // 2) * LANES, (h // 2 + 1) * LANES)

    def transpose_v(c, _):
        cols = pl.ds(pl.multiple_of(c * t, t), t)
        for p in range(n_heads // 2):
            pc = slice(p * LANES, (p + 1) * LANES)
            v_t = v_ref[0, cols, pc].astype(F32).T.astype(BF16)
            for e in range(2):
                vt_s[2 * p + e, :HEAD_DIM, cols] = v_t[e * HEAD_DIM:(e + 1) * HEAD_DIM, :]
                vt_s[2 * p + e, HEAD_DIM:, cols] = jnp.ones((ONES_ROWS, t), BF16)
        return 0

    lax.fori_loop(0, seq // t, transpose_v, 0)

    def q_block(i, _):
        rows = pl.ds(pl.multiple_of(i * t, t), t)
        for h in range(n_heads):
            half = (lane >= HEAD_DIM) if h % 2 else (lane < HEAD_DIM)
            q2 = q_ref[0, rows, pair_cols(h)]
            qa_s[h] = jnp.concatenate([jnp.where(half, q2, jnp.zeros_like(q2)), qf_ref[0, h, rows, :]], axis=1)
            m_s[h] = jnp.full((1, t), NEG, F32)
            acc_s[h] = jnp.zeros((HEAD_DIM + ONES_ROWS, t), F32)

        def chunk(kb, diagonal):
            cols = pl.ds(pl.multiple_of(kb * t, t), t)
            states = [(m_s[h], acc_s[h]) for h in range(n_heads)]
            s_ts = [_dot_nt(jnp.concatenate([k_ref[0, cols, pair_cols(h)], kf_ref[0, h, cols, :]], axis=1),
                            qa_s[h]) for h in range(n_heads)]
            if diagonal:
                s_ts = [jnp.where(key_le_query, s_t, NEG) for s_t in s_ts]
            v_ts = [vt_s[h, :, cols] for h in range(n_heads)]
            for h, st in enumerate(_online_update_heads(states, s_ts, v_ts)):
                m_s[h], acc_s[h] = st

        def off_diagonal(kb, _):
            chunk(kb, False)
            return 0

        lax.fori_loop(0, i, off_diagonal, 0)
        chunk(i, True)
        for p in range(n_heads // 2):
            h0, h1 = 2 * p, 2 * p + 1
            out = _finish_pair(acc_s[h0], acc_s[h1]) * _silu(z_ref[0, rows, pair_cols(h0)].astype(F32))
            o_ref[0, rows, pair_cols(h0)] = out.astype(o_ref.dtype)
        return 0

    lax.fori_loop(0, seq // t, q_block, 0)


def _fox_attention(main, qf, kf, batch, seq):
    t = FOX_T
    blk = lambda base: pl.BlockSpec((1, seq, A_W), lambda b, base=base: (b, 0, base // A_W))
    feat = pl.BlockSpec((1, A_HEADS, seq, LANES), lambda b: (b, 0, 0, 0))
    return pl.pallas_call(
        _fox_kernel,
        grid=(batch,),
        in_specs=[blk(MAIN_QA), blk(MAIN_KA), blk(MAIN_VA), blk(MAIN_ZA), feat, feat],
        out_specs=pl.BlockSpec((1, seq, A_W), lambda b: (b, 0, 0)),
        out_shape=jax.ShapeDtypeStruct((batch, seq, A_W), BF16),
        scratch_shapes=[pltpu.VMEM((A_HEADS, HEAD_DIM + ONES_ROWS, seq), BF16),
                        pltpu.VMEM((A_HEADS, t, 2 * LANES), BF16),
                        pltpu.VMEM((A_HEADS, 1, t), F32),
                        pltpu.VMEM((A_HEADS, HEAD_DIM + ONES_ROWS, t), F32)],
        compiler_params=_cparams(1),
        name="fox_attention",
    )(main, main, main, main, qf, kf)


def _lane_group(lane, first):
    rel = lane - first
    return jnp.where((rel >= 0) & (rel < 9), rel // 3, -1)


def _key_pos_features(pos, lane, first):
    hi = lax.shift_right_logical(pos, 6).astype(F32)
    lo = (pos & (POS_SPLIT - 1)).astype(F32)
    grp = _lane_group(lane, first)
    return jnp.where(grp == 0, hi, jnp.where(grp == 1, lo, jnp.where(grp == 2, 1.0, 0.0)))


def _query_pos_features(pos, lane, first, slope):
    grp = _lane_group(lane, first)
    x = jnp.where(grp == 0, POS_SPLIT * slope,
                  jnp.where(grp == 1, slope, jnp.where(grp == 2, -slope * pos.astype(F32), 0.0)))
    hi, mid, lo = _split3(x)
    piece = (lane - first) % 3
    return jnp.where(piece == 0, hi, jnp.where(piece == 1, mid, lo))


def _dsa_kernel(side_ref, z_ref, gcq_ref, wuq_ref, wuqi_ref, gckv_ref, wkv_ref, wvk_ref, wvt_ref,
                tile_ref, tri_ref, qfeat_ref, o_ref,
                qb_s, qi_s, kt_s, ka_s, kb_s, vt_s, wt_s, qm_s, qa_s, key_s, hi_s, lo_s, mb_s, m_s, acc_s,
                *, top_k):
    seq = side_ref.shape[1]
    t = ATT_T
    n_heads = B_HEADS
    lane = lax.broadcasted_iota(I32, (1, LANES), 1)
    lane2 = lax.broadcasted_iota(I32, (1, 2 * LANES), 1)
    key_le_query = (lax.broadcasted_iota(I32, (t, t), 0) <= lax.broadcasted_iota(I32, (t, t), 1))
    row_iota = lax.broadcasted_iota(I32, (t, 1), 0)

    def prep(c, _):
        r0 = pl.multiple_of(c * t, t)
        rows = pl.ds(r0, t)
        blk = side_ref[0, rows, :]
        cqn = _rms(blk[:, SIDE_CQ:SIDE_CQ + B_Q_RANK], gcq_ref[...]).astype(BF16)
        qb_s[rows, :] = _dot(cqn, wuq_ref[...]).astype(BF16)
        qi_s[rows, :] = _dot(cqn, wuqi_ref[...]).astype(BF16)
        kvn = _rms(blk[:, B_Q_RANK:SIDE_MISC], gckv_ref[...]).astype(BF16)
        pos = r0 + row_iota
        ka_s[rows, :] = jnp.where(lane < HEAD_DIM, _dot(kvn, wkv_ref[...]),
                                  _key_pos_features(pos, lane, HEAD_DIM)).astype(BF16)
        kb_s[rows, :] = jnp.where(lane >= HEAD_DIM, _dot(kvn, wvk_ref[...]),
                                  _key_pos_features(pos, lane, 0)).astype(BF16)
        vt_s[:HEAD_DIM, rows] = _dot_nt(wvt_ref[...], kvn).astype(BF16)
        vt_s[HEAD_DIM:, rows] = jnp.ones((ONES_ROWS, t), BF16)
        misc = blk[:, SIDE_MISC:]
        kt_s[rows, :] = _dot(misc.astype(BF16), tile_ref[...]).astype(BF16)
        wt_s[:, rows] = misc.T
        return 0

    lax.fori_loop(0, seq // t, prep, 0)

    def q_block(i, _):
        r0 = pl.multiple_of(i * t, t)
        rows = pl.ds(r0, t)
        nk = i + 1

        qi = qi_s[rows, :]
        for h in range(IDX_HEADS):
            in_head = (lane2 >= h * IDX_DIM) & (lane2 < (h + 1) * IDX_DIM)
            qm_s[h] = jnp.where(in_head, qi, jnp.zeros_like(qi))
        w_rows = [wt_s[SIDE_WIDX + h:SIDE_WIDX + h + 1, rows] for h in range(IDX_HEADS)]

        def score_chunk(kc, diagonal):
            c0 = pl.multiple_of(kc * t, t)
            kt = kt_s[pl.ds(c0, t), :]
            acc = jnp.zeros((t, t), F32)
            for h in range(IDX_HEADS):
                acc = acc + w_rows[h] * jnp.maximum(_dot_nt(kt, qm_s[h]), 0.0)
            bits = lax.bitcast_convert_type(acc, I32)
            key = bits ^ (lax.shift_right_arithmetic(bits, 31) & 0x7FFFFFFF)
            key = jnp.where(key == -1, 0, key)
            if diagonal:
                key = jnp.where(key_le_query, key, INT_MIN)
            key_s[pl.ds(c0, t), :] = key
            hi_s[pl.ds(c0, t), :] = lax.shift_right_arithmetic(key, 16).astype(I16)

        def score_pair(kp, _):
            score_chunk(2 * kp, False)
            score_chunk(2 * kp + 1, False)
            return 0

        lax.fori_loop(0, i // 2, score_pair, 0)

        @pl.when(i % 2 == 1)
        def _():
            score_chunk(i - 1, False)

        score_chunk(i, True)

        def count(pred):
            def body(kc, acc):
                c0 = pl.multiple_of(kc * t, t)
                return acc + _tree_fold(jnp.where(pred(key_s[pl.ds(c0, t), :]), 1.0, 0.0), jnp.add)
            acc = lax.fori_loop(0, nk, body, jnp.zeros((SUBLANES, t), F32))
            return jnp.sum(acc, axis=0, keepdims=True)

        def count16(src, cand):
            def body(kc, acc):
                c0 = pl.multiple_of(kc * t, t)
                hit = jnp.where(src[pl.ds(c0, t), :] >= cand, jnp.ones((), BF16), jnp.zeros((), BF16))
                return acc + _tree_fold(hit, jnp.add, PACKED_ROWS)
            acc = lax.fori_loop(0, nk, body, jnp.zeros((PACKED_ROWS, t), BF16))
            return jnp.sum(acc.astype(F32), axis=0, keepdims=True)

        def search16(src):
            def step(p, thr):
                cand = thr + lax.shift_left(jnp.int32(1), 15 - p)
                return jnp.where(count16(src, cand.astype(I16)) >= top_k, cand, thr)
            return lax.fori_loop(0, 16, step, jnp.full((1, t), INT16_MIN, I32))

        t_hi = search16(hi_s)

        def low_body(kc, _):
            c0 = pl.multiple_of(kc * t, t)
            key = key_s[pl.ds(c0, t), :]
            hi = lax.shift_right_arithmetic(key, 16)
            lo = (key & 0xFFFF) + INT16_MIN
            lo_s[pl.ds(c0, t), :] = jnp.where(hi > t_hi, INT16_MAX, jnp.where(hi == t_hi, lo, INT16_MIN)).astype(I16)
            return 0

        lax.fori_loop(0, nk, low_body, 0)
        t_lo = search16(lo_s)
        thr = lax.shift_left(t_hi, 16) | ((t_lo - INT16_MIN) & 0xFFFF)
        thr = jnp.maximum(thr, INT_MIN + 1)

        def mask_body(kc, _):
            c0 = pl.multiple_of(kc * t, t)
            mb_s[pl.ds(c0, t), :] = jnp.where(key_s[pl.ds(c0, t), :] >= thr, 0.0, NEG)
            return 0

        lax.fori_loop(0, nk, mask_body, 0)

        n_ge = count(lambda k: k >= thr)

        @pl.when(jnp.max(n_ge) > top_k)
        def _():
            room = top_k - count(lambda k: k > thr)

            def tie_body(kc, seen):
                c0 = pl.multiple_of(kc * t, t)
                tie = key_s[pl.ds(c0, t), :] == thr
                rank = seen + _dot(tri_ref[...], jnp.where(tie, 1.0, 0.0).astype(BF16))
                drop = tie & (rank > room)
                mb_s[pl.ds(c0, t), :] = jnp.where(drop, NEG, mb_s[pl.ds(c0, t), :])
                return rank[t - 1:t, :]

            lax.fori_loop(0, nk, tie_body, jnp.zeros((1, t), F32))

        for h in range(n_heads):
            half = (lane >= HEAD_DIM) if h % 2 else (lane < HEAD_DIM)
            q2 = qb_s[rows, (h // 2) * LANES:(h // 2 + 1) * LANES]
            qa_s[h] = jnp.where(half, q2, qfeat_ref[h, rows, :])
            m_s[h] = jnp.full((1, t), NEG, F32)
            acc_s[h] = jnp.zeros((HEAD_DIM + ONES_ROWS, t), F32)

        def attend(kc, _):
            c0 = pl.multiple_of(kc * t, t)
            cols = pl.ds(c0, t)
            states = [(m_s[h], acc_s[h]) for h in range(n_heads)]
            mask = mb_s[cols, :]
            k_ops = (ka_s[cols, :], kb_s[cols, :])
            s_ts = [_dot_nt(k_ops[h % 2], qa_s[h]) + mask for h in range(n_heads)]
            v_list = [vt_s[:, cols]] * n_heads
            for h, st in enumerate(_online_update_heads(states, s_ts, v_list)):
                m_s[h], acc_s[h] = st
            return 0

        def attend_pair(kp, _):
            attend(2 * kp, 0)
            attend(2 * kp + 1, 0)
            return 0

        lax.fori_loop(0, nk // 2, attend_pair, 0)

        @pl.when(nk % 2 == 1)
        def _():
            attend(nk - 1, 0)

        for hp in range(n_heads // 2):
            cols = slice(hp * LANES, (hp + 1) * LANES)
            out = _finish_pair(acc_s[2 * hp], acc_s[2 * hp + 1]) * _silu(z_ref[0, rows, cols].astype(F32))
            o_ref[0, rows, cols] = out.astype(o_ref.dtype)
        return 0

    lax.fori_loop(0, seq // t, q_block, 0)


def _dsa_attention(side, main, w, tile, tri, batch, seq):
    assert seq // POS_SPLIT <= 256
    assert seq // PACKED_ROWS <= 256
    top_k = min(IDX_TOPK_MAX, seq // 4)
    slopes = tuple(float(2.0 ** (-8.0 * (h + 1) / B_HEADS)) * LOG2E for h in range(B_HEADS))
    t = ATT_T
    const = lambda a: pl.BlockSpec(a.shape, lambda b: (0,) * a.ndim)
    pos = jnp.arange(seq, dtype=I32).reshape(seq, 1)
    lane = jnp.arange(LANES, dtype=I32).reshape(1, LANES)
    qfeat = jnp.stack([_query_pos_features(pos, lane, 0 if h % 2 else HEAD_DIM, slopes[h])
                       for h in range(B_HEADS)]).astype(BF16)
    weights = [w["g_cq"], w["w_uq"], w["w_uq_idx"], w["g_ckv"], w["w_kv"], w["w_vk"], w["w_v_t"], tile, tri, qfeat]
    return pl.pallas_call(
        functools.partial(_dsa_kernel, top_k=top_k),
        grid=(batch,),
        in_specs=[pl.BlockSpec((1, seq, 512), lambda b: (b, 0, 0)),
                  pl.BlockSpec((1, seq, B_W), lambda b: (b, 0, MAIN_ZB // B_W))]
                 + [const(a) for a in weights],
        out_specs=pl.BlockSpec((1, seq, B_W), lambda b: (b, 0, 0)),
        out_shape=jax.ShapeDtypeStruct((batch, seq, B_W), BF16),
        scratch_shapes=[pltpu.VMEM((seq, B_W), BF16),
                        pltpu.VMEM((seq, IDX_HEADS * IDX_DIM), BF16),
                        pltpu.VMEM((seq, IDX_HEADS * IDX_DIM), BF16),
                        pltpu.VMEM((seq, LANES), BF16),
                        pltpu.VMEM((seq, LANES), BF16),
                        pltpu.VMEM((HEAD_DIM + ONES_ROWS, seq), BF16),
                        pltpu.VMEM((LANES, seq), F32),
                        pltpu.VMEM((IDX_HEADS, t, IDX_HEADS * IDX_DIM), BF16),
                        pltpu.VMEM((B_HEADS, t, LANES), BF16),
                        pltpu.VMEM((seq, t), I32),
                        pltpu.VMEM((seq, t), I16),
                        pltpu.VMEM((seq, t), I16),
                        pltpu.VMEM((seq, t), F32),
                        pltpu.VMEM((B_HEADS, 1, t), F32),
                        pltpu.VMEM((B_HEADS, HEAD_DIM + ONES_ROWS, t), F32)],
        compiler_params=_cparams(1),
        name="dsa_attention",
    )(side, main, *weights)


C_UNROLL = 8


def _dilated_kernel(q_ref, k_ref, v_ref, bias_ref, z0_ref, z1_ref, z2_ref, o_ref, o_s, l_s):
    seq = q_ref.shape[1]
    sp = pl.program_id(1)
    grp = pl.program_id(2)
    lane = lax.broadcasted_iota(I32, (1, LANES), 1)
    cw = CLASS_WINDOW

    def rows(start, d):
        return pl.ds(start, cw) if d == 1 else pl.ds(start, cw, stride=d)

    def group_body(g, d):
        per_class = seq // (d * cw)

        ld = lambda ref, s0: ref[0, rows(s0, d), :].astype(BF16)

        def load_block(n, u, before):
            r = n // per_class
            ib = n % per_class
            start = r + ib * (cw * d)
            b = dict(start=start, q=ld(q_ref, start), kc=ld(k_ref, start), vc=ld(v_ref, start), prev=None)
            if per_class <= C_UNROLL:
                if u % per_class:
                    b.update(prev="static", kp=before["kc"], vp=before["vc"])
            elif u:
                b.update(prev="static", kp=before["kc"], vp=before["vc"])
            else:
                pstart = r + jnp.maximum(ib - 1, 0) * (cw * d)
                b.update(prev="dynamic", kp=ld(k_ref, pstart), vp=ld(v_ref, pstart), has_prev=ib > 0)
            return b

        def blk(it, _):
            blocks = []
            for u in range(C_UNROLL):
                blocks.append(load_block(it * C_UNROLL + u, u, blocks[-1] if blocks else None))
            logits = []
            for b in blocks:
                for e in range(2):
                    half = (lane >= HEAD_DIM) if e else (lane < HEAD_DIM)
                    qm = jnp.where(half, b["q"], jnp.zeros_like(b["q"]))
                    s_cur = _dot_nt(qm, b["kc"]) + bias_ref[0, e, :, cw:]
                    s_prev = None
                    if b["prev"] == "static":
                        s_prev = _dot_nt(qm, b["kp"]) + bias_ref[0, e, :, :cw]
                    elif b["prev"] == "dynamic":
                        s_prev = _dot_nt(qm, b["kp"]) + jnp.where(b["has_prev"], bias_ref[0, e, :, :cw], NEG)
                    logits.append((s_cur, s_prev))
            probs = []
            for s_cur, s_prev in logits:
                if s_prev is None:
                    m = jnp.max(s_cur, axis=-1, keepdims=True)
                    p_cur, p_prev = jnp.exp(s_cur - m), None
                    l = jnp.sum(p_cur, axis=-1, keepdims=True)
                else:
                    m = jnp.max(jnp.maximum(s_cur, s_prev), axis=-1, keepdims=True)
                    p_cur, p_prev = jnp.exp(s_cur - m), jnp.exp(s_prev - m)
                    l = jnp.sum(p_cur + p_prev, axis=-1, keepdims=True)
                    p_prev = p_prev.astype(BF16)
                probs.append((p_cur.astype(BF16), p_prev, 1.0 / l, m + jnp.log(l)))
            for u, b in enumerate(blocks):
                outs = []
                for e in range(2):
                    p_cur, p_prev, inv_l, _ = probs[2 * u + e]
                    o = _dot(p_cur, b["vc"])
                    if p_prev is not None:
                        o = o + _dot(p_prev, b["vp"])
                    outs.append(o * inv_l)
                o_s[g, rows(b["start"], d), :] = jnp.where(lane < HEAD_DIM, outs[0], outs[1])
                l_s[g, rows(b["start"], d), :] = jnp.where(lane < HEAD_DIM, probs[2 * u][3], probs[2 * u + 1][3])
            return 0

        lax.fori_loop(0, seq // (cw * C_UNROLL), blk, 0)

    for g, (_, d) in enumerate(C_GROUPS):
        pl.when(grp == g)(functools.partial(group_body, g, d))

    n_grp = len(C_GROUPS)
    z_refs = (z0_ref, z1_ref, z2_ref)

    def combine(sp_static):
        def chunk(c, _):
            r0 = pl.multiple_of(c * 256, 256)
            ls = [l_s[g, pl.ds(r0, 256), :] for g in range(n_grp)]
            m = functools.reduce(jnp.maximum, ls)
            ws = [jnp.exp(x - m) for x in ls]
            inv = 1.0 / functools.reduce(lambda a, b: a + b, ws)
            for g in range(n_grp):
                col = (g * 2 + sp_static) * LANES
                y = (ws[g] * inv) * o_s[g, pl.ds(r0, 256), :]
                y = y * _silu(z_refs[g][0, pl.ds(r0, 256), :].astype(F32))
                o_ref[0, pl.ds(r0, 256), col:col + LANES] = y.astype(o_ref.dtype)
            return 0

        lax.fori_loop(0, seq // 256, chunk, 0)

    for s in range(2):
        pl.when((grp == n_grp - 1) & (sp == s))(functools.partial(combine, s))


def _dilated_attention(side, main, bias, batch, seq):
    assert seq % (C_GROUPS[-1][1] * CLASS_WINDOW) == 0
    assert (seq // CLASS_WINDOW) % C_UNROLL == 0
    n_grp = len(C_GROUPS)
    blk = lambda base: pl.BlockSpec(
        (1, seq, LANES), lambda b, s, g, base=base: (b, 0, base // LANES + 2 * g + s))
    zblk = lambda g: pl.BlockSpec(
        (1, seq, LANES), lambda b, s, _, g=g: (b, 0, MAIN_ZC // LANES + 2 * g + s))
    return pl.pallas_call(
        _dilated_kernel,
        grid=(batch, 2, n_grp),
        in_specs=[blk(SIDE_QC), blk(SIDE_KC), blk(SIDE_VC),
                  pl.BlockSpec((1, 2, CLASS_WINDOW, 2 * CLASS_WINDOW), lambda b, s, g: (2 * g + s, 0, 0, 0)),
                  zblk(0), zblk(1), zblk(2)],
        out_specs=pl.BlockSpec((1, seq, C_W), lambda b, s, g: (b, 0, 0)),
        out_shape=jax.ShapeDtypeStruct((batch, seq, C_W), BF16),
        scratch_shapes=[pltpu.VMEM((n_grp, seq, LANES), F32),
                        pltpu.VMEM((n_grp, seq, LANES), F32)],
        compiler_params=_cparams(3),
        name="dilated_attention",
    )(side, side, side, bias, main, main, main)


def _dilated_bias():
    cw = CLASS_WINDOW
    slopes = np.exp2(-8.0 * (np.arange(C_HEADS, dtype=np.float32) + 1.0) / C_HEADS).astype(np.float32)
    i = np.arange(cw)[:, None]
    j = np.arange(cw)[None, :]
    out = np.empty((C_HEADS // 2, 2, cw, 2 * cw), np.float32)
    for h in range(C_HEADS):
        d = C_GROUPS[h // C_HEADS_PER_GROUP][1]
        prev = np.where(j >= i, -slopes[h] * np.float32(d) * (cw + i - j).astype(np.float32), NEG)
        cur = np.where(j <= i, -slopes[h] * np.float32(d) * (i - j).astype(np.float32), NEG)
        out[h // 2, h % 2] = np.concatenate([prev, cur], axis=1)
    return jnp.asarray(out)


def _merge_kernel(x_ref, ya_ref, yb_ref, yc_ref, g_ref, wa_ref, wb_ref, wc_ref, wo_ref, fg_ref, o_ref,
                  *, final):
    gate = lambda n: _sigmoid(g_ref[:, n * D_MODEL:(n + 1) * D_MODEL].astype(F32))
    merged = (gate(0) * _dot(ya_ref[...], wa_ref[...])
              + gate(1) * _dot(yb_ref[...], wb_ref[...])
              + gate(2) * _dot(yc_ref[...], wc_ref[...]))
    y = x_ref[...] + _dot(merged.astype(BF16), wo_ref[...])
    if final:
        y = _rms(y, fg_ref[...])
    o_ref[...] = y


def _merge(x2d, ya, yb, yc, main, w_a, w_b, w_c, w_o, final_g, final, tm=512):
    m = x2d.shape[0]
    row = lambda w: pl.BlockSpec((tm, w), lambda i: (i, 0))
    const = lambda a: pl.BlockSpec(a.shape, lambda i: (0, 0))
    return pl.pallas_call(
        functools.partial(_merge_kernel, final=final),
        grid=(m // tm,),
        in_specs=[row(D_MODEL), row(A_W), row(B_W), row(C_W), row(3 * D_MODEL),
                  const(w_a), const(w_b), const(w_c), const(w_o), const(final_g)],
        out_specs=row(D_MODEL),
        out_shape=jax.ShapeDtypeStruct((m, D_MODEL), F32),
        compiler_params=_cparams(1),
        name="gated_merge",
    )(x2d, ya, yb, yc, main, w_a, w_b, w_c, w_o, final_g)


def _layer_weights(norm_g, w_in, b_forget, g_cq, w_uq, w_uq_idx, g_ckv, w_ukv, w_a, w_b, w_c, w_o):
    points = np.cumsum(IN_WIDTHS)[:-1].tolist()
    (wqa, wka, wva, wfa, wza, wcq, wckv, wkidx, wwidx, wzb,
     wqc, wkc, wvc, wzc, wga, wgb, wgc) = jnp.split(w_in, points, axis=1)
    zeros = lambda n: jnp.zeros((D_MODEL, n), F32)
    w_main = jnp.concatenate([wga, wgb, wgc, wzb, wqa * (SCALE * LOG2E), wka, wva, wza, wzc], axis=1)
    w_side = jnp.concatenate(
        [wcq, wckv, wkidx, wwidx, zeros(LANES - IDX_DIM - IDX_HEADS),
         wfa, zeros(LANES - A_HEADS), wqc * SCALE, wkc, wvc, zeros(SIDE_N - SIDE_VC - C_W)], axis=1)
    assert w_main.shape[1] == MAIN_N and w_side.shape[1] == SIDE_N
    wk, wv = w_ukv[:, :HEAD_DIM], w_ukv[:, HEAD_DIM:]
    w_kv = jnp.concatenate([wk, wv], axis=1).astype(BF16)
    w_vk = jnp.concatenate([wv, wk], axis=1).astype(BF16)
    return dict(
        norm_g=norm_g.reshape(1, D_MODEL),
        w_main=w_main.astype(BF16), w_side=w_side.astype(BF16),
        b_pad=jnp.zeros((1, LANES), F32).at[0, :A_HEADS].set(b_forget),
        g_cq=g_cq.reshape(1, B_Q_RANK), w_uq=(w_uq * (SCALE * LOG2E)).astype(BF16), w_uq_idx=w_uq_idx.astype(BF16),
        g_ckv=g_ckv.reshape(1, B_KV_RANK), w_kv=w_kv, w_vk=w_vk, w_v_t=wv.T.astype(BF16),
        w_a=w_a.astype(BF16), w_b=w_b.astype(BF16), w_c=w_c.astype(BF16), w_o=w_o.astype(BF16))


def _constants():
    r = np.arange(LANES)
    tri128 = (r[None, :] <= r[:, None]).astype(np.float32)
    r2 = np.arange(ATT_T)
    tri_t = (r2[None, :] <= r2[:, None]).astype(np.float32)
    c = np.arange(IDX_HEADS * IDX_DIM)
    tile = ((r[:, None] == c[None, :] % IDX_DIM) & (r[:, None] < IDX_DIM)).astype(np.float32)
    sel = np.zeros((A_HEADS, LANES, 2 * LANES), np.float32)
    for h in range(A_HEADS):
        for piece in range(3):
            sel[h, piece * A_HEADS + h, piece] = 1.0
            sel[h, piece * A_HEADS + h, LANES + 3 + piece] = -1.0
    one = np.zeros((1, 2 * LANES), np.float32)
    one[0, 3:6] = 1.0
    one[0, LANES:LANES + 3] = 1.0
    bf = lambda a: jnp.asarray(a, BF16)
    cum = (bf(tri128), bf(sel), jnp.asarray(one))
    return cum, bf(tri_t), bf(tile), _dilated_bias()


def _hybrid_layer(x2d, batch, seq, w, consts, final_g, final):
    cum_consts, tri_t, tile, bias_c = consts
    main = _rms_matmul(x2d, w["norm_g"], w["w_main"], BF16, tm=2048, tn=1280)
    side = _rms_matmul(x2d, w["norm_g"], w["w_side"], F32, tm=1024, tn=SIDE_N)
    main3 = main.reshape(batch, seq, MAIN_N)
    side3 = side.reshape(batch, seq, SIDE_N)
    qf, kf = _cum_forget(side3, w["b_pad"], cum_consts, batch, seq)
    ya = _fox_attention(main3, qf, kf, batch, seq)
    yb = _dsa_attention(side3, main3, w, tile, tri_t, batch, seq)
    yc = _dilated_attention(side3, main3, bias_c, batch, seq)
    m = batch * seq
    return _merge(x2d, ya.reshape(m, A_W), yb.reshape(m, B_W), yc.reshape(m, C_W), main,
                  w["w_a"], w["w_b"], w["w_c"], w["w_o"], final_g, final)


def kernel(x, norm_g, w_in, b_forget, g_cq, w_uq, w_uq_idx, g_ckv, w_ukv, w_a, w_b, w_c, w_o, final_g):
    batch, seq, d_model = x.shape
    assert d_model == D_MODEL and seq % ATT_T == 0 and seq % FOX_T == 0
    depth = norm_g.shape[0]
    consts = _constants()
    fg = final_g.reshape(1, D_MODEL)
    x2d = x.reshape(batch * seq, D_MODEL)
    for l in range(depth):
        w = _layer_weights(norm_g[l], w_in[l], b_forget[l], g_cq[l], w_uq[l], w_uq_idx[l], g_ckv[l],
                           w_ukv[l], w_a[l], w_b[l], w_c[l], w_o[l])
        x2d = _hybrid_layer(x2d, batch, seq, w, consts, fg, final=(l == depth - 1))
    return x2d.reshape(batch, seq, D_MODEL)
```

```python
import functools

import numpy as np
import jax
import jax.numpy as jnp
from jax import lax
from jax.experimental import pallas as pl
from jax.experimental.pallas import tpu as pltpu

F32 = jnp.float32
BF16 = jnp.bfloat16
I32 = jnp.int32
I16 = jnp.int16

D_MODEL = 1024
HEAD_DIM = 64
EPS = 1e-6
A_HEADS = 8
A_W = A_HEADS * HEAD_DIM
B_HEADS = 8
B_W = B_HEADS * HEAD_DIM
B_Q_RANK = 256
B_KV_RANK = 128
IDX_HEADS = 8
IDX_DIM = 32
IDX_TOPK_MAX = 256
C_GROUPS = ((128, 1), (512, 4), (2048, 16))
C_HEADS_PER_GROUP = 4
C_HEADS = C_HEADS_PER_GROUP * len(C_GROUPS)
C_W = C_HEADS * HEAD_DIM
IN_WIDTHS = (A_W, A_W, A_W, A_HEADS, A_W,
             B_Q_RANK, B_KV_RANK, IDX_DIM, IDX_HEADS, B_W,
             C_W, C_W, C_W, C_W,
             D_MODEL, D_MODEL, D_MODEL)

LANES = 128
SUBLANES = 8
PACKED_ROWS = 16
SCALE = HEAD_DIM ** -0.5
LOG2E = float(np.log2(np.e))
ONES_ROWS = 16
NEG = -1e30
INT_MIN = -2 ** 31
INT16_MIN, INT16_MAX = -2 ** 15, 2 ** 15 - 1
CLASS_WINDOW = 128
assert all(w // d == CLASS_WINDOW for w, d in C_GROUPS)
ATT_T = 256
FOX_T = 512
POS_SPLIT = 64

MAIN_G = 0
MAIN_ZB = 3 * D_MODEL
MAIN_QA = MAIN_ZB + B_W
MAIN_KA = MAIN_QA + A_W
MAIN_VA = MAIN_KA + A_W
MAIN_ZA = MAIN_VA + A_W
MAIN_ZC = MAIN_ZA + A_W
MAIN_N = MAIN_ZC + C_W
SIDE_CQ = 0
SIDE_MISC = B_Q_RANK + B_KV_RANK
SIDE_WIDX = IDX_DIM
SIDE_FA = 512
SIDE_QC = SIDE_FA + LANES
SIDE_KC = SIDE_QC + C_W
SIDE_VC = SIDE_KC + C_W
SIDE_N = 3072

VMEM_LIMIT = 56 * 1024 * 1024


def _cparams(n_axes, vmem=VMEM_LIMIT):
    return pltpu.CompilerParams(dimension_semantics=("arbitrary",) * n_axes,
                                vmem_limit_bytes=vmem)


def _dot(a, b):
    return jnp.dot(a, b, preferred_element_type=F32)


def _dot_nt(a, b):
    return lax.dot_general(a, b, (((1,), (1,)), ((), ())), preferred_element_type=F32)


def _sigmoid(x):
    return 1.0 / (1.0 + jnp.exp(-x))


def _silu(x):
    return x * _sigmoid(x)


def _rms(x, g):
    return x * lax.rsqrt(jnp.mean(x * x, axis=-1, keepdims=True) + EPS) * g


def _split3(x):
    hi = x.astype(BF16)
    r1 = x - hi.astype(F32)
    mid = r1.astype(BF16)
    lo = (r1 - mid.astype(F32)).astype(BF16)
    return hi, mid, lo


def _fold8(x, op):
    n, t = x.shape
    return op(x.reshape(n // SUBLANES, SUBLANES, t), axis=0)


def _tree_fold(x, op, rows=SUBLANES):
    n, t = x.shape
    parts = [x[r:r + rows, :] for r in range(0, n, rows)]
    while len(parts) > 1:
        nxt = [op(parts[j], parts[j + 1]) for j in range(0, len(parts) - 1, 2)]
        if len(parts) % 2:
            nxt.append(parts[-1])
        parts = nxt
    return parts[0]


def _keys_max(x):
    return jnp.max(_fold8(x, jnp.max), axis=0, keepdims=True)


def _online_update_heads(states, s_ts, v_ts):
    ps, scaled = [], []
    for (m, acc), s_t in zip(states, s_ts):
        m_new = jnp.maximum(m, _keys_max(s_t))
        alpha = jnp.exp2(m - m_new)
        ps.append(jnp.exp2(s_t - m_new).astype(BF16))
        scaled.append((m_new, alpha * acc))
    return [(m, acc + _dot(v_t, p)) for (m, acc), v_t, p in zip(scaled, v_ts, ps)]


def _finish_pair(acc0, acc1):
    norm = lambda acc: acc[:HEAD_DIM, :] * (1.0 / acc[HEAD_DIM:HEAD_DIM + 1, :])
    return jnp.concatenate([norm(acc0), norm(acc1)], axis=0).T


def _rms_matmul_kernel(x_ref, g_ref, w_ref, o_ref, h_ref):
    @pl.when(pl.program_id(1) == 0)
    def _():
        h_ref[...] = _rms(x_ref[...], g_ref[...]).astype(BF16)

    o_ref[...] = _dot(h_ref[...], w_ref[...]).astype(o_ref.dtype)


def _rms_matmul(x2d, g, w, out_dtype, tm, tn):
    m, k = x2d.shape
    n = w.shape[1]
    return pl.pallas_call(
        _rms_matmul_kernel,
        grid=(m // tm, n // tn),
        in_specs=[pl.BlockSpec((tm, k), lambda i, j: (i, 0)),
                  pl.BlockSpec((1, k), lambda i, j: (0, 0)),
                  pl.BlockSpec((k, tn), lambda i, j: (0, j))],
        out_specs=pl.BlockSpec((tm, tn), lambda i, j: (i, j)),
        out_shape=jax.ShapeDtypeStruct((m, n), out_dtype),
        scratch_shapes=[pltpu.VMEM((tm, k), BF16)],
        compiler_params=_cparams(2),
        name="rms_in_proj",
    )(x2d, g, w)


def _cum_kernel(fa_ref, b_ref, tri_ref, sel_ref, one_ref, qf_ref, kf_ref, c3_s):
    seq = fa_ref.shape[1]
    tri = tri_ref[...]
    lane = lax.broadcasted_iota(I32, (1, LANES), 1)
    carry = jnp.zeros((1, LANES), F32)
    for blk in range(seq // LANES):
        rows = slice(blk * LANES, (blk + 1) * LANES)
        x = fa_ref[0, rows, :] + b_ref[...]
        lf = jnp.minimum(x, 0.0) - jnp.log(1.0 + jnp.exp(-jnp.abs(x)))
        hi, mid, lo = _split3(lf)
        c = _dot(tri, hi) + _dot(tri, mid) + _dot(tri, lo) + carry
        carry = c[LANES - 1:LANES, :]
        hi, mid, lo = _split3(jnp.where(lane < A_HEADS, c * LOG2E, 0.0))
        c3_s[rows, :] = (hi.astype(F32) + pltpu.roll(mid.astype(F32), A_HEADS, 1)
                         + pltpu.roll(lo.astype(F32), 2 * A_HEADS, 1)).astype(BF16)
    half = seq // 2
    for h in range(A_HEADS):
        for rows in (slice(0, half), slice(half, seq)):
            feats = _dot(c3_s[rows, :], sel_ref[h]) + one_ref[...]
            qf_ref[0, h, rows, :] = feats[:, :LANES].astype(BF16)
            kf_ref[0, h, rows, :] = feats[:, LANES:].astype(BF16)


def _cum_forget(side, b_pad, consts, batch, seq):
    tri, sel, one = consts
    full = lambda a: pl.BlockSpec(a.shape, lambda b: (0,) * a.ndim)
    feat = jax.ShapeDtypeStruct((batch, A_HEADS, seq, LANES), BF16)
    return pl.pallas_call(
        _cum_kernel,
        grid=(batch,),
        in_specs=[pl.BlockSpec((1, seq, LANES), lambda b: (b, 0, SIDE_FA // LANES)),
                  full(b_pad), full(tri), full(sel), full(one)],
        out_specs=[pl.BlockSpec((1, A_HEADS, seq, LANES), lambda b: (b, 0, 0, 0))] * 2,
        out_shape=[feat, feat],
        scratch_shapes=[pltpu.VMEM((seq, LANES), BF16)],
        compiler_params=_cparams(1),
        name="fox_cumsum",
    )(side, b_pad, tri, sel, one)


def _fox_kernel(q_ref, k_ref, v_ref, z_ref, qf_ref, kf_ref, o_ref, vt_s, qa_s, m_s, acc_s):
    seq = q_ref.shape[1]
    t = FOX_T
    n_heads = A_HEADS
    lane = lax.broadcasted_iota(I32, (1, LANES), 1)
    key_le_query = (lax.broadcasted_iota(I32, (t, t), 0) <= lax.broadcasted_iota(I32, (t, t), 1))
    pair_cols = lambda h: slice((h // 2) * LANES, (h // 2 + 1) * LANES)

    def transpose_v(c, _):
        cols = pl.ds(pl.multiple_of(c * t, t), t)
        for p in range(n_heads // 2):
            pc = slice(p * LANES, (p + 1) * LANES)
            v_t = v_ref[0, cols, pc].astype(F32).T.astype(BF16)
            for e in range(2):
                vt_s[2 * p + e, :HEAD_DIM, cols] = v_t[e * HEAD_DIM:(e + 1) * HEAD_DIM, :]
                vt_s[2 * p + e, HEAD_DIM:, cols] = jnp.ones((ONES_ROWS, t), BF16)
        return 0

    lax.fori_loop(0, seq // t, transpose_v, 0)

    def q_block(i, _):
        rows = pl.ds(pl.multiple_of(i * t, t), t)
        for h in range(n_heads):
            half = (lane >= HEAD_DIM) if h % 2 else (lane < HEAD_DIM)
            q2 = q_ref[0, rows, pair_cols(h)]
            qa_s[h] = jnp.concatenate([jnp.where(half, q2, jnp.zeros_like(q2)), qf_ref[0, h, rows, :]], axis=1)
            m_s[h] = jnp.full((1, t), NEG, F32)
            acc_s[h] = jnp.zeros((HEAD_DIM + ONES_ROWS, t), F32)

        def chunk(kb, diagonal):
            cols = pl.ds(pl.multiple_of(kb * t, t), t)
            states = [(m_s[h], acc_s[h]) for h in range(n_heads)]
            s_ts = [_dot_nt(jnp.concatenate([k_ref[0, cols, pair_cols(h)], kf_ref[0, h, cols, :]], axis=1),
                            qa_s[h]) for h in range(n_heads)]
            if diagonal:
                s_ts = [jnp.where(key_le_query, s_t, NEG) for s_t in s_ts]
            v_ts = [vt_s[h, :, cols] for h in range(n_heads)]
            for h, st in enumerate(_online_update_heads(states, s_ts, v_ts)):
                m_s[h], acc_s[h] = st

        def off_diagonal(kb, _):
            chunk(kb, False)
            return 0

        lax.fori_loop(0, i, off_diagonal, 0)
        chunk(i, True)
        for p in range(n_heads // 2):
            h0, h1 = 2 * p, 2 * p + 1
            out = _finish_pair(acc_s[h0], acc_s[h1]) * _silu(z_ref[0, rows, pair_cols(h0)].astype(F32))
            o_ref[0, rows, pair_cols(h0)] = out.astype(o_ref.dtype)
        return 0

    lax.fori_loop(0, seq // t, q_block, 0)


def _fox_attention(main, qf, kf, batch, seq):
    t = FOX_T
    blk = lambda base: pl.BlockSpec((1, seq, A_W), lambda b, base=base: (b, 0, base // A_W))
    feat = pl.BlockSpec((1, A_HEADS, seq, LANES), lambda b: (b, 0, 0, 0))
    return pl.pallas_call(
        _fox_kernel,
        grid=(batch,),
        in_specs=[blk(MAIN_QA), blk(MAIN_KA), blk(MAIN_VA), blk(MAIN_ZA), feat, feat],
        out_specs=pl.BlockSpec((1, seq, A_W), lambda b: (b, 0, 0)),
        out_shape=jax.ShapeDtypeStruct((batch, seq, A_W), BF16),
        scratch_shapes=[pltpu.VMEM((A_HEADS, HEAD_DIM + ONES_ROWS, seq), BF16),
                        pltpu.VMEM((A_HEADS, t, 2 * LANES), BF16),
                        pltpu.VMEM((A_HEADS, 1, t), F32),
                        pltpu.VMEM((A_HEADS, HEAD_DIM + ONES_ROWS, t), F32)],
        compiler_params=_cparams(1),
        name="fox_attention",
    )(main, main, main, main, qf, kf)


def _lane_group(lane, first):
    rel = lane - first
    return jnp.where((rel >= 0) & (rel < 9), rel // 3, -1)


def _key_pos_features(pos, lane, first):
    hi = lax.shift_right_logical(pos, 6).astype(F32)
    lo = (pos & (POS_SPLIT - 1)).astype(F32)
    grp = _lane_group(lane, first)
    return jnp.where(grp == 0, hi, jnp.where(grp == 1, lo, jnp.where(grp == 2, 1.0, 0.0)))


def _query_pos_features(pos, lane, first, slope):
    grp = _lane_group(lane, first)
    x = jnp.where(grp == 0, POS_SPLIT * slope,
                  jnp.where(grp == 1, slope, jnp.where(grp == 2, -slope * pos.astype(F32), 0.0)))
    hi, mid, lo = _split3(x)
    piece = (lane - first) % 3
    return jnp.where(piece == 0, hi, jnp.where(piece == 1, mid, lo))


def _dsa_kernel(side_ref, z_ref, gcq_ref, wuq_ref, wuqi_ref, gckv_ref, wkv_ref, wvk_ref, wvt_ref,
                tile_ref, tri_ref, qfeat_ref, o_ref,
                qb_s, qi_s, kt_s, ka_s, kb_s, vt_s, wt_s, qm_s, qa_s, key_s, hi_s, lo_s, mb_s, m_s, acc_s,
                *, top_k):
    seq = side_ref.shape[1]
    t = ATT_T
    n_heads = B_HEADS
    lane = lax.broadcasted_iota(I32, (1, LANES), 1)
    lane2 = lax.broadcasted_iota(I32, (1, 2 * LANES), 1)
    key_le_query = (lax.broadcasted_iota(I32, (t, t), 0) <= lax.broadcasted_iota(I32, (t, t), 1))
    row_iota = lax.broadcasted_iota(I32, (t, 1), 0)

    def prep(c, _):
        r0 = pl.multiple_of(c * t, t)
        rows = pl.ds(r0, t)
        blk = side_ref[0, rows, :]
        cqn = _rms(blk[:, SIDE_CQ:SIDE_CQ + B_Q_RANK], gcq_ref[...]).astype(BF16)
        qb_s[rows, :] = _dot(cqn, wuq_ref[...]).astype(BF16)
        qi_s[rows, :] = _dot(cqn, wuqi_ref[...]).astype(BF16)
        kvn = _rms(blk[:, B_Q_RANK:SIDE_MISC], gckv_ref[...]).astype(BF16)
        pos = r0 + row_iota
        ka_s[rows, :] = jnp.where(lane < HEAD_DIM, _dot(kvn, wkv_ref[...]),
                                  _key_pos_features(pos, lane, HEAD_DIM)).astype(BF16)
        kb_s[rows, :] = jnp.where(lane >= HEAD_DIM, _dot(kvn, wvk_ref[...]),
                                  _key_pos_features(pos, lane, 0)).astype(BF16)
        vt_s[:HEAD_DIM, rows] = _dot_nt(wvt_ref[...], kvn).astype(BF16)
        vt_s[HEAD_DIM:, rows] = jnp.ones((ONES_ROWS, t), BF16)
        misc = blk[:, SIDE_MISC:]
        kt_s[rows, :] = _dot(misc.astype(BF16), tile_ref[...]).astype(BF16)
        wt_s[:, rows] = misc.T
        return 0

    lax.fori_loop(0, seq // t, prep, 0)

    def q_block(i, _):
        r0 = pl.multiple_of(i * t, t)
        rows = pl.ds(r0, t)
        nk = i + 1

        qi = qi_s[rows, :]
        for h in range(IDX_HEADS):
            in_head = (lane2 >= h * IDX_DIM) & (lane2 < (h + 1) * IDX_DIM)
            qm_s[h] = jnp.where(in_head, qi, jnp.zeros_like(qi))
        w_rows = [wt_s[SIDE_WIDX + h:SIDE_WIDX + h + 1, rows] for h in range(IDX_HEADS)]

        def score_chunk(kc, diagonal):
            c0 = pl.multiple_of(kc * t, t)
            kt = kt_s[pl.ds(c0, t), :]
            acc = jnp.zeros((t, t), F32)
            for h in range(IDX_HEADS):
                acc = acc + w_rows[h] * jnp.maximum(_dot_nt(kt, qm_s[h]), 0.0)
            bits = lax.bitcast_convert_type(acc, I32)
            key = bits ^ (lax.shift_right_arithmetic(bits, 31) & 0x7FFFFFFF)
            key = jnp.where(key == -1, 0, key)
            if diagonal:
                key = jnp.where(key_le_query, key, INT_MIN)
            key_s[pl.ds(c0, t), :] = key
            hi_s[pl.ds(c0, t), :] = lax.shift_right_arithmetic(key, 16).astype(I16)

        def score_quad(kq, _):
            for j in range(4):
                score_chunk(4 * kq + j, False)
            return 0

        lax.fori_loop(0, i // 4, score_quad, 0)

        @pl.when(i % 4 >= 2)
        def _():
            score_chunk((i // 4) * 4, False)
            score_chunk((i // 4) * 4 + 1, False)

        @pl.when(i % 2 == 1)
        def _():
            score_chunk(i - 1, False)

        score_chunk(i, True)

        def count(pred):
            def body(kc, acc):
                c0 = pl.multiple_of(kc * t, t)
                return acc + _tree_fold(jnp.where(pred(key_s[pl.ds(c0, t), :]), 1.0, 0.0), jnp.add)
            acc = lax.fori_loop(0, nk, body, jnp.zeros((SUBLANES, t), F32))
            return jnp.sum(acc, axis=0, keepdims=True)

        def count16(src, cand):
            def body(kc, acc):
                c0 = pl.multiple_of(kc * t, t)
                hit = jnp.where(src[pl.ds(c0, t), :] >= cand, jnp.ones((), BF16), jnp.zeros((), BF16))
                return acc + _tree_fold(hit, jnp.add, PACKED_ROWS)
            acc = lax.fori_loop(0, nk, body, jnp.zeros((PACKED_ROWS, t), BF16))
            return jnp.sum(acc.astype(F32), axis=0, keepdims=True)

        def search16(src):
            def step(p, thr):
                cand = thr + lax.shift_left(jnp.int32(1), 15 - p)
                return jnp.where(count16(src, cand.astype(I16)) >= top_k, cand, thr)
            return lax.fori_loop(0, 16, step, jnp.full((1, t), INT16_MIN, I32))

        t_hi = search16(hi_s)

        def low_body(kc, _):
            c0 = pl.multiple_of(kc * t, t)
            key = key_s[pl.ds(c0, t), :]
            hi = lax.shift_right_arithmetic(key, 16)
            lo = (key & 0xFFFF) + INT16_MIN
            lo_s[pl.ds(c0, t), :] = jnp.where(hi > t_hi, INT16_MAX, jnp.where(hi == t_hi, lo, INT16_MIN)).astype(I16)
            return 0

        lax.fori_loop(0, nk, low_body, 0)
        t_lo = search16(lo_s)
        thr = lax.shift_left(t_hi, 16) | ((t_lo - INT16_MIN) & 0xFFFF)
        thr = jnp.maximum(thr, INT_MIN + 1)

        def mask_body(kc, acc):
            c0 = pl.multiple_of(kc * t, t)
            admitted = key_s[pl.ds(c0, t), :] >= thr
            mb_s[pl.ds(c0, t), :] = jnp.where(admitted, 0.0, NEG)
            return acc + _tree_fold(jnp.where(admitted, 1.0, 0.0), jnp.add)

        n_ge = jnp.sum(lax.fori_loop(0, nk, mask_body, jnp.zeros((SUBLANES, t), F32)), axis=0, keepdims=True)

        @pl.when(jnp.max(n_ge) > top_k)
        def _():
            room = top_k - count(lambda k: k > thr)

            def tie_body(kc, seen):
                c0 = pl.multiple_of(kc * t, t)
                tie = key_s[pl.ds(c0, t), :] == thr
                rank = seen + _dot(tri_ref[...], jnp.where(tie, 1.0, 0.0).astype(BF16))
                drop = tie & (rank > room)
                mb_s[pl.ds(c0, t), :] = jnp.where(drop, NEG, mb_s[pl.ds(c0, t), :])
                return rank[t - 1:t, :]

            lax.fori_loop(0, nk, tie_body, jnp.zeros((1, t), F32))

        for h in range(n_heads):
            half = (lane >= HEAD_DIM) if h % 2 else (lane < HEAD_DIM)
            q2 = qb_s[rows, (h // 2) * LANES:(h // 2 + 1) * LANES]
            qa_s[h] = jnp.where(half, q2, qfeat_ref[h, rows, :])
            m_s[h] = jnp.full((1, t), NEG, F32)
            acc_s[h] = jnp.zeros((HEAD_DIM + ONES_ROWS, t), F32)

        def attend(kc, _):
            c0 = pl.multiple_of(kc * t, t)
            cols = pl.ds(c0, t)
            states = [(m_s[h], acc_s[h]) for h in range(n_heads)]
            mask = mb_s[cols, :]
            k_ops = (ka_s[cols, :], kb_s[cols, :])
            s_ts = [_dot_nt(k_ops[h % 2], qa_s[h]) + mask for h in range(n_heads)]
            v_list = [vt_s[:, cols]] * n_heads
            for h, st in enumerate(_online_update_heads(states, s_ts, v_list)):
                m_s[h], acc_s[h] = st
            return 0

        def attend_pair(kp, _):
            attend(2 * kp, 0)
            attend(2 * kp + 1, 0)
            return 0

        lax.fori_loop(0, nk // 2, attend_pair, 0)

        @pl.when(nk % 2 == 1)
        def _():
            attend(nk - 1, 0)

        for hp in range(n_heads // 2):
            cols = slice(hp * LANES, (hp + 1) * LANES)
            out = _finish_pair(acc_s[2 * hp], acc_s[2 * hp + 1]) * _silu(z_ref[0, rows, cols].astype(F32))
            o_ref[0, rows, cols] = out.astype(o_ref.dtype)
        return 0

    lax.fori_loop(0, seq // t, q_block, 0)


def _dsa_attention(side, main, w, tile, tri, batch, seq):
    assert seq // POS_SPLIT <= 256
    assert seq // PACKED_ROWS <= 256
    top_k = min(IDX_TOPK_MAX, seq // 4)
    slopes = tuple(float(2.0 ** (-8.0 * (h + 1) / B_HEADS)) * LOG2E for h in range(B_HEADS))
    t = ATT_T
    const = lambda a: pl.BlockSpec(a.shape, lambda b: (0,) * a.ndim)
    pos = jnp.arange(seq, dtype=I32).reshape(seq, 1)
    lane = jnp.arange(LANES, dtype=I32).reshape(1, LANES)
    qfeat = jnp.stack([_query_pos_features(pos, lane, 0 if h % 2 else HEAD_DIM, slopes[h])
                       for h in range(B_HEADS)]).astype(BF16)
    weights = [w["g_cq"], w["w_uq"], w["w_uq_idx"], w["g_ckv"], w["w_kv"], w["w_vk"], w["w_v_t"], tile, tri, qfeat]
    return pl.pallas_call(
        functools.partial(_dsa_kernel, top_k=top_k),
        grid=(batch,),
        in_specs=[pl.BlockSpec((1, seq, 512), lambda b: (b, 0, 0)),
                  pl.BlockSpec((1, seq, B_W), lambda b: (b, 0, MAIN_ZB // B_W))]
                 + [const(a) for a in weights],
        out_specs=pl.BlockSpec((1, seq, B_W), lambda b: (b, 0, 0)),
        out_shape=jax.ShapeDtypeStruct((batch, seq, B_W), BF16),
        scratch_shapes=[pltpu.VMEM((seq, B_W), BF16),
                        pltpu.VMEM((seq, IDX_HEADS * IDX_DIM), BF16),
                        pltpu.VMEM((seq, IDX_HEADS * IDX_DIM), BF16),
                        pltpu.VMEM((seq, LANES), BF16),
                        pltpu.VMEM((seq, LANES), BF16),
                        pltpu.VMEM((HEAD_DIM + ONES_ROWS, seq), BF16),
                        pltpu.VMEM((LANES, seq), F32),
                        pltpu.VMEM((IDX_HEADS, t, IDX_HEADS * IDX_DIM), BF16),
                        pltpu.VMEM((B_HEADS, t, LANES), BF16),
                        pltpu.VMEM((seq, t), I32),
                        pltpu.VMEM((seq, t), I16),
                        pltpu.VMEM((seq, t), I16),
                        pltpu.VMEM((seq, t), F32),
                        pltpu.VMEM((B_HEADS, 1, t), F32),
                        pltpu.VMEM((B_HEADS, HEAD_DIM + ONES_ROWS, t), F32)],
        compiler_params=_cparams(1),
        name="dsa_attention",
    )(side, main, *weights)


C_UNROLL = 8


def _dilated_kernel(q_ref, k_ref, v_ref, bias_ref, z0_ref, z1_ref, z2_ref, o_ref, o_s, l_s):
    seq = q_ref.shape[1]
    sp = pl.program_id(1)
    grp = pl.program_id(2)
    lane = lax.broadcasted_iota(I32, (1, LANES), 1)
    cw = CLASS_WINDOW

    def rows(start, d):
        return pl.ds(start, cw) if d == 1 else pl.ds(start, cw, stride=d)

    def group_body(g, d):
        per_class = seq // (d * cw)

        ld = lambda ref, s0: ref[0, rows(s0, d), :].astype(BF16)

        def load_block(n, u, before):
            r = n // per_class
            ib = n % per_class
            start = r + ib * (cw * d)
            b = dict(start=start, q=ld(q_ref, start), kc=ld(k_ref, start), vc=ld(v_ref, start), prev=None)
            if per_class <= C_UNROLL:
                if u % per_class:
                    b.update(prev="static", kp=before["kc"], vp=before["vc"])
            elif u:
                b.update(prev="static", kp=before["kc"], vp=before["vc"])
            else:
                pstart = r + jnp.maximum(ib - 1, 0) * (cw * d)
                b.update(prev="dynamic", kp=ld(k_ref, pstart), vp=ld(v_ref, pstart), has_prev=ib > 0)
            return b

        def blk(it, _):
            blocks = []
            for u in range(C_UNROLL):
                blocks.append(load_block(it * C_UNROLL + u, u, blocks[-1] if blocks else None))
            logits = []
            for b in blocks:
                for e in range(2):
                    half = (lane >= HEAD_DIM) if e else (lane < HEAD_DIM)
                    qm = jnp.where(half, b["q"], jnp.zeros_like(b["q"]))
                    s_cur = _dot_nt(qm, b["kc"]) + bias_ref[0, e, :, cw:]
                    s_prev = None
                    if b["prev"] == "static":
                        s_prev = _dot_nt(qm, b["kp"]) + bias_ref[0, e, :, :cw]
                    elif b["prev"] == "dynamic":
                        s_prev = _dot_nt(qm, b["kp"]) + jnp.where(b["has_prev"], bias_ref[0, e, :, :cw], NEG)
                    logits.append((s_cur, s_prev))
            probs = []
            for s_cur, s_prev in logits:
                if s_prev is None:
                    m = jnp.max(s_cur, axis=-1, keepdims=True)
                    p_cur, p_prev = jnp.exp(s_cur - m), None
                    l = jnp.sum(p_cur, axis=-1, keepdims=True)
                else:
                    m = jnp.max(jnp.maximum(s_cur, s_prev), axis=-1, keepdims=True)
                    p_cur, p_prev = jnp.exp(s_cur - m), jnp.exp(s_prev - m)
                    l = jnp.sum(p_cur + p_prev, axis=-1, keepdims=True)
                    p_prev = p_prev.astype(BF16)
                probs.append((p_cur.astype(BF16), p_prev, 1.0 / l, m + jnp.log(l)))
            for u, b in enumerate(blocks):
                outs = []
                for e in range(2):
                    p_cur, p_prev, inv_l, _ = probs[2 * u + e]
                    o = _dot(p_cur, b["vc"])
                    if p_prev is not None:
                        o = o + _dot(p_prev, b["vp"])
                    outs.append(o * inv_l)
                o_s[g, rows(b["start"], d), :] = jnp.where(lane < HEAD_DIM, outs[0], outs[1])
                l_s[g, rows(b["start"], d), :] = jnp.where(lane < HEAD_DIM, probs[2 * u][3], probs[2 * u + 1][3])
            return 0

        lax.fori_loop(0, seq // (cw * C_UNROLL), blk, 0)

    for g, (_, d) in enumerate(C_GROUPS):
        pl.when(grp == g)(functools.partial(group_body, g, d))

    n_grp = len(C_GROUPS)
    z_refs = (z0_ref, z1_ref, z2_ref)

    def combine(sp_static):
        def chunk(c, _):
            r0 = pl.multiple_of(c * 256, 256)
            ls = [l_s[g, pl.ds(r0, 256), :] for g in range(n_grp)]
            m = functools.reduce(jnp.maximum, ls)
            ws = [jnp.exp(x - m) for x in ls]
            inv = 1.0 / functools.reduce(lambda a, b: a + b, ws)
            for g in range(n_grp):
                col = (g * 2 + sp_static) * LANES
                y = (ws[g] * inv) * o_s[g, pl.ds(r0, 256), :]
                y = y * _silu(z_refs[g][0, pl.ds(r0, 256), :].astype(F32))
                o_ref[0, pl.ds(r0, 256), col:col + LANES] = y.astype(o_ref.dtype)
            return 0

        lax.fori_loop(0, seq // 256, chunk, 0)

    for s in range(2):
        pl.when((grp == n_grp - 1) & (sp == s))(functools.partial(combine, s))


def _dilated_attention(side, main, bias, batch, seq):
    assert seq % (C_GROUPS[-1][1] * CLASS_WINDOW) == 0
    assert (seq // CLASS_WINDOW) % C_UNROLL == 0
    n_grp = len(C_GROUPS)
    blk = lambda base: pl.BlockSpec(
        (1, seq, LANES), lambda b, s, g, base=base: (b, 0, base // LANES + 2 * g + s))
    zblk = lambda g: pl.BlockSpec(
        (1, seq, LANES), lambda b, s, _, g=g: (b, 0, MAIN_ZC // LANES + 2 * g + s))
    return pl.pallas_call(
        _dilated_kernel,
        grid=(batch, 2, n_grp),
        in_specs=[blk(SIDE_QC), blk(SIDE_KC), blk(SIDE_VC),
                  pl.BlockSpec((1, 2, CLASS_WINDOW, 2 * CLASS_WINDOW), lambda b, s, g: (2 * g + s, 0, 0, 0)),
                  zblk(0), zblk(1), zblk(2)],
        out_specs=pl.BlockSpec((1, seq, C_W), lambda b, s, g: (b, 0, 0)),
        out_shape=jax.ShapeDtypeStruct((batch, seq, C_W), BF16),
        scratch_shapes=[pltpu.VMEM((n_grp, seq, LANES), F32),
                        pltpu.VMEM((n_grp, seq, LANES), F32)],
        compiler_params=_cparams(3),
        name="dilated_attention",
    )(side, side, side, bias, main, main, main)


def _dilated_bias():
    cw = CLASS_WINDOW
    slopes = np.exp2(-8.0 * (np.arange(C_HEADS, dtype=np.float32) + 1.0) / C_HEADS).astype(np.float32)
    i = np.arange(cw)[:, None]
    j = np.arange(cw)[None, :]
    out = np.empty((C_HEADS // 2, 2, cw, 2 * cw), np.float32)
    for h in range(C_HEADS):
        d = C_GROUPS[h // C_HEADS_PER_GROUP][1]
        prev = np.where(j >= i, -slopes[h] * np.float32(d) * (cw + i - j).astype(np.float32), NEG)
        cur = np.where(j <= i, -slopes[h] * np.float32(d) * (i - j).astype(np.float32), NEG)
        out[h // 2, h % 2] = np.concatenate([prev, cur], axis=1)
    return jnp.asarray(out)


def _merge_kernel(x_ref, ya_ref, yb_ref, yc_ref, g_ref, wa_ref, wb_ref, wc_ref, wo_ref, fg_ref, o_ref,
                  *, final):
    gate = lambda n: _sigmoid(g_ref[:, n * D_MODEL:(n + 1) * D_MODEL].astype(F32))
    merged = (gate(0) * _dot(ya_ref[...], wa_ref[...])
              + gate(1) * _dot(yb_ref[...], wb_ref[...])
              + gate(2) * _dot(yc_ref[...], wc_ref[...]))
    y = x_ref[...] + _dot(merged.astype(BF16), wo_ref[...])
    if final:
        y = _rms(y, fg_ref[...])
    o_ref[...] = y


def _merge(x2d, ya, yb, yc, main, w_a, w_b, w_c, w_o, final_g, final, tm=512):
    m = x2d.shape[0]
    row = lambda w: pl.BlockSpec((tm, w), lambda i: (i, 0))
    const = lambda a: pl.BlockSpec(a.shape, lambda i: (0, 0))
    return pl.pallas_call(
        functools.partial(_merge_kernel, final=final),
        grid=(m // tm,),
        in_specs=[row(D_MODEL), row(A_W), row(B_W), row(C_W), row(3 * D_MODEL),
                  const(w_a), const(w_b), const(w_c), const(w_o), const(final_g)],
        out_specs=row(D_MODEL),
        out_shape=jax.ShapeDtypeStruct((m, D_MODEL), F32),
        compiler_params=_cparams(1),
        name="gated_merge",
    )(x2d, ya, yb, yc, main, w_a, w_b, w_c, w_o, final_g)


def _layer_weights(norm_g, w_in, b_forget, g_cq, w_uq, w_uq_idx, g_ckv, w_ukv, w_a, w_b, w_c, w_o):
    points = np.cumsum(IN_WIDTHS)[:-1].tolist()
    (wqa, wka, wva, wfa, wza, wcq, wckv, wkidx, wwidx, wzb,
     wqc, wkc, wvc, wzc, wga, wgb, wgc) = jnp.split(w_in, points, axis=1)
    zeros = lambda n: jnp.zeros((D_MODEL, n), F32)
    w_main = jnp.concatenate([wga, wgb, wgc, wzb, wqa * (SCALE * LOG2E), wka, wva, wza, wzc], axis=1)
    w_side = jnp.concatenate(
        [wcq, wckv, wkidx, wwidx, zeros(LANES - IDX_DIM - IDX_HEADS),
         wfa, zeros(LANES - A_HEADS), wqc * SCALE, wkc, wvc, zeros(SIDE_N - SIDE_VC - C_W)], axis=1)
    assert w_main.shape[1] == MAIN_N and w_side.shape[1] == SIDE_N
    wk, wv = w_ukv[:, :HEAD_DIM], w_ukv[:, HEAD_DIM:]
    w_kv = jnp.concatenate([wk, wv], axis=1).astype(BF16)
    w_vk = jnp.concatenate([wv, wk], axis=1).astype(BF16)
    return dict(
        norm_g=norm_g.reshape(1, D_MODEL),
        w_main=w_main.astype(BF16), w_side=w_side.astype(BF16),
        b_pad=jnp.zeros((1, LANES), F32).at[0, :A_HEADS].set(b_forget),
        g_cq=g_cq.reshape(1, B_Q_RANK), w_uq=(w_uq * (SCALE * LOG2E)).astype(BF16), w_uq_idx=w_uq_idx.astype(BF16),
        g_ckv=g_ckv.reshape(1, B_KV_RANK), w_kv=w_kv, w_vk=w_vk, w_v_t=wv.T.astype(BF16),
        w_a=w_a.astype(BF16), w_b=w_b.astype(BF16), w_c=w_c.astype(BF16), w_o=w_o.astype(BF16))


def _constants():
    r = np.arange(LANES)
    tri128 = (r[None, :] <= r[:, None]).astype(np.float32)
    r2 = np.arange(ATT_T)
    tri_t = (r2[None, :] <= r2[:, None]).astype(np.float32)
    c = np.arange(IDX_HEADS * IDX_DIM)
    tile = ((r[:, None] == c[None, :] % IDX_DIM) & (r[:, None] < IDX_DIM)).astype(np.float32)
    sel = np.zeros((A_HEADS, LANES, 2 * LANES), np.float32)
    for h in range(A_HEADS):
        for piece in range(3):
            sel[h, piece * A_HEADS + h, piece] = 1.0
            sel[h, piece * A_HEADS + h, LANES + 3 + piece] = -1.0
    one = np.zeros((1, 2 * LANES), np.float32)
    one[0, 3:6] = 1.0
    one[0, LANES:LANES + 3] = 1.0
    bf = lambda a: jnp.asarray(a, BF16)
    cum = (bf(tri128), bf(sel), jnp.asarray(one))
    return cum, bf(tri_t), bf(tile), _dilated_bias()


def _hybrid_layer(x2d, batch, seq, w, consts, final_g, final):
    cum_consts, tri_t, tile, bias_c = consts
    main = _rms_matmul(x2d, w["norm_g"], w["w_main"], BF16, tm=2048, tn=1280)
    side = _rms_matmul(x2d, w["norm_g"], w["w_side"], F32, tm=1024, tn=SIDE_N)
    main3 = main.reshape(batch, seq, MAIN_N)
    side3 = side.reshape(batch, seq, SIDE_N)
    qf, kf = _cum_forget(side3, w["b_pad"], cum_consts, batch, seq)
    ya = _fox_attention(main3, qf, kf, batch, seq)
    yb = _dsa_attention(side3, main3, w, tile, tri_t, batch, seq)
    yc = _dilated_attention(side3, main3, bias_c, batch, seq)
    m = batch * seq
    return _merge(x2d, ya.reshape(m, A_W), yb.reshape(m, B_W), yc.reshape(m, C_W), main,
                  w["w_a"], w["w_b"], w["w_c"], w["w_o"], final_g, final)


def kernel(x, norm_g, w_in, b_forget, g_cq, w_uq, w_uq_idx, g_ckv, w_ukv, w_a, w_b, w_c, w_o, final_g):
    batch, seq, d_model = x.shape
    assert d_model == D_MODEL and seq % ATT_T == 0 and seq % FOX_T == 0
    depth = norm_g.shape[0]
    consts = _constants()
    fg = final_g.reshape(1, D_MODEL)
    x2d = x.reshape(batch * seq, D_MODEL)
    for l in range(depth):
        w = _layer_weights(norm_g[l], w_in[l], b_forget[l], g_cq[l], w_uq[l], w_uq_idx[l], g_ckv[l],
                           w_ukv[l], w_a[l], w_b[l], w_c[l], w_o[l])
        x2d = _hybrid_layer(x2d, batch, seq, w, consts, fg, final=(l == depth - 1))
    return x2d.reshape(batch, seq, D_MODEL)
```

```python
import functools

import numpy as np
import jax
import jax.numpy as jnp
from jax import lax
from jax.experimental import pallas as pl
from jax.experimental.pallas import tpu as pltpu

F32 = jnp.float32
BF16 = jnp.bfloat16
I32 = jnp.int32
I16 = jnp.int16

D_MODEL = 1024
HEAD_DIM = 64
EPS = 1e-6
A_HEADS = 8
A_W = A_HEADS * HEAD_DIM
B_HEADS = 8
B_W = B_HEADS * HEAD_DIM
B_Q_RANK = 256
B_KV_RANK = 128
IDX_HEADS = 8
IDX_DIM = 32
IDX_TOPK_MAX = 256
C_GROUPS = ((128, 1), (512, 4), (2048, 16))
C_HEADS_PER_GROUP = 4
C_HEADS = C_HEADS_PER_GROUP * len(C_GROUPS)
C_W = C_HEADS * HEAD_DIM
IN_WIDTHS = (A_W, A_W, A_W, A_HEADS, A_W,
             B_Q_RANK, B_KV_RANK, IDX_DIM, IDX_HEADS, B_W,
             C_W, C_W, C_W, C_W,
             D_MODEL, D_MODEL, D_MODEL)

LANES = 128
SUBLANES = 8
PACKED_ROWS = 16
SCALE = HEAD_DIM ** -0.5
LOG2E = float(np.log2(np.e))
ONES_ROWS = 16
NEG = -1e30
INT_MIN = -2 ** 31
INT16_MIN, INT16_MAX = -2 ** 15, 2 ** 15 - 1
CLASS_WINDOW = 128
assert all(w // d == CLASS_WINDOW for w, d in C_GROUPS)
ATT_T = 256
FOX_T = 512
POS_SPLIT = 64

MAIN_G = 0
MAIN_ZB = 3 * D_MODEL
MAIN_QA = MAIN_ZB + B_W
MAIN_KA = MAIN_QA + A_W
MAIN_VA = MAIN_KA + A_W
MAIN_ZA = MAIN_VA + A_W
MAIN_ZC = MAIN_ZA + A_W
MAIN_N = MAIN_ZC + C_W
SIDE_CQ = 0
SIDE_MISC = B_Q_RANK + B_KV_RANK
SIDE_WIDX = IDX_DIM
SIDE_FA = 512
SIDE_QC = SIDE_FA + LANES
SIDE_KC = SIDE_QC + C_W
SIDE_VC = SIDE_KC + C_W
SIDE_N = 3072

VMEM_LIMIT = 56 * 1024 * 1024


def _cparams(n_axes, vmem=VMEM_LIMIT):
    return pltpu.CompilerParams(dimension_semantics=("arbitrary",) * n_axes,
                                vmem_limit_bytes=vmem)


def _dot(a, b):
    return jnp.dot(a, b, preferred_element_type=F32)


def _dot_nt(a, b):
    return lax.dot_general(a, b, (((1,), (1,)), ((), ())), preferred_element_type=F32)


def _sigmoid(x):
    return 1.0 / (1.0 + jnp.exp(-x))


def _silu(x):
    return x * _sigmoid(x)


def _rms(x, g):
    return x * lax.rsqrt(jnp.mean(x * x, axis=-1, keepdims=True) + EPS) * g


def _split3(x):
    hi = x.astype(BF16)
    r1 = x - hi.astype(F32)
    mid = r1.astype(BF16)
    lo = (r1 - mid.astype(F32)).astype(BF16)
    return hi, mid, lo


def _fold8(x, op):
    n, t = x.shape
    return op(x.reshape(n // SUBLANES, SUBLANES, t), axis=0)


def _tree_fold(x, op, rows=SUBLANES):
    n, t = x.shape
    parts = [x[r:r + rows, :] for r in range(0, n, rows)]
    while len(parts) > 1:
        nxt = [op(parts[j], parts[j + 1]) for j in range(0, len(parts) - 1, 2)]
        if len(parts) % 2:
            nxt.append(parts[-1])
        parts = nxt
    return parts[0]


def _keys_max(x):
    return jnp.max(_fold8(x, jnp.max), axis=0, keepdims=True)


def _online_update_heads(states, s_ts, v_ts):
    ps, scaled = [], []
    for (m, acc), s_t in zip(states, s_ts):
        m_new = jnp.maximum(m, _keys_max(s_t))
        alpha = jnp.exp2(m - m_new)
        ps.append(jnp.exp2(s_t - m_new).astype(BF16))
        scaled.append((m_new, alpha * acc))
    return [(m, acc + _dot(v_t, p)) for (m, acc), v_t, p in zip(scaled, v_ts, ps)]


def _finish_pair(acc0, acc1):
    norm = lambda acc: acc[:HEAD_DIM, :] * (1.0 / acc[HEAD_DIM:HEAD_DIM + 1, :])
    return jnp.concatenate([norm(acc0), norm(acc1)], axis=0).T


def _rms_matmul_kernel(x_ref, g_ref, w_ref, o_ref, h_ref):
    @pl.when(pl.program_id(1) == 0)
    def _():
        h_ref[...] = _rms(x_ref[...], g_ref[...]).astype(BF16)

    o_ref[...] = _dot(h_ref[...], w_ref[...]).astype(o_ref.dtype)


def _rms_matmul(x2d, g, w, out_dtype, tm, tn):
    m, k = x2d.shape
    n = w.shape[1]
    return pl.pallas_call(
        _rms_matmul_kernel,
        grid=(m // tm, n // tn),
        in_specs=[pl.BlockSpec((tm, k), lambda i, j: (i, 0)),
                  pl.BlockSpec((1, k), lambda i, j: (0, 0)),
                  pl.BlockSpec((k, tn), lambda i, j: (0, j))],
        out_specs=pl.BlockSpec((tm, tn), lambda i, j: (i, j)),
        out_shape=jax.ShapeDtypeStruct((m, n), out_dtype),
        scratch_shapes=[pltpu.VMEM((tm, k), BF16)],
        compiler_params=_cparams(2),
        name="rms_in_proj",
    )(x2d, g, w)


def _cum_kernel(fa_ref, b_ref, tri_ref, sel_ref, one_ref, qf_ref, kf_ref, c3_s):
    seq = fa_ref.shape[1]
    tri = tri_ref[...]
    lane = lax.broadcasted_iota(I32, (1, LANES), 1)
    carry = jnp.zeros((1, LANES), F32)
    for blk in range(seq // LANES):
        rows = slice(blk * LANES, (blk + 1) * LANES)
        x = fa_ref[0, rows, :] + b_ref[...]
        lf = jnp.minimum(x, 0.0) - jnp.log(1.0 + jnp.exp(-jnp.abs(x)))
        hi, mid, lo = _split3(lf)
        c = _dot(tri, hi) + _dot(tri, mid) + _dot(tri, lo) + carry
        carry = c[LANES - 1:LANES, :]
        hi, mid, lo = _split3(jnp.where(lane < A_HEADS, c * LOG2E, 0.0))
        c3_s[rows, :] = (hi.astype(F32) + pltpu.roll(mid.astype(F32), A_HEADS, 1)
                         + pltpu.roll(lo.astype(F32), 2 * A_HEADS, 1)).astype(BF16)
    half = seq // 2
    for h in range(A_HEADS):
        for rows in (slice(0, half), slice(half, seq)):
            feats = _dot(c3_s[rows, :], sel_ref[h]) + one_ref[...]
            qf_ref[0, h, rows, :] = feats[:, :LANES].astype(BF16)
            kf_ref[0, h, rows, :] = feats[:, LANES:].astype(BF16)


def _cum_forget(side, b_pad, consts, batch, seq):
    tri, sel, one = consts
    full = lambda a: pl.BlockSpec(a.shape, lambda b: (0,) * a.ndim)
    feat = jax.ShapeDtypeStruct((batch, A_HEADS, seq, LANES), BF16)
    return pl.pallas_call(
        _cum_kernel,
        grid=(batch,),
        in_specs=[pl.BlockSpec((1, seq, LANES), lambda b: (b, 0, SIDE_FA // LANES)),
                  full(b_pad), full(tri), full(sel), full(one)],
        out_specs=[pl.BlockSpec((1, A_HEADS, seq, LANES), lambda b: (b, 0, 0, 0))] * 2,
        out_shape=[feat, feat],
        scratch_shapes=[pltpu.VMEM((seq, LANES), BF16)],
        compiler_params=_cparams(1),
        name="fox_cumsum",
    )(side, b_pad, tri, sel, one)


def _fox_kernel(q_ref, k_ref, v_ref, z_ref, qf_ref, kf_ref, o_ref, vt_s, qa_s, m_s, acc_s):
    seq = q_ref.shape[1]
    t = FOX_T
    n_heads = A_HEADS
    lane = lax.broadcasted_iota(I32, (1, LANES), 1)
    key_le_query = (lax.broadcasted_iota(I32, (t, t), 0) <= lax.broadcasted_iota(I32, (t, t), 1))
    pair_cols = lambda h: slice((h // 2) * LANES, (h // 2 + 1) * LANES)

    def transpose_v(c, _):
        cols = pl.ds(pl.multiple_of(c * t, t), t)
        for p in range(n_heads // 2):
            pc = slice(p * LANES, (p + 1) * LANES)
            v_t = v_ref[0, cols, pc].astype(F32).T.astype(BF16)
            for e in range(2):
                vt_s[2 * p + e, :HEAD_DIM, cols] = v_t[e * HEAD_DIM:(e + 1) * HEAD_DIM, :]
                vt_s[2 * p + e, HEAD_DIM:, cols] = jnp.ones((ONES_ROWS, t), BF16)
        return 0

    lax.fori_loop(0, seq // t, transpose_v, 0)

    def q_block(i, _):
        rows = pl.ds(pl.multiple_of(i * t, t), t)
        for h in range(n_heads):
            half = (lane >= HEAD_DIM) if h % 2 else (lane < HEAD_DIM)
            q2 = q_ref[0, rows, pair_cols(h)]
            qa_s[h] = jnp.concatenate([jnp.where(half, q2, jnp.zeros_like(q2)), qf_ref[0, h, rows, :]], axis=1)
            m_s[h] = jnp.full((1, t), NEG, F32)
            acc_s[h] = jnp.zeros((HEAD_DIM + ONES_ROWS, t), F32)

        def chunk(kb, diagonal):
            cols = pl.ds(pl.multiple_of(kb * t, t), t)
            states = [(m_s[h], acc_s[h]) for h in range(n_heads)]
            s_ts = [_dot_nt(jnp.concatenate([k_ref[0, cols, pair_cols(h)], kf_ref[0, h, cols, :]], axis=1),
                            qa_s[h]) for h in range(n_heads)]
            if diagonal:
                s_ts = [jnp.where(key_le_query, s_t, NEG) for s_t in s_ts]
            v_ts = [vt_s[h, :, cols] for h in range(n_heads)]
            for h, st in enumerate(_online_update_heads(states, s_ts, v_ts)):
                m_s[h], acc_s[h] = st

        def off_diagonal(kb, _):
            chunk(kb, False)
            return 0

        lax.fori_loop(0, i, off_diagonal, 0)
        chunk(i, True)
        for p in range(n_heads // 2):
            h0, h1 = 2 * p, 2 * p + 1
            out = _finish_pair(acc_s[h0], acc_s[h1]) * _silu(z_ref[0, rows, pair_cols(h0)].astype(F32))
            o_ref[0, rows, pair_cols(h0)] = out.astype(o_ref.dtype)
        return 0

    lax.fori_loop(0, seq // t, q_block, 0)


def _fox_attention(main, qf, kf, batch, seq):
    t = FOX_T
    blk = lambda base: pl.BlockSpec((1, seq, A_W), lambda b, base=base: (b, 0, base // A_W))
    feat = pl.BlockSpec((1, A_HEADS, seq, LANES), lambda b: (b, 0, 0, 0))
    return pl.pallas_call(
        _fox_kernel,
        grid=(batch,),
        in_specs=[blk(MAIN_QA), blk(MAIN_KA), blk(MAIN_VA), blk(MAIN_ZA), feat, feat],
        out_specs=pl.BlockSpec((1, seq, A_W), lambda b: (b, 0, 0)),
        out_shape=jax.ShapeDtypeStruct((batch, seq, A_W), BF16),
        scratch_shapes=[pltpu.VMEM((A_HEADS, HEAD_DIM + ONES_ROWS, seq), BF16),
                        pltpu.VMEM((A_HEADS, t, 2 * LANES), BF16),
                        pltpu.VMEM((A_HEADS, 1, t), F32),
                        pltpu.VMEM((A_HEADS, HEAD_DIM + ONES_ROWS, t), F32)],
        compiler_params=_cparams(1),
        name="fox_attention",
    )(main, main, main, main, qf, kf)


def _lane_group(lane, first):
    rel = lane - first
    return jnp.where((rel >= 0) & (rel < 9), rel // 3, -1)


def _key_pos_features(pos, lane, first):
    hi = lax.shift_right_logical(pos, 6).astype(F32)
    lo = (pos & (POS_SPLIT - 1)).astype(F32)
    grp = _lane_group(lane, first)
    return jnp.where(grp == 0, hi, jnp.where(grp == 1, lo, jnp.where(grp == 2, 1.0, 0.0)))


def _query_pos_features(pos, lane, first, slope):
    grp = _lane_group(lane, first)
    x = jnp.where(grp == 0, POS_SPLIT * slope,
                  jnp.where(grp == 1, slope, jnp.where(grp == 2, -slope * pos.astype(F32), 0.0)))
    hi, mid, lo = _split3(x)
    piece = (lane - first) % 3
    return jnp.where(piece == 0, hi, jnp.where(piece == 1, mid, lo))


def _dsa_kernel(side_ref, z_ref, gcq_ref, wuq_ref, wuqi_ref, gckv_ref, wkv_ref, wvk_ref, wvt_ref,
                tile_ref, tri_ref, qfeat_ref, o_ref,
                qb_s, qi_s, kt_s, ka_s, kb_s, vt_s, wt_s, qm_s, qa_s, key_s, hi_s, lo_s, mb_s, m_s, acc_s,
                *, top_k):
    seq = side_ref.shape[1]
    t = ATT_T
    n_heads = B_HEADS
    lane = lax.broadcasted_iota(I32, (1, LANES), 1)
    lane2 = lax.broadcasted_iota(I32, (1, 2 * LANES), 1)
    key_le_query = (lax.broadcasted_iota(I32, (t, t), 0) <= lax.broadcasted_iota(I32, (t, t), 1))
    row_iota = lax.broadcasted_iota(I32, (t, 1), 0)

    def prep(c, _):
        r0 = pl.multiple_of(c * t, t)
        rows = pl.ds(r0, t)
        blk = side_ref[0, rows, :]
        cqn = _rms(blk[:, SIDE_CQ:SIDE_CQ + B_Q_RANK], gcq_ref[...]).astype(BF16)
        qb_s[rows, :] = _dot(cqn, wuq_ref[...]).astype(BF16)
        qi_s[rows, :] = _dot(cqn, wuqi_ref[...]).astype(BF16)
        kvn = _rms(blk[:, B_Q_RANK:SIDE_MISC], gckv_ref[...]).astype(BF16)
        pos = r0 + row_iota
        ka_s[rows, :] = jnp.where(lane < HEAD_DIM, _dot(kvn, wkv_ref[...]),
                                  _key_pos_features(pos, lane, HEAD_DIM)).astype(BF16)
        kb_s[rows, :] = jnp.where(lane >= HEAD_DIM, _dot(kvn, wvk_ref[...]),
                                  _key_pos_features(pos, lane, 0)).astype(BF16)
        vt_s[:HEAD_DIM, rows] = _dot_nt(wvt_ref[...], kvn).astype(BF16)
        vt_s[HEAD_DIM:, rows] = jnp.ones((ONES_ROWS, t), BF16)
        misc = blk[:, SIDE_MISC:]
        kt_s[rows, :] = _dot(misc.astype(BF16), tile_ref[...]).astype(BF16)
        wt_s[:, rows] = misc.T
        return 0

    lax.fori_loop(0, seq // t, prep, 0)

    def q_block(i, _):
        r0 = pl.multiple_of(i * t, t)
        rows = pl.ds(r0, t)
        nk = i + 1

        qi = qi_s[rows, :]
        for h in range(IDX_HEADS):
            in_head = (lane2 >= h * IDX_DIM) & (lane2 < (h + 1) * IDX_DIM)
            qm_s[h] = jnp.where(in_head, qi, jnp.zeros_like(qi))
        w_rows = [wt_s[SIDE_WIDX + h:SIDE_WIDX + h + 1, rows] for h in range(IDX_HEADS)]

        def score_chunk(kc, diagonal):
            c0 = pl.multiple_of(kc * t, t)
            kt = kt_s[pl.ds(c0, t), :]
            acc = jnp.zeros((t, t), F32)
            for h in range(IDX_HEADS):
                acc = acc + w_rows[h] * jnp.maximum(_dot_nt(kt, qm_s[h]), 0.0)
            bits = lax.bitcast_convert_type(acc, I32)
            key = bits ^ (lax.shift_right_arithmetic(bits, 31) & 0x7FFFFFFF)
            key = jnp.where(key == -1, 0, key)
            if diagonal:
                key = jnp.where(key_le_query, key, INT_MIN)
            key_s[pl.ds(c0, t), :] = key
            hi_s[pl.ds(c0, t), :] = lax.shift_right_arithmetic(key, 16).astype(I16)

        def score_quad(kq, _):
            for j in range(4):
                score_chunk(4 * kq + j, False)
            return 0

        lax.fori_loop(0, i // 4, score_quad, 0)

        @pl.when(i % 4 >= 2)
        def _():
            score_chunk((i // 4) * 4, False)
            score_chunk((i // 4) * 4 + 1, False)

        @pl.when(i % 2 == 1)
        def _():
            score_chunk(i - 1, False)

        score_chunk(i, True)

        def count(pred):
            def body(kc, acc):
                c0 = pl.multiple_of(kc * t, t)
                return acc + _tree_fold(jnp.where(pred(key_s[pl.ds(c0, t), :]), 1.0, 0.0), jnp.add)
            acc = lax.fori_loop(0, nk, body, jnp.zeros((SUBLANES, t), F32))
            return jnp.sum(acc, axis=0, keepdims=True)

        def count16(src, cand):
            def body(kc, acc):
                c0 = pl.multiple_of(kc * t, t)
                hit = jnp.where(src[pl.ds(c0, t), :] >= cand, jnp.ones((), BF16), jnp.zeros((), BF16))
                return acc + _tree_fold(hit, jnp.add, PACKED_ROWS)
            acc = lax.fori_loop(0, nk, body, jnp.zeros((PACKED_ROWS, t), BF16))
            return jnp.sum(acc.astype(F32), axis=0, keepdims=True)

        def search16(src):
            def step(p, thr):
                cand = thr + lax.shift_left(jnp.int32(1), 15 - p)
                return jnp.where(count16(src, cand.astype(I16)) >= top_k, cand, thr)
            return lax.fori_loop(0, 16, step, jnp.full((1, t), INT16_MIN, I32))

        t_hi = search16(hi_s)

        def low_body(kc, _):
            c0 = pl.multiple_of(kc * t, t)
            key = key_s[pl.ds(c0, t), :]
            hi = lax.shift_right_arithmetic(key, 16)
            lo = (key & 0xFFFF) + INT16_MIN
            lo_s[pl.ds(c0, t), :] = jnp.where(hi > t_hi, INT16_MAX, jnp.where(hi == t_hi, lo, INT16_MIN)).astype(I16)
            return 0

        lax.fori_loop(0, nk, low_body, 0)
        t_lo = search16(lo_s)
        thr = lax.shift_left(t_hi, 16) | ((t_lo - INT16_MIN) & 0xFFFF)
        thr = jnp.maximum(thr, INT_MIN + 1)

        def mask_body(kc, acc):
            c0 = pl.multiple_of(kc * t, t)
            admitted = key_s[pl.ds(c0, t), :] >= thr
            mb_s[pl.ds(c0, t), :] = jnp.where(admitted, 0.0, NEG)
            return acc + _tree_fold(jnp.where(admitted, 1.0, 0.0), jnp.add)

        n_ge = jnp.sum(lax.fori_loop(0, nk, mask_body, jnp.zeros((SUBLANES, t), F32)), axis=0, keepdims=True)

        @pl.when(jnp.max(n_ge) > top_k)
        def _():
            room = top_k - count(lambda k: k > thr)

            def tie_body(kc, seen):
                c0 = pl.multiple_of(kc * t, t)
                tie = key_s[pl.ds(c0, t), :] == thr
                rank = seen + _dot(tri_ref[...], jnp.where(tie, 1.0, 0.0).astype(BF16))
                drop = tie & (rank > room)
                mb_s[pl.ds(c0, t), :] = jnp.where(drop, NEG, mb_s[pl.ds(c0, t), :])
                return rank[t - 1:t, :]

            lax.fori_loop(0, nk, tie_body, jnp.zeros((1, t), F32))

        for h in range(n_heads):
            half = (lane >= HEAD_DIM) if h % 2 else (lane < HEAD_DIM)
            q2 = qb_s[rows, (h // 2) * LANES:(h // 2 + 1) * LANES]
            qa_s[h] = jnp.where(half, q2, qfeat_ref[h, rows, :])
            m_s[h] = jnp.full((1, t), NEG, F32)
            acc_s[h] = jnp.zeros((HEAD_DIM + ONES_ROWS, t), F32)

        def attend(kc, _):
            c0 = pl.multiple_of(kc * t, t)
            cols = pl.ds(c0, t)
            states = [(m_s[h], acc_s[h]) for h in range(n_heads)]
            mask = mb_s[cols, :]
            k_ops = (ka_s[cols, :], kb_s[cols, :])
            s_ts = [_dot_nt(k_ops[h % 2], qa_s[h]) + mask for h in range(n_heads)]
            v_list = [vt_s[:, cols]] * n_heads
            for h, st in enumerate(_online_update_heads(states, s_ts, v_list)):
                m_s[h], acc_s[h] = st
            return 0

        def attend_pair(kp, _):
            attend(2 * kp, 0)
            attend(2 * kp + 1, 0)
            return 0

        lax.fori_loop(0, nk // 2, attend_pair, 0)

        @pl.when(nk % 2 == 1)
        def _():
            attend(nk - 1, 0)

        for hp in range(n_heads // 2):
            cols = slice(hp * LANES, (hp + 1) * LANES)
            out = _finish_pair(acc_s[2 * hp], acc_s[2 * hp + 1]) * _silu(z_ref[0, rows, cols].astype(F32))
            o_ref[0, rows, cols] = out.astype(o_ref.dtype)
        return 0

    lax.fori_loop(0, seq // t, q_block, 0)


def _dsa_attention(side, main, w, tile, tri, batch, seq):
    assert seq // POS_SPLIT <= 256
    assert seq // PACKED_ROWS <= 256
    top_k = min(IDX_TOPK_MAX, seq // 4)
    slopes = tuple(float(2.0 ** (-8.0 * (h + 1) / B_HEADS)) * LOG2E for h in range(B_HEADS))
    t = ATT_T
    const = lambda a: pl.BlockSpec(a.shape, lambda b: (0,) * a.ndim)
    pos = jnp.arange(seq, dtype=I32).reshape(seq, 1)
    lane = jnp.arange(LANES, dtype=I32).reshape(1, LANES)
    qfeat = jnp.stack([_query_pos_features(pos, lane, 0 if h % 2 else HEAD_DIM, slopes[h])
                       for h in range(B_HEADS)]).astype(BF16)
    weights = [w["g_cq"], w["w_uq"], w["w_uq_idx"], w["g_ckv"], w["w_kv"], w["w_vk"], w["w_v_t"], tile, tri, qfeat]
    return pl.pallas_call(
        functools.partial(_dsa_kernel, top_k=top_k),
        grid=(batch,),
        in_specs=[pl.BlockSpec((1, seq, 512), lambda b: (b, 0, 0)),
                  pl.BlockSpec((1, seq, B_W), lambda b: (b, 0, MAIN_ZB // B_W))]
                 + [const(a) for a in weights],
        out_specs=pl.BlockSpec((1, seq, B_W), lambda b: (b, 0, 0)),
        out_shape=jax.ShapeDtypeStruct((batch, seq, B_W), BF16),
        scratch_shapes=[pltpu.VMEM((seq, B_W), BF16),
                        pltpu.VMEM((seq, IDX_HEADS * IDX_DIM), BF16),
                        pltpu.VMEM((seq, IDX_HEADS * IDX_DIM), BF16),
                        pltpu.VMEM((seq, LANES), BF16),
                        pltpu.VMEM((seq, LANES), BF16),
                        pltpu.VMEM((HEAD_DIM + ONES_ROWS, seq), BF16),
                        pltpu.VMEM((LANES, seq), F32),
                        pltpu.VMEM((IDX_HEADS, t, IDX_HEADS * IDX_DIM), BF16),
                        pltpu.VMEM((B_HEADS, t, LANES), BF16),
                        pltpu.VMEM((seq, t), I32),
                        pltpu.VMEM((seq, t), I16),
                        pltpu.VMEM((seq, t), I16),
                        pltpu.VMEM((seq, t), F32),
                        pltpu.VMEM((B_HEADS, 1, t), F32),
                        pltpu.VMEM((B_HEADS, HEAD_DIM + ONES_ROWS, t), F32)],
        compiler_params=_cparams(1),
        name="dsa_attention",
    )(side, main, *weights)


C_UNROLL = 8


def _dilated_kernel(q_ref, k_ref, v_ref, bias_ref, z0_ref, z1_ref, z2_ref, o_ref, o_s, l_s):
    seq = q_ref.shape[1]
    sp = pl.program_id(1)
    grp = pl.program_id(2)
    lane = lax.broadcasted_iota(I32, (1, LANES), 1)
    cw = CLASS_WINDOW

    def rows(start, d):
        return pl.ds(start, cw) if d == 1 else pl.ds(start, cw, stride=d)

    def group_body(g, d):
        per_class = seq // (d * cw)

        ld = lambda ref, s0: ref[0, rows(s0, d), :].astype(BF16)

        def load_block(n, u, before):
            r = n // per_class
            ib = n % per_class
            start = r + ib * (cw * d)
            b = dict(start=start, q=ld(q_ref, start), kc=ld(k_ref, start), vc=ld(v_ref, start), prev=None)
            if per_class <= C_UNROLL:
                if u % per_class:
                    b.update(prev="static", kp=before["kc"], vp=before["vc"])
            elif u:
                b.update(prev="static", kp=before["kc"], vp=before["vc"])
            else:
                pstart = r + jnp.maximum(ib - 1, 0) * (cw * d)
                b.update(prev="dynamic", kp=ld(k_ref, pstart), vp=ld(v_ref, pstart), has_prev=ib > 0)
            return b

        def blk(it, _):
            blocks = []
            for u in range(C_UNROLL):
                blocks.append(load_block(it * C_UNROLL + u, u, blocks[-1] if blocks else None))
            logits = []
            for b in blocks:
                for e in range(2):
                    half = (lane >= HEAD_DIM) if e else (lane < HEAD_DIM)
                    qm = jnp.where(half, b["q"], jnp.zeros_like(b["q"]))
                    s_cur = _dot_nt(qm, b["kc"]) + bias_ref[0, e, :, cw:]
                    s_prev = None
                    if b["prev"] == "static":
                        s_prev = _dot_nt(qm, b["kp"]) + bias_ref[0, e, :, :cw]
                    elif b["prev"] == "dynamic":
                        s_prev = _dot_nt(qm, b["kp"]) + jnp.where(b["has_prev"], bias_ref[0, e, :, :cw], NEG)
                    logits.append((s_cur, s_prev))
            probs = []
            for s_cur, s_prev in logits:
                if s_prev is None:
                    m = jnp.max(s_cur, axis=-1, keepdims=True)
                    p_cur, p_prev = jnp.exp(s_cur - m), None
                    l = jnp.sum(p_cur, axis=-1, keepdims=True)
                else:
                    m = jnp.max(jnp.maximum(s_cur, s_prev), axis=-1, keepdims=True)
                    p_cur, p_prev = jnp.exp(s_cur - m), jnp.exp(s_prev - m)
                    l = jnp.sum(p_cur + p_prev, axis=-1, keepdims=True)
                    p_prev = p_prev.astype(BF16)
                probs.append((p_cur.astype(BF16), p_prev, 1.0 / l, m + jnp.log(l)))
            for u, b in enumerate(blocks):
                outs = []
                for e in range(2):
                    p_cur, p_prev, inv_l, _ = probs[2 * u + e]
                    o = _dot(p_cur, b["vc"])
                    if p_prev is not None:
                        o = o + _dot(p_prev, b["vp"])
                    outs.append(o * inv_l)
                o_s[g, rows(b["start"], d), :] = jnp.where(lane < HEAD_DIM, outs[0], outs[1])
                l_s[g, rows(b["start"], d), :] = jnp.where(lane < HEAD_DIM, probs[2 * u][3], probs[2 * u + 1][3])
            return 0

        lax.fori_loop(0, seq // (cw * C_UNROLL), blk, 0)

    for g, (_, d) in enumerate(C_GROUPS):
        pl.when(grp == g)(functools.partial(group_body, g, d))

    n_grp = len(C_GROUPS)
    z_refs = (z0_ref, z1_ref, z2_ref)

    def combine(sp_static):
        def chunk(c, _):
            r0 = pl.multiple_of(c * 256, 256)
            ls = [l_s[g, pl.ds(r0, 256), :] for g in range(n_grp)]
            m = functools.reduce(jnp.maximum, ls)
            ws = [jnp.exp(x - m) for x in ls]
            inv = 1.0 / functools.reduce(lambda a, b: a + b, ws)
            for g in range(n_grp):
                col = (g * 2 + sp_static) * LANES
                y = (ws[g] * inv) * o_s[g, pl.ds(r0, 256), :]
                y = y * _silu(z_refs[g][0, pl.ds(r0, 256), :].astype(F32))
                o_ref[0, pl.ds(r0, 256), col:col + LANES] = y.astype(o_ref.dtype)
            return 0

        lax.fori_loop(0, seq // 256, chunk, 0)

    for s in range(2):
        pl.when((grp == n_grp - 1) & (sp == s))(functools.partial(combine, s))


def _dilated_attention(side, main, bias, batch, seq):
    assert seq % (C_GROUPS[-1][1] * CLASS_WINDOW) == 0
    assert (seq // CLASS_WINDOW) % C_UNROLL == 0
    n_grp = len(C_GROUPS)
    blk = lambda base: pl.BlockSpec(
        (1, seq, LANES), lambda b, s, g, base=base: (b, 0, base // LANES + 2 * g + s))
    zblk = lambda g: pl.BlockSpec(
        (1, seq, LANES), lambda b, s, _, g=g: (b, 0, MAIN_ZC // LANES + 2 * g + s))
    return pl.pallas_call(
        _dilated_kernel,
        grid=(batch, 2, n_grp),
        in_specs=[blk(SIDE_QC), blk(SIDE_KC), blk(SIDE_VC),
                  pl.BlockSpec((1, 2, CLASS_WINDOW, 2 * CLASS_WINDOW), lambda b, s, g: (2 * g + s, 0, 0, 0)),
                  zblk(0), zblk(1), zblk(2)],
        out_specs=pl.BlockSpec((1, seq, C_W), lambda b, s, g: (b, 0, 0)),
        out_shape=jax.ShapeDtypeStruct((batch, seq, C_W), BF16),
        scratch_shapes=[pltpu.VMEM((n_grp, seq, LANES), F32),
                        pltpu.VMEM((n_grp, seq, LANES), F32)],
        compiler_params=_cparams(3),
        name="dilated_attention",
    )(side, side, side, bias, main, main, main)


def _dilated_bias():
    cw = CLASS_WINDOW
    slopes = np.exp2(-8.0 * (np.arange(C_HEADS, dtype=np.float32) + 1.0) / C_HEADS).astype(np.float32)
    i = np.arange(cw)[:, None]
    j = np.arange(cw)[None, :]
    out = np.empty((C_HEADS // 2, 2, cw, 2 * cw), np.float32)
    for h in range(C_HEADS):
        d = C_GROUPS[h // C_HEADS_PER_GROUP][1]
        prev = np.where(j >= i, -slopes[h] * np.float32(d) * (cw + i - j).astype(np.float32), NEG)
        cur = np.where(j <= i, -slopes[h] * np.float32(d) * (i - j).astype(np.float32), NEG)
        out[h // 2, h % 2] = np.concatenate([prev, cur], axis=1)
    return jnp.asarray(out)


def _merge_kernel(x_ref, ya_ref, yb_ref, yc_ref, g_ref, wa_ref, wb_ref, wc_ref, wo_ref, fg_ref, o_ref,
                  *, final):
    gate = lambda n: _sigmoid(g_ref[:, n * D_MODEL:(n + 1) * D_MODEL].astype(F32))
    merged = (gate(0) * _dot(ya_ref[...], wa_ref[...])
              + gate(1) * _dot(yb_ref[...], wb_ref[...])
              + gate(2) * _dot(yc_ref[...], wc_ref[...]))
    y = x_ref[...] + _dot(merged.astype(BF16), wo_ref[...])
    if final:
        y = _rms(y, fg_ref[...])
    o_ref[...] = y


def _merge(x2d, ya, yb, yc, main, w_a, w_b, w_c, w_o, final_g, final, tm=512):
    m = x2d.shape[0]
    row = lambda w: pl.BlockSpec((tm, w), lambda i: (i, 0))
    const = lambda a: pl.BlockSpec(a.shape, lambda i: (0, 0))
    return pl.pallas_call(
        functools.partial(_merge_kernel, final=final),
        grid=(m // tm,),
        in_specs=[row(D_MODEL), row(A_W), row(B_W), row(C_W), row(3 * D_MODEL),
                  const(w_a), const(w_b), const(w_c), const(w_o), const(final_g)],
        out_specs=row(D_MODEL),
        out_shape=jax.ShapeDtypeStruct((m, D_MODEL), F32),
        compiler_params=_cparams(1),
        name="gated_merge",
    )(x2d, ya, yb, yc, main, w_a, w_b, w_c, w_o, final_g)


def _layer_weights(norm_g, w_in, b_forget, g_cq, w_uq, w_uq_idx, g_ckv, w_ukv, w_a, w_b, w_c, w_o):
    points = np.cumsum(IN_WIDTHS)[:-1].tolist()
    (wqa, wka, wva, wfa, wza, wcq, wckv, wkidx, wwidx, wzb,
     wqc, wkc, wvc, wzc, wga, wgb, wgc) = jnp.split(w_in, points, axis=1)
    zeros = lambda n: jnp.zeros((D_MODEL, n), F32)
    w_main = jnp.concatenate([wga, wgb, wgc, wzb, wqa * (SCALE * LOG2E), wka, wva, wza, wzc], axis=1)
    w_side = jnp.concatenate(
        [wcq, wckv, wkidx, wwidx, zeros(LANES - IDX_DIM - IDX_HEADS),
         wfa, zeros(LANES - A_HEADS), wqc * SCALE, wkc, wvc, zeros(SIDE_N - SIDE_VC - C_W)], axis=1)
    assert w_main.shape[1] == MAIN_N and w_side.shape[1] == SIDE_N
    wk, wv = w_ukv[:, :HEAD_DIM], w_ukv[:, HEAD_DIM:]
    w_kv = jnp.concatenate([wk, wv], axis=1).astype(BF16)
    w_vk = jnp.concatenate([wv, wk], axis=1).astype(BF16)
    return dict(
        norm_g=norm_g.reshape(1, D_MODEL),
        w_main=w_main.astype(BF16), w_side=w_side.astype(BF16),
        b_pad=jnp.zeros((1, LANES), F32).at[0, :A_HEADS].set(b_forget),
        g_cq=g_cq.reshape(1, B_Q_RANK), w_uq=(w_uq * (SCALE * LOG2E)).astype(BF16), w_uq_idx=w_uq_idx.astype(BF16),
        g_ckv=g_ckv.reshape(1, B_KV_RANK), w_kv=w_kv, w_vk=w_vk, w_v_t=wv.T.astype(BF16),
        w_a=w_a.astype(BF16), w_b=w_b.astype(BF16), w_c=w_c.astype(BF16), w_o=w_o.astype(BF16))


def _constants():
    r = np.arange(LANES)
    tri128 = (r[None, :] <= r[:, None]).astype(np.float32)
    r2 = np.arange(ATT_T)
    tri_t = (r2[None, :] <= r2[:, None]).astype(np.float32)
    c = np.arange(IDX_HEADS * IDX_DIM)
    tile = ((r[:, None] == c[None, :] % IDX_DIM) & (r[:, None] < IDX_DIM)).astype(np.float32)
    sel = np.zeros((A_HEADS, LANES, 2 * LANES), np.float32)
    for h in range(A_HEADS):
        for piece in range(3):
            sel[h, piece * A_HEADS + h, piece] = 1.0
            sel[h, piece * A_HEADS + h, LANES + 3 + piece] = -1.0
    one = np.zeros((1, 2 * LANES), np.float32)
    one[0, 3:6] = 1.0
    one[0, LANES:LANES + 3] = 1.0
    bf = lambda a: jnp.asarray(a, BF16)
    cum = (bf(tri128), bf(sel), jnp.asarray(one))
    return cum, bf(tri_t), bf(tile), _dilated_bias()


def _hybrid_layer(x2d, batch, seq, w, consts, final_g, final):
    cum_consts, tri_t, tile, bias_c = consts
    main = _rms_matmul(x2d, w["norm_g"], w["w_main"], BF16, tm=512, tn=MAIN_N)
    side = _rms_matmul(x2d, w["norm_g"], w["w_side"], F32, tm=1024, tn=SIDE_N)
    main3 = main.reshape(batch, seq, MAIN_N)
    side3 = side.reshape(batch, seq, SIDE_N)
    qf, kf = _cum_forget(side3, w["b_pad"], cum_consts, batch, seq)
    ya = _fox_attention(main3, qf, kf, batch, seq)
    yb = _dsa_attention(side3, main3, w, tile, tri_t, batch, seq)
    yc = _dilated_attention(side3, main3, bias_c, batch, seq)
    m = batch * seq
    return _merge(x2d, ya.reshape(m, A_W), yb.reshape(m, B_W), yc.reshape(m, C_W), main,
                  w["w_a"], w["w_b"], w["w_c"], w["w_o"], final_g, final)


def kernel(x, norm_g, w_in, b_forget, g_cq, w_uq, w_uq_idx, g_ckv, w_ukv, w_a, w_b, w_c, w_o, final_g):
    batch, seq, d_model = x.shape
    assert d_model == D_MODEL and seq % ATT_T == 0 and seq % FOX_T == 0
    depth = norm_g.shape[0]
    consts = _constants()
    fg = final_g.reshape(1, D_MODEL)
    x2d = x.reshape(batch * seq, D_MODEL)
    for l in range(depth):
        w = _layer_weights(norm_g[l], w_in[l], b_forget[l], g_cq[l], w_uq[l], w_uq_idx[l], g_ckv[l],
                           w_ukv[l], w_a[l], w_b[l], w_c[l], w_o[l])
        x2d = _hybrid_layer(x2d, batch, seq, w, consts, fg, final=(l == depth - 1))
    return x2d.reshape(batch, seq, D_MODEL)
```

```python
import functools

import numpy as np
import jax
import jax.numpy as jnp
from jax import lax
from jax.experimental import pallas as pl
from jax.experimental.pallas import tpu as pltpu

F32 = jnp.float32
BF16 = jnp.bfloat16
I32 = jnp.int32
I16 = jnp.int16

D_MODEL = 1024
HEAD_DIM = 64
EPS = 1e-6
A_HEADS = 8
A_W = A_HEADS * HEAD_DIM
B_HEADS = 8
B_W = B_HEADS * HEAD_DIM
B_Q_RANK = 256
B_KV_RANK = 128
IDX_HEADS = 8
IDX_DIM = 32
IDX_TOPK_MAX = 256
C_GROUPS = ((128, 1), (512, 4), (2048, 16))
C_HEADS_PER_GROUP = 4
C_HEADS = C_HEADS_PER_GROUP * len(C_GROUPS)
C_W = C_HEADS * HEAD_DIM
IN_WIDTHS = (A_W, A_W, A_W, A_HEADS, A_W,
             B_Q_RANK, B_KV_RANK, IDX_DIM, IDX_HEADS, B_W,
             C_W, C_W, C_W, C_W,
             D_MODEL, D_MODEL, D_MODEL)

LANES = 128
SUBLANES = 8
PACKED_ROWS = 16
SCALE = HEAD_DIM ** -0.5
LOG2E = float(np.log2(np.e))
ONES_ROWS = 16
NEG = -1e30
INT_MIN = -2 ** 31
INT16_MIN, INT16_MAX = -2 ** 15, 2 ** 15 - 1
CLASS_WINDOW = 128
assert all(w // d == CLASS_WINDOW for w, d in C_GROUPS)
ATT_T = 256
FOX_T = 512
POS_SPLIT = 64

MAIN_G = 0
MAIN_ZB = 3 * D_MODEL
MAIN_QA = MAIN_ZB + B_W
MAIN_KA = MAIN_QA + A_W
MAIN_VA = MAIN_KA + A_W
MAIN_ZA = MAIN_VA + A_W
MAIN_ZC = MAIN_ZA + A_W
MAIN_N = MAIN_ZC + C_W
SIDE_CQ = 0
SIDE_MISC = B_Q_RANK + B_KV_RANK
SIDE_WIDX = IDX_DIM
SIDE_FA = 512
SIDE_QC = SIDE_FA + LANES
SIDE_KC = SIDE_QC + C_W
SIDE_VC = SIDE_KC + C_W
SIDE_N = 3072

VMEM_LIMIT = 56 * 1024 * 1024


def _cparams(n_axes, vmem=VMEM_LIMIT):
    return pltpu.CompilerParams(dimension_semantics=("arbitrary",) * n_axes,
                                vmem_limit_bytes=vmem)


def _dot(a, b):
    return jnp.dot(a, b, preferred_element_type=F32)


def _dot_nt(a, b):
    return lax.dot_general(a, b, (((1,), (1,)), ((), ())), preferred_element_type=F32)


def _sigmoid(x):
    return 1.0 / (1.0 + jnp.exp(-x))


def _silu(x):
    return x * _sigmoid(x)


def _rms(x, g):
    return x * lax.rsqrt(jnp.mean(x * x, axis=-1, keepdims=True) + EPS) * g


def _split3(x):
    hi = x.astype(BF16)
    r1 = x - hi.astype(F32)
    mid = r1.astype(BF16)
    lo = (r1 - mid.astype(F32)).astype(BF16)
    return hi, mid, lo


def _fold8(x, op):
    n, t = x.shape
    return op(x.reshape(n // SUBLANES, SUBLANES, t), axis=0)


def _tree_fold(x, op, rows=SUBLANES):
    n, t = x.shape
    parts = [x[r:r + rows, :] for r in range(0, n, rows)]
    while len(parts) > 1:
        nxt = [op(parts[j], parts[j + 1]) for j in range(0, len(parts) - 1, 2)]
        if len(parts) % 2:
            nxt.append(parts[-1])
        parts = nxt
    return parts[0]


def _keys_max(x):
    return jnp.max(_fold8(x, jnp.max), axis=0, keepdims=True)


def _online_update_heads(states, s_ts, v_ts):
    ps, scaled = [], []
    for (m, acc), s_t in zip(states, s_ts):
        m_new = jnp.maximum(m, _keys_max(s_t))
        alpha = jnp.exp2(m - m_new)
        ps.append(jnp.exp2(s_t - m_new).astype(BF16))
        scaled.append((m_new, alpha * acc))
    return [(m, acc + _dot(v_t, p)) for (m, acc), v_t, p in zip(scaled, v_ts, ps)]


def _finish_pair(acc0, acc1):
    norm = lambda acc: acc[:HEAD_DIM, :] * (1.0 / acc[HEAD_DIM:HEAD_DIM + 1, :])
    return jnp.concatenate([norm(acc0), norm(acc1)], axis=0).T


def _rms_matmul_kernel(x_ref, g_ref, w_ref, o_ref, h_ref):
    @pl.when(pl.program_id(1) == 0)
    def _():
        h_ref[...] = _rms(x_ref[...], g_ref[...]).astype(BF16)

    o_ref[...] = _dot(h_ref[...], w_ref[...]).astype(o_ref.dtype)


def _rms_matmul(x2d, g, w, out_dtype, tm, tn):
    m, k = x2d.shape
    n = w.shape[1]
    return pl.pallas_call(
        _rms_matmul_kernel,
        grid=(m // tm, n // tn),
        in_specs=[pl.BlockSpec((tm, k), lambda i, j: (i, 0)),
                  pl.BlockSpec((1, k), lambda i, j: (0, 0)),
                  pl.BlockSpec((k, tn), lambda i, j: (0, j))],
        out_specs=pl.BlockSpec((tm, tn), lambda i, j: (i, j)),
        out_shape=jax.ShapeDtypeStruct((m, n), out_dtype),
        scratch_shapes=[pltpu.VMEM((tm, k), BF16)],
        compiler_params=_cparams(2),
        name="rms_in_proj",
    )(x2d, g, w)


def _cum_kernel(fa_ref, b_ref, tri_ref, sel_ref, one_ref, qf_ref, kf_ref, c3_s):
    seq = fa_ref.shape[1]
    tri = tri_ref[...]
    lane = lax.broadcasted_iota(I32, (1, LANES), 1)
    carry = jnp.zeros((1, LANES), F32)
    for blk in range(seq // LANES):
        rows = slice(blk * LANES, (blk + 1) * LANES)
        x = fa_ref[0, rows, :] + b_ref[...]
        lf = jnp.minimum(x, 0.0) - jnp.log(1.0 + jnp.exp(-jnp.abs(x)))
        hi, mid, lo = _split3(lf)
        c = _dot(tri, hi) + _dot(tri, mid) + _dot(tri, lo) + carry
        carry = c[LANES - 1:LANES, :]
        hi, mid, lo = _split3(jnp.where(lane < A_HEADS, c * LOG2E, 0.0))
        c3_s[rows, :] = (hi.astype(F32) + pltpu.roll(mid.astype(F32), A_HEADS, 1)
                         + pltpu.roll(lo.astype(F32), 2 * A_HEADS, 1)).astype(BF16)
    half = seq // 2
    for h in range(A_HEADS):
        for rows in (slice(0, half), slice(half, seq)):
            feats = _dot(c3_s[rows, :], sel_ref[h]) + one_ref[...]
            qf_ref[0, h, rows, :] = feats[:, :LANES].astype(BF16)
            kf_ref[0, h, rows, :] = feats[:, LANES:].astype(BF16)


def _cum_forget(side, b_pad, consts, batch, seq):
    tri, sel, one = consts
    full = lambda a: pl.BlockSpec(a.shape, lambda b: (0,) * a.ndim)
    feat = jax.ShapeDtypeStruct((batch, A_HEADS, seq, LANES), BF16)
    return pl.pallas_call(
        _cum_kernel,
        grid=(batch,),
        in_specs=[pl.BlockSpec((1, seq, LANES), lambda b: (b, 0, SIDE_FA // LANES)),
                  full(b_pad), full(tri), full(sel), full(one)],
        out_specs=[pl.BlockSpec((1, A_HEADS, seq, LANES), lambda b: (b, 0, 0, 0))] * 2,
        out_shape=[feat, feat],
        scratch_shapes=[pltpu.VMEM((seq, LANES), BF16)],
        compiler_params=_cparams(1),
        name="fox_cumsum",
    )(side, b_pad, tri, sel, one)


def _fox_kernel(q_ref, k_ref, v_ref, z_ref, qf_ref, kf_ref, o_ref, vt_s, qa_s, m_s, acc_s):
    seq = q_ref.shape[1]
    t = FOX_T
    n_heads = A_HEADS
    lane = lax.broadcasted_iota(I32, (1, LANES), 1)
    key_le_query = (lax.broadcasted_iota(I32, (t, t), 0) <= lax.broadcasted_iota(I32, (t, t), 1))
    pair_cols = lambda h: slice((h // 2) * LANES, (h // 2 + 1) * LANES)

    def transpose_v(c, _):
        cols = pl.ds(pl.multiple_of(c * t, t), t)
        for p in range(n_heads // 2):
            pc = slice(p * LANES, (p + 1) * LANES)
            v_t = v_ref[0, cols, pc].astype(F32).T.astype(BF16)
            for e in range(2):
                vt_s[2 * p + e, :HEAD_DIM, cols] = v_t[e * HEAD_DIM:(e + 1) * HEAD_DIM, :]
                vt_s[2 * p + e, HEAD_DIM:, cols] = jnp.ones((ONES_ROWS, t), BF16)
        return 0

    lax.fori_loop(0, seq // t, transpose_v, 0)

    def q_block(i, _):
        rows = pl.ds(pl.multiple_of(i * t, t), t)
        for h in range(n_heads):
            half = (lane >= HEAD_DIM) if h % 2 else (lane < HEAD_DIM)
            q2 = q_ref[0, rows, pair_cols(h)]
            qa_s[h] = jnp.concatenate([jnp.where(half, q2, jnp.zeros_like(q2)), qf_ref[0, h, rows, :]], axis=1)
            m_s[h] = jnp.full((1, t), NEG, F32)
            acc_s[h] = jnp.zeros((HEAD_DIM + ONES_ROWS, t), F32)

        def chunk(kb, diagonal):
            cols = pl.ds(pl.multiple_of(kb * t, t), t)
            states = [(m_s[h], acc_s[h]) for h in range(n_heads)]
            s_ts = [_dot_nt(jnp.concatenate([k_ref[0, cols, pair_cols(h)], kf_ref[0, h, cols, :]], axis=1),
                            qa_s[h]) for h in range(n_heads)]
            if diagonal:
                s_ts = [jnp.where(key_le_query, s_t, NEG) for s_t in s_ts]
            v_ts = [vt_s[h, :, cols] for h in range(n_heads)]
            for h, st in enumerate(_online_update_heads(states, s_ts, v_ts)):
                m_s[h], acc_s[h] = st

        def off_diagonal(kb, _):
            chunk(kb, False)
            return 0

        lax.fori_loop(0, i, off_diagonal, 0)
        chunk(i, True)
        for p in range(n_heads // 2):
            h0, h1 = 2 * p, 2 * p + 1
            out = _finish_pair(acc_s[h0], acc_s[h1]) * _silu(z_ref[0, rows, pair_cols(h0)].astype(F32))
            o_ref[0, rows, pair_cols(h0)] = out.astype(o_ref.dtype)
        return 0

    lax.fori_loop(0, seq // t, q_block, 0)


def _fox_attention(main, qf, kf, batch, seq):
    t = FOX_T
    blk = lambda base: pl.BlockSpec((1, seq, A_W), lambda b, base=base: (b, 0, base // A_W))
    feat = pl.BlockSpec((1, A_HEADS, seq, LANES), lambda b: (b, 0, 0, 0))
    return pl.pallas_call(
        _fox_kernel,
        grid=(batch,),
        in_specs=[blk(MAIN_QA), blk(MAIN_KA), blk(MAIN_VA), blk(MAIN_ZA), feat, feat],
        out_specs=pl.BlockSpec((1, seq, A_W), lambda b: (b, 0, 0)),
        out_shape=jax.ShapeDtypeStruct((batch, seq, A_W), BF16),
        scratch_shapes=[pltpu.VMEM((A_HEADS, HEAD_DIM + ONES_ROWS, seq), BF16),
                        pltpu.VMEM((A_HEADS, t, 2 * LANES), BF16),
                        pltpu.VMEM((A_HEADS, 1, t), F32),
                        pltpu.VMEM((A_HEADS, HEAD_DIM + ONES_ROWS, t), F32)],
        compiler_params=_cparams(1),
        name="fox_attention",
    )(main, main, main, main, qf, kf)


def _lane_group(lane, first):
    rel = lane - first
    return jnp.where((rel >= 0) & (rel < 9), rel // 3, -1)


def _key_pos_features(pos, lane, first):
    hi = lax.shift_right_logical(pos, 6).astype(F32)
    lo = (pos & (POS_SPLIT - 1)).astype(F32)
    grp = _lane_group(lane, first)
    return jnp.where(grp == 0, hi, jnp.where(grp == 1, lo, jnp.where(grp == 2, 1.0, 0.0)))


def _query_pos_features(pos, lane, first, slope):
    grp = _lane_group(lane, first)
    x = jnp.where(grp == 0, POS_SPLIT * slope,
                  jnp.where(grp == 1, slope, jnp.where(grp == 2, -slope * pos.astype(F32), 0.0)))
    hi, mid, lo = _split3(x)
    piece = (lane - first) % 3
    return jnp.where(piece == 0, hi, jnp.where(piece == 1, mid, lo))


def _dsa_kernel(side_ref, z_ref, gcq_ref, wuq_ref, wuqi_ref, gckv_ref, wkv_ref, wvk_ref, wvt_ref,
                tile_ref, tri_ref, qfeat_ref, o_ref,
                qb_s, qi_s, kt_s, ka_s, kb_s, vt_s, wt_s, qm_s, qa_s, key_s, hi_s, lo_s, mb_s, m_s, acc_s,
                *, top_k):
    seq = side_ref.shape[1]
    t = ATT_T
    n_heads = B_HEADS
    lane = lax.broadcasted_iota(I32, (1, LANES), 1)
    lane2 = lax.broadcasted_iota(I32, (1, 2 * LANES), 1)
    key_le_query = (lax.broadcasted_iota(I32, (t, t), 0) <= lax.broadcasted_iota(I32, (t, t), 1))
    row_iota = lax.broadcasted_iota(I32, (t, 1), 0)

    def prep(c, _):
        r0 = pl.multiple_of(c * t, t)
        rows = pl.ds(r0, t)
        blk = side_ref[0, rows, :]
        cqn = _rms(blk[:, SIDE_CQ:SIDE_CQ + B_Q_RANK], gcq_ref[...]).astype(BF16)
        qb_s[rows, :] = _dot(cqn, wuq_ref[...]).astype(BF16)
        qi_s[rows, :] = _dot(cqn, wuqi_ref[...]).astype(BF16)
        kvn = _rms(blk[:, B_Q_RANK:SIDE_MISC], gckv_ref[...]).astype(BF16)
        pos = r0 + row_iota
        ka_s[rows, :] = jnp.where(lane < HEAD_DIM, _dot(kvn, wkv_ref[...]),
                                  _key_pos_features(pos, lane, HEAD_DIM)).astype(BF16)
        kb_s[rows, :] = jnp.where(lane >= HEAD_DIM, _dot(kvn, wvk_ref[...]),
                                  _key_pos_features(pos, lane, 0)).astype(BF16)
        vt_s[:HEAD_DIM, rows] = _dot_nt(wvt_ref[...], kvn).astype(BF16)
        vt_s[HEAD_DIM:, rows] = jnp.ones((ONES_ROWS, t), BF16)
        misc = blk[:, SIDE_MISC:]
        kt_s[rows, :] = _dot(misc.astype(BF16), tile_ref[...]).astype(BF16)
        wt_s[:, rows] = misc.T
        return 0

    lax.fori_loop(0, seq // t, prep, 0)

    def q_block(i, _):
        r0 = pl.multiple_of(i * t, t)
        rows = pl.ds(r0, t)
        nk = i + 1

        qi = qi_s[rows, :]
        for h in range(IDX_HEADS):
            in_head = (lane2 >= h * IDX_DIM) & (lane2 < (h + 1) * IDX_DIM)
            qm_s[h] = jnp.where(in_head, qi, jnp.zeros_like(qi))
        w_rows = [wt_s[SIDE_WIDX + h:SIDE_WIDX + h + 1, rows] for h in range(IDX_HEADS)]

        def score_chunk(kc, diagonal):
            c0 = pl.multiple_of(kc * t, t)
            kt = kt_s[pl.ds(c0, t), :]
            acc = jnp.zeros((t, t), F32)
            for h in range(IDX_HEADS):
                acc = acc + w_rows[h] * jnp.maximum(_dot_nt(kt, qm_s[h]), 0.0)
            bits = lax.bitcast_convert_type(acc, I32)
            key = bits ^ (lax.shift_right_arithmetic(bits, 31) & 0x7FFFFFFF)
            key = jnp.where(key == -1, 0, key)
            if diagonal:
                key = jnp.where(key_le_query, key, INT_MIN)
            key_s[pl.ds(c0, t), :] = key
            hi_s[pl.ds(c0, t), :] = lax.shift_right_arithmetic(key, 16).astype(I16)

        def score_quad(kq, _):
            for j in range(4):
                score_chunk(4 * kq + j, False)
            return 0

        lax.fori_loop(0, i // 4, score_quad, 0)

        @pl.when(i % 4 >= 2)
        def _():
            score_chunk((i // 4) * 4, False)
            score_chunk((i // 4) * 4 + 1, False)

        @pl.when(i % 2 == 1)
        def _():
            score_chunk(i - 1, False)

        score_chunk(i, True)

        def count(pred):
            def body(kc, acc):
                c0 = pl.multiple_of(kc * t, t)
                return acc + _tree_fold(jnp.where(pred(key_s[pl.ds(c0, t), :]), 1.0, 0.0), jnp.add)
            acc = lax.fori_loop(0, nk, body, jnp.zeros((SUBLANES, t), F32))
            return jnp.sum(acc, axis=0, keepdims=True)

        def count16(src, cand):
            def body(kc, acc):
                c0 = pl.multiple_of(kc * t, t)
                hit = jnp.where(src[pl.ds(c0, t), :] >= cand, jnp.ones((), BF16), jnp.zeros((), BF16))
                return acc + _tree_fold(hit, jnp.add, PACKED_ROWS)
            acc = lax.fori_loop(0, nk, body, jnp.zeros((PACKED_ROWS, t), BF16))
            return jnp.sum(acc.astype(F32), axis=0, keepdims=True)

        def search16(src):
            def step(p, thr):
                cand = thr + lax.shift_left(jnp.int32(1), 15 - p)
                return jnp.where(count16(src, cand.astype(I16)) >= top_k, cand, thr)
            return lax.fori_loop(0, 16, step, jnp.full((1, t), INT16_MIN, I32))

        t_hi = search16(hi_s)

        def low_body(kc, _):
            c0 = pl.multiple_of(kc * t, t)
            key = key_s[pl.ds(c0, t), :]
            hi = lax.shift_right_arithmetic(key, 16)
            lo = (key & 0xFFFF) + INT16_MIN
            lo_s[pl.ds(c0, t), :] = jnp.where(hi > t_hi, INT16_MAX, jnp.where(hi == t_hi, lo, INT16_MIN)).astype(I16)
            return 0

        lax.fori_loop(0, nk, low_body, 0)
        t_lo = search16(lo_s)
        thr = lax.shift_left(t_hi, 16) | ((t_lo - INT16_MIN) & 0xFFFF)
        thr = jnp.maximum(thr, INT_MIN + 1)

        def mask_body(kc, acc):
            c0 = pl.multiple_of(kc * t, t)
            admitted = key_s[pl.ds(c0, t), :] >= thr
            mb_s[pl.ds(c0, t), :] = jnp.where(admitted, 0.0, NEG)
            return acc + _tree_fold(jnp.where(admitted, 1.0, 0.0), jnp.add)

        n_ge = jnp.sum(lax.fori_loop(0, nk, mask_body, jnp.zeros((SUBLANES, t), F32)), axis=0, keepdims=True)

        @pl.when(jnp.max(n_ge) > top_k)
        def _():
            room = top_k - count(lambda k: k > thr)

            def tie_body(kc, seen):
                c0 = pl.multiple_of(kc * t, t)
                tie = key_s[pl.ds(c0, t), :] == thr
                rank = seen + _dot(tri_ref[...], jnp.where(tie, 1.0, 0.0).astype(BF16))
                drop = tie & (rank > room)
                mb_s[pl.ds(c0, t), :] = jnp.where(drop, NEG, mb_s[pl.ds(c0, t), :])
                return rank[t - 1:t, :]

            lax.fori_loop(0, nk, tie_body, jnp.zeros((1, t), F32))

        for h in range(n_heads):
            half = (lane >= HEAD_DIM) if h % 2 else (lane < HEAD_DIM)
            q2 = qb_s[rows, (h // 2) * LANES:(h // 2 + 1) * LANES]
            qa_s[h] = jnp.where(half, q2, qfeat_ref[h, rows, :])
            m_s[h] = jnp.full((1, t), NEG, F32)
            acc_s[h] = jnp.zeros((HEAD_DIM + ONES_ROWS, t), F32)

        def attend(kc, _):
            c0 = pl.multiple_of(kc * t, t)
            cols = pl.ds(c0, t)
            states = [(m_s[h], acc_s[h]) for h in range(n_heads)]
            mask = mb_s[cols, :]
            k_ops = (ka_s[cols, :], kb_s[cols, :])
            s_ts = [_dot_nt(k_ops[h % 2], qa_s[h]) + mask for h in range(n_heads)]
            v_list = [vt_s[:, cols]] * n_heads
            for h, st in enumerate(_online_update_heads(states, s_ts, v_list)):
                m_s[h], acc_s[h] = st
            return 0

        def attend_pair(kp, _):
            attend(2 * kp, 0)
            attend(2 * kp + 1, 0)
            return 0

        lax.fori_loop(0, nk // 2, attend_pair, 0)

        @pl.when(nk % 2 == 1)
        def _():
            attend(nk - 1, 0)

        for hp in range(n_heads // 2):
            cols = slice(hp * LANES, (hp + 1) * LANES)
            out = _finish_pair(acc_s[2 * hp], acc_s[2 * hp + 1]) * _silu(z_ref[0, rows, cols].astype(F32))
            o_ref[0, rows, cols] = out.astype(o_ref.dtype)
        return 0

    lax.fori_loop(0, seq // t, q_block, 0)


def _dsa_attention(side, main, w, tile, tri, batch, seq):
    assert seq // POS_SPLIT <= 256
    assert seq // PACKED_ROWS <= 256
    top_k = min(IDX_TOPK_MAX, seq // 4)
    slopes = tuple(float(2.0 ** (-8.0 * (h + 1) / B_HEADS)) * LOG2E for h in range(B_HEADS))
    t = ATT_T
    const = lambda a: pl.BlockSpec(a.shape, lambda b: (0,) * a.ndim)
    pos = jnp.arange(seq, dtype=I32).reshape(seq, 1)
    lane = jnp.arange(LANES, dtype=I32).reshape(1, LANES)
    qfeat = jnp.stack([_query_pos_features(pos, lane, 0 if h % 2 else HEAD_DIM, slopes[h])
                       for h in range(B_HEADS)]).astype(BF16)
    weights = [w["g_cq"], w["w_uq"], w["w_uq_idx"], w["g_ckv"], w["w_kv"], w["w_vk"], w["w_v_t"], tile, tri, qfeat]
    return pl.pallas_call(
        functools.partial(_dsa_kernel, top_k=top_k),
        grid=(batch,),
        in_specs=[pl.BlockSpec((1, seq, 512), lambda b: (b, 0, 0)),
                  pl.BlockSpec((1, seq, B_W), lambda b: (b, 0, MAIN_ZB // B_W))]
                 + [const(a) for a in weights],
        out_specs=pl.BlockSpec((1, seq, B_W), lambda b: (b, 0, 0)),
        out_shape=jax.ShapeDtypeStruct((batch, seq, B_W), BF16),
        scratch_shapes=[pltpu.VMEM((seq, B_W), BF16),
                        pltpu.VMEM((seq, IDX_HEADS * IDX_DIM), BF16),
                        pltpu.VMEM((seq, IDX_HEADS * IDX_DIM), BF16),
                        pltpu.VMEM((seq, LANES), BF16),
                        pltpu.VMEM((seq, LANES), BF16),
                        pltpu.VMEM((HEAD_DIM + ONES_ROWS, seq), BF16),
                        pltpu.VMEM((LANES, seq), F32),
                        pltpu.VMEM((IDX_HEADS, t, IDX_HEADS * IDX_DIM), BF16),
                        pltpu.VMEM((B_HEADS, t, LANES), BF16),
                        pltpu.VMEM((seq, t), I32),
                        pltpu.VMEM((seq, t), I16),
                        pltpu.VMEM((seq, t), I16),
                        pltpu.VMEM((seq, t), F32),
                        pltpu.VMEM((B_HEADS, 1, t), F32),
                        pltpu.VMEM((B_HEADS, HEAD_DIM + ONES_ROWS, t), F32)],
        compiler_params=_cparams(1),
        name="dsa_attention",
    )(side, main, *weights)


C_UNROLL = 8


def _dilated_kernel(q_ref, k_ref, v_ref, bias_ref, z0_ref, z1_ref, z2_ref, o_ref, o_s, l_s):
    seq = q_ref.shape[1]
    sp = pl.program_id(1)
    grp = pl.program_id(2)
    lane = lax.broadcasted_iota(I32, (1, LANES), 1)
    cw = CLASS_WINDOW

    def rows(start, d):
        return pl.ds(start, cw) if d == 1 else pl.ds(start, cw, stride=d)

    def group_body(g, d):
        per_class = seq // (d * cw)

        ld = lambda ref, s0: ref[0, rows(s0, d), :].astype(BF16)

        def load_block(n, u, before):
            r = n // per_class
            ib = n % per_class
            start = r + ib * (cw * d)
            b = dict(start=start, q=ld(q_ref, start), kc=ld(k_ref, start), vc=ld(v_ref, start), prev=None)
            if per_class <= C_UNROLL:
                if u % per_class:
                    b.update(prev="static", kp=before["kc"], vp=before["vc"])
            elif u:
                b.update(prev="static", kp=before["kc"], vp=before["vc"])
            else:
                pstart = r + jnp.maximum(ib - 1, 0) * (cw * d)
                b.update(prev="dynamic", kp=ld(k_ref, pstart), vp=ld(v_ref, pstart), has_prev=ib > 0)
            return b

        def blk(it, _):
            blocks = []
            for u in range(C_UNROLL):
                blocks.append(load_block(it * C_UNROLL + u, u, blocks[-1] if blocks else None))
            logits = []
            for b in blocks:
                for e in range(2):
                    half = (lane >= HEAD_DIM) if e else (lane < HEAD_DIM)
                    qm = jnp.where(half, b["q"], jnp.zeros_like(b["q"]))
                    s_cur = _dot_nt(qm, b["kc"]) + bias_ref[0, e, :, cw:]
                    s_prev = None
                    if b["prev"] == "static":
                        s_prev = _dot_nt(qm, b["kp"]) + bias_ref[0, e, :, :cw]
                    elif b["prev"] == "dynamic":
                        s_prev = _dot_nt(qm, b["kp"]) + jnp.where(b["has_prev"], bias_ref[0, e, :, :cw], NEG)
                    logits.append((s_cur, s_prev))
            probs = []
            for s_cur, s_prev in logits:
                if s_prev is None:
                    m = jnp.max(s_cur, axis=-1, keepdims=True)
                    p_cur, p_prev = jnp.exp(s_cur - m), None
                    l = jnp.sum(p_cur, axis=-1, keepdims=True)
                else:
                    m = jnp.max(jnp.maximum(s_cur, s_prev), axis=-1, keepdims=True)
                    p_cur, p_prev = jnp.exp(s_cur - m), jnp.exp(s_prev - m)
                    l = jnp.sum(p_cur + p_prev, axis=-1, keepdims=True)
                    p_prev = p_prev.astype(BF16)
                probs.append((p_cur.astype(BF16), p_prev, 1.0 / l, m + jnp.log(l)))
            for u, b in enumerate(blocks):
                outs = []
                for e in range(2):
                    p_cur, p_prev, inv_l, _ = probs[2 * u + e]
                    o = _dot(p_cur, b["vc"])
                    if p_prev is not None:
                        o = o + _dot(p_prev, b["vp"])
                    outs.append(o * inv_l)
                o_s[g, rows(b["start"], d), :] = jnp.where(lane < HEAD_DIM, outs[0], outs[1])
                l_s[g, rows(b["start"], d), :] = jnp.where(lane < HEAD_DIM, probs[2 * u][3], probs[2 * u + 1][3])
            return 0

        lax.fori_loop(0, seq // (cw * C_UNROLL), blk, 0)

    for g, (_, d) in enumerate(C_GROUPS):
        pl.when(grp == g)(functools.partial(group_body, g, d))

    n_grp = len(C_GROUPS)
    z_refs = (z0_ref, z1_ref, z2_ref)

    def combine(sp_static):
        def chunk(c, _):
            r0 = pl.multiple_of(c * 256, 256)
            ls = [l_s[g, pl.ds(r0, 256), :] for g in range(n_grp)]
            m = functools.reduce(jnp.maximum, ls)
            ws = [jnp.exp(x - m) for x in ls]
            inv = 1.0 / functools.reduce(lambda a, b: a + b, ws)
            for g in range(n_grp):
                col = (g * 2 + sp_static) * LANES
                y = (ws[g] * inv) * o_s[g, pl.ds(r0, 256), :]
                y = y * _silu(z_refs[g][0, pl.ds(r0, 256), :].astype(F32))
                o_ref[0, pl.ds(r0, 256), col:col + LANES] = y.astype(o_ref.dtype)
            return 0

        lax.fori_loop(0, seq // 256, chunk, 0)

    for s in range(2):
        pl.when((grp == n_grp - 1) & (sp == s))(functools.partial(combine, s))


def _dilated_attention(side, main, bias, batch, seq):
    assert seq % (C_GROUPS[-1][1] * CLASS_WINDOW) == 0
    assert (seq // CLASS_WINDOW) % C_UNROLL == 0
    n_grp = len(C_GROUPS)
    blk = lambda base: pl.BlockSpec(
        (1, seq, LANES), lambda b, s, g, base=base: (b, 0, base // LANES + 2 * g + s))
    zblk = lambda g: pl.BlockSpec(
        (1, seq, LANES), lambda b, s, _, g=g: (b, 0, MAIN_ZC // LANES + 2 * g + s))
    return pl.pallas_call(
        _dilated_kernel,
        grid=(batch, 2, n_grp),
        in_specs=[blk(SIDE_QC), blk(SIDE_KC), blk(SIDE_VC),
                  pl.BlockSpec((1, 2, CLASS_WINDOW, 2 * CLASS_WINDOW), lambda b, s, g: (2 * g + s, 0, 0, 0)),
                  zblk(0), zblk(1), zblk(2)],
        out_specs=pl.BlockSpec((1, seq, C_W), lambda b, s, g: (b, 0, 0)),
        out_shape=jax.ShapeDtypeStruct((batch, seq, C_W), BF16),
        scratch_shapes=[pltpu.VMEM((n_grp, seq, LANES), F32),
                        pltpu.VMEM((n_grp, seq, LANES), F32)],
        compiler_params=_cparams(3),
        name="dilated_attention",
    )(side, side, side, bias, main, main, main)


def _dilated_bias():
    cw = CLASS_WINDOW
    slopes = np.exp2(-8.0 * (np.arange(C_HEADS, dtype=np.float32) + 1.0) / C_HEADS).astype(np.float32)
    i = np.arange(cw)[:, None]
    j = np.arange(cw)[None, :]
    out = np.empty((C_HEADS // 2, 2, cw, 2 * cw), np.float32)
    for h in range(C_HEADS):
        d = C_GROUPS[h // C_HEADS_PER_GROUP][1]
        prev = np.where(j >= i, -slopes[h] * np.float32(d) * (cw + i - j).astype(np.float32), NEG)
        cur = np.where(j <= i, -slopes[h] * np.float32(d) * (i - j).astype(np.float32), NEG)
        out[h // 2, h % 2] = np.concatenate([prev, cur], axis=1)
    return jnp.asarray(out)


def _merge_kernel(x_ref, ya_ref, yb_ref, yc_ref, g_ref, wa_ref, wb_ref, wc_ref, wo_ref, fg_ref, o_ref,
                  *, final):
    gate = lambda n: _sigmoid(g_ref[:, n * D_MODEL:(n + 1) * D_MODEL].astype(F32))
    merged = (gate(0) * _dot(ya_ref[...], wa_ref[...])
              + gate(1) * _dot(yb_ref[...], wb_ref[...])
              + gate(2) * _dot(yc_ref[...], wc_ref[...]))
    y = x_ref[...] + _dot(merged.astype(BF16), wo_ref[...])
    if final:
        y = _rms(y, fg_ref[...])
    o_ref[...] = y


def _merge(x2d, ya, yb, yc, main, w_a, w_b, w_c, w_o, final_g, final, tm=1024):
    m = x2d.shape[0]
    row = lambda w: pl.BlockSpec((tm, w), lambda i: (i, 0))
    const = lambda a: pl.BlockSpec(a.shape, lambda i: (0, 0))
    return pl.pallas_call(
        functools.partial(_merge_kernel, final=final),
        grid=(m // tm,),
        in_specs=[row(D_MODEL), row(A_W), row(B_W), row(C_W), row(3 * D_MODEL),
                  const(w_a), const(w_b), const(w_c), const(w_o), const(final_g)],
        out_specs=row(D_MODEL),
        out_shape=jax.ShapeDtypeStruct((m, D_MODEL), F32),
        compiler_params=_cparams(1),
        name="gated_merge",
    )(x2d, ya, yb, yc, main, w_a, w_b, w_c, w_o, final_g)


def _layer_weights(norm_g, w_in, b_forget, g_cq, w_uq, w_uq_idx, g_ckv, w_ukv, w_a, w_b, w_c, w_o):
    points = np.cumsum(IN_WIDTHS)[:-1].tolist()
    (wqa, wka, wva, wfa, wza, wcq, wckv, wkidx, wwidx, wzb,
     wqc, wkc, wvc, wzc, wga, wgb, wgc) = jnp.split(w_in, points, axis=1)
    zeros = lambda n: jnp.zeros((D_MODEL, n), F32)
    w_main = jnp.concatenate([wga, wgb, wgc, wzb, wqa * (SCALE * LOG2E), wka, wva, wza, wzc], axis=1)
    w_side = jnp.concatenate(
        [wcq, wckv, wkidx, wwidx, zeros(LANES - IDX_DIM - IDX_HEADS),
         wfa, zeros(LANES - A_HEADS), wqc * SCALE, wkc, wvc, zeros(SIDE_N - SIDE_VC - C_W)], axis=1)
    assert w_main.shape[1] == MAIN_N and w_side.shape[1] == SIDE_N
    wk, wv = w_ukv[:, :HEAD_DIM], w_ukv[:, HEAD_DIM:]
    w_kv = jnp.concatenate([wk, wv], axis=1).astype(BF16)
    w_vk = jnp.concatenate([wv, wk], axis=1).astype(BF16)
    return dict(
        norm_g=norm_g.reshape(1, D_MODEL),
        w_main=w_main.astype(BF16), w_side=w_side.astype(BF16),
        b_pad=jnp.zeros((1, LANES), F32).at[0, :A_HEADS].set(b_forget),
        g_cq=g_cq.reshape(1, B_Q_RANK), w_uq=(w_uq * (SCALE * LOG2E)).astype(BF16), w_uq_idx=w_uq_idx.astype(BF16),
        g_ckv=g_ckv.reshape(1, B_KV_RANK), w_kv=w_kv, w_vk=w_vk, w_v_t=wv.T.astype(BF16),
        w_a=w_a.astype(BF16), w_b=w_b.astype(BF16), w_c=w_c.astype(BF16), w_o=w_o.astype(BF16))


def _constants():
    r = np.arange(LANES)
    tri128 = (r[None, :] <= r[:, None]).astype(np.float32)
    r2 = np.arange(ATT_T)
    tri_t = (r2[None, :] <= r2[:, None]).astype(np.float32)
    c = np.arange(IDX_HEADS * IDX_DIM)
    tile = ((r[:, None] == c[None, :] % IDX_DIM) & (r[:, None] < IDX_DIM)).astype(np.float32)
    sel = np.zeros((A_HEADS, LANES, 2 * LANES), np.float32)
    for h in range(A_HEADS):
        for piece in range(3):
            sel[h, piece * A_HEADS + h, piece] = 1.0
            sel[h, piece * A_HEADS + h, LANES + 3 + piece] = -1.0
    one = np.zeros((1, 2 * LANES), np.float32)
    one[0, 3:6] = 1.0
    one[0, LANES:LANES + 3] = 1.0
    bf = lambda a: jnp.asarray(a, BF16)
    cum = (bf(tri128), bf(sel), jnp.asarray(one))
    return cum, bf(tri_t), bf(tile), _dilated_bias()


def _hybrid_layer(x2d, batch, seq, w, consts, final_g, final):
    cum_consts, tri_t, tile, bias_c = consts
    main = _rms_matmul(x2d, w["norm_g"], w["w_main"], BF16, tm=512, tn=MAIN_N)
    side = _rms_matmul(x2d, w["norm_g"], w["w_side"], F32, tm=1024, tn=SIDE_N)
    main3 = main.reshape(batch, seq, MAIN_N)
    side3 = side.reshape(batch, seq, SIDE_N)
    qf, kf = _cum_forget(side3, w["b_pad"], cum_consts, batch, seq)
    ya = _fox_attention(main3, qf, kf, batch, seq)
    yb = _dsa_attention(side3, main3, w, tile, tri_t, batch, seq)
    yc = _dilated_attention(side3, main3, bias_c, batch, seq)
    m = batch * seq
    return _merge(x2d, ya.reshape(m, A_W), yb.reshape(m, B_W), yc.reshape(m, C_W), main,
                  w["w_a"], w["w_b"], w["w_c"], w["w_o"], final_g, final)


def kernel(x, norm_g, w_in, b_forget, g_cq, w_uq, w_uq_idx, g_ckv, w_ukv, w_a, w_b, w_c, w_o, final_g):
    batch, seq, d_model = x.shape
    assert d_model == D_MODEL and seq % ATT_T == 0 and seq % FOX_T == 0
    depth = norm_g.shape[0]
    consts = _constants()
    fg = final_g.reshape(1, D_MODEL)
    x2d = x.reshape(batch * seq, D_MODEL)
    for l in range(depth):
        w = _layer_weights(norm_g[l], w_in[l], b_forget[l], g_cq[l], w_uq[l], w_uq_idx[l], g_ckv[l],
                           w_ukv[l], w_a[l], w_b[l], w_c[l], w_o[l])
        x2d = _hybrid_layer(x2d, batch, seq, w, consts, fg, final=(l == depth - 1))
    return x2d.reshape(batch, seq, D_MODEL)
```

```python
import functools

import numpy as np
import jax
import jax.numpy as jnp
from jax import lax
from jax.experimental import pallas as pl
from jax.experimental.pallas import tpu as pltpu

F32 = jnp.float32
BF16 = jnp.bfloat16
I32 = jnp.int32
I16 = jnp.int16

D_MODEL = 1024
HEAD_DIM = 64
EPS = 1e-6
A_HEADS = 8
A_W = A_HEADS * HEAD_DIM
B_HEADS = 8
B_W = B_HEADS * HEAD_DIM
B_Q_RANK = 256
B_KV_RANK = 128
IDX_HEADS = 8
IDX_DIM = 32
IDX_TOPK_MAX = 256
C_GROUPS = ((128, 1), (512, 4), (2048, 16))
C_HEADS_PER_GROUP = 4
C_HEADS = C_HEADS_PER_GROUP * len(C_GROUPS)
C_W = C_HEADS * HEAD_DIM
IN_WIDTHS = (A_W, A_W, A_W, A_HEADS, A_W,
             B_Q_RANK, B_KV_RANK, IDX_DIM, IDX_HEADS, B_W,
             C_W, C_W, C_W, C_W,
             D_MODEL, D_MODEL, D_MODEL)

LANES = 128
SUBLANES = 8
PACKED_ROWS = 16
SCALE = HEAD_DIM ** -0.5
LOG2E = float(np.log2(np.e))
ONES_ROWS = 16
NEG = -1e30
INT_MIN = -2 ** 31
INT16_MIN, INT16_MAX = -2 ** 15, 2 ** 15 - 1
CLASS_WINDOW = 128
assert all(w // d == CLASS_WINDOW for w, d in C_GROUPS)
ATT_T = 256
FOX_T = 512
POS_SPLIT = 64

MAIN_G = 0
MAIN_ZB = 3 * D_MODEL
MAIN_QA = MAIN_ZB + B_W
MAIN_KA = MAIN_QA + A_W
MAIN_VA = MAIN_KA + A_W
MAIN_ZA = MAIN_VA + A_W
MAIN_ZC = MAIN_ZA + A_W
MAIN_N = MAIN_ZC + C_W
SIDE_CQ = 0
SIDE_MISC = B_Q_RANK + B_KV_RANK
SIDE_WIDX = IDX_DIM
SIDE_FA = 512
SIDE_QC = SIDE_FA + LANES
SIDE_KC = SIDE_QC + C_W
SIDE_VC = SIDE_KC + C_W
SIDE_N = 3072

VMEM_LIMIT = 56 * 1024 * 1024


def _cparams(n_axes, vmem=VMEM_LIMIT):
    return pltpu.CompilerParams(dimension_semantics=("arbitrary",) * n_axes,
                                vmem_limit_bytes=vmem)


def _dot(a, b):
    return jnp.dot(a, b, preferred_element_type=F32)


def _dot_nt(a, b):
    return lax.dot_general(a, b, (((1,), (1,)), ((), ())), preferred_element_type=F32)


def _sigmoid(x):
    return 1.0 / (1.0 + jnp.exp(-x))


def _silu(x):
    return x * _sigmoid(x)


def _rms(x, g):
    return x * lax.rsqrt(jnp.mean(x * x, axis=-1, keepdims=True) + EPS) * g


def _split3(x):
    hi = x.astype(BF16)
    r1 = x - hi.astype(F32)
    mid = r1.astype(BF16)
    lo = (r1 - mid.astype(F32)).astype(BF16)
    return hi, mid, lo


def _fold8(x, op):
    n, t = x.shape
    return op(x.reshape(n // SUBLANES, SUBLANES, t), axis=0)


def _tree_fold(x, op, rows=SUBLANES):
    n, t = x.shape
    parts = [x[r:r + rows, :] for r in range(0, n, rows)]
    while len(parts) > 1:
        nxt = [op(parts[j], parts[j + 1]) for j in range(0, len(parts) - 1, 2)]
        if len(parts) % 2:
            nxt.append(parts[-1])
        parts = nxt
    return parts[0]


def _keys_max(x):
    return jnp.max(_fold8(x, jnp.max), axis=0, keepdims=True)


def _online_update_heads(states, s_ts, v_ts):
    ps, scaled = [], []
    for (m, acc), s_t in zip(states, s_ts):
        m_new = jnp.maximum(m, _keys_max(s_t))
        alpha = jnp.exp2(m - m_new)
        ps.append(jnp.exp2(s_t - m_new).astype(BF16))
        scaled.append((m_new, alpha * acc))
    return [(m, acc + _dot(v_t, p)) for (m, acc), v_t, p in zip(scaled, v_ts, ps)]


def _finish_pair(acc0, acc1):
    norm = lambda acc: acc[:HEAD_DIM, :] * (1.0 / acc[HEAD_DIM:HEAD_DIM + 1, :])
    return jnp.concatenate([norm(acc0), norm(acc1)], axis=0).T


def _rms_matmul_kernel(x_ref, g_ref, w_ref, o_ref, h_ref):
    @pl.when(pl.program_id(1) == 0)
    def _():
        h_ref[...] = _rms(x_ref[...], g_ref[...]).astype(BF16)

    o_ref[...] = _dot(h_ref[...], w_ref[...]).astype(o_ref.dtype)


def _rms_matmul(x2d, g, w, out_dtype, tm, tn):
    m, k = x2d.shape
    n = w.shape[1]
    w_mode = pl.Buffered(1) if tn == n else None
    return pl.pallas_call(
        _rms_matmul_kernel,
        grid=(m // tm, n // tn),
        in_specs=[pl.BlockSpec((tm, k), lambda i, j: (i, 0)),
                  pl.BlockSpec((1, k), lambda i, j: (0, 0)),
                  pl.BlockSpec((k, tn), lambda i, j: (0, j), pipeline_mode=w_mode)],
        out_specs=pl.BlockSpec((tm, tn), lambda i, j: (i, j)),
        out_shape=jax.ShapeDtypeStruct((m, n), out_dtype),
        scratch_shapes=[pltpu.VMEM((tm, k), BF16)],
        compiler_params=_cparams(2),
        name="rms_in_proj",
    )(x2d, g, w)


def _cum_kernel(fa_ref, b_ref, tri_ref, sel_ref, one_ref, qf_ref, kf_ref, c3_s):
    seq = fa_ref.shape[1]
    tri = tri_ref[...]
    lane = lax.broadcasted_iota(I32, (1, LANES), 1)
    carry = jnp.zeros((1, LANES), F32)
    for blk in range(seq // LANES):
        rows = slice(blk * LANES, (blk + 1) * LANES)
        x = fa_ref[0, rows, :] + b_ref[...]
        lf = jnp.minimum(x, 0.0) - jnp.log(1.0 + jnp.exp(-jnp.abs(x)))
        hi, mid, lo = _split3(lf)
        c = _dot(tri, hi) + _dot(tri, mid) + _dot(tri, lo) + carry
        carry = c[LANES - 1:LANES, :]
        hi, mid, lo = _split3(jnp.where(lane < A_HEADS, c * LOG2E, 0.0))
        c3_s[rows, :] = (hi.astype(F32) + pltpu.roll(mid.astype(F32), A_HEADS, 1)
                         + pltpu.roll(lo.astype(F32), 2 * A_HEADS, 1)).astype(BF16)
    half = seq // 2
    for h in range(A_HEADS):
        for rows in (slice(0, half), slice(half, seq)):
            feats = _dot(c3_s[rows, :], sel_ref[h]) + one_ref[...]
            qf_ref[0, h, rows, :] = feats[:, :LANES].astype(BF16)
            kf_ref[0, h, rows, :] = feats[:, LANES:].astype(BF16)


def _cum_forget(side, b_pad, consts, batch, seq):
    tri, sel, one = consts
    full = lambda a: pl.BlockSpec(a.shape, lambda b: (0,) * a.ndim)
    feat = jax.ShapeDtypeStruct((batch, A_HEADS, seq, LANES), BF16)
    return pl.pallas_call(
        _cum_kernel,
        grid=(batch,),
        in_specs=[pl.BlockSpec((1, seq, LANES), lambda b: (b, 0, SIDE_FA // LANES)),
                  full(b_pad), full(tri), full(sel), full(one)],
        out_specs=[pl.BlockSpec((1, A_HEADS, seq, LANES), lambda b: (b, 0, 0, 0))] * 2,
        out_shape=[feat, feat],
        scratch_shapes=[pltpu.VMEM((seq, LANES), BF16)],
        compiler_params=_cparams(1),
        name="fox_cumsum",
    )(side, b_pad, tri, sel, one)


def _fox_kernel(q_ref, k_ref, v_ref, z_ref, qf_ref, kf_ref, o_ref, vt_s, qa_s, m_s, acc_s):
    seq = q_ref.shape[1]
    t = FOX_T
    n_heads = A_HEADS
    lane = lax.broadcasted_iota(I32, (1, LANES), 1)
    key_le_query = (lax.broadcasted_iota(I32, (t, t), 0) <= lax.broadcasted_iota(I32, (t, t), 1))
    pair_cols = lambda h: slice((h // 2) * LANES, (h // 2 + 1) * LANES)

    def transpose_v(c, _):
        cols = pl.ds(pl.multiple_of(c * t, t), t)
        for p in range(n_heads // 2):
            pc = slice(p * LANES, (p + 1) * LANES)
            v_t = v_ref[0, cols, pc].astype(F32).T.astype(BF16)
            for e in range(2):
                vt_s[2 * p + e, :HEAD_DIM, cols] = v_t[e * HEAD_DIM:(e + 1) * HEAD_DIM, :]
                vt_s[2 * p + e, HEAD_DIM:, cols] = jnp.ones((ONES_ROWS, t), BF16)
        return 0

    lax.fori_loop(0, seq // t, transpose_v, 0)

    def q_block(i, _):
        rows = pl.ds(pl.multiple_of(i * t, t), t)
        for h in range(n_heads):
            half = (lane >= HEAD_DIM) if h % 2 else (lane < HEAD_DIM)
            q2 = q_ref[0, rows, pair_cols(h)]
            qa_s[h] = jnp.concatenate([jnp.where(half, q2, jnp.zeros_like(q2)), qf_ref[0, h, rows, :]], axis=1)
            m_s[h] = jnp.full((1, t), NEG, F32)
            acc_s[h] = jnp.zeros((HEAD_DIM + ONES_ROWS, t), F32)

        def chunk(kb, diagonal):
            cols = pl.ds(pl.multiple_of(kb * t, t), t)
            states = [(m_s[h], acc_s[h]) for h in range(n_heads)]
            s_ts = [_dot_nt(jnp.concatenate([k_ref[0, cols, pair_cols(h)], kf_ref[0, h, cols, :]], axis=1),
                            qa_s[h]) for h in range(n_heads)]
            if diagonal:
                s_ts = [jnp.where(key_le_query, s_t, NEG) for s_t in s_ts]
            v_ts = [vt_s[h, :, cols] for h in range(n_heads)]
            for h, st in enumerate(_online_update_heads(states, s_ts, v_ts)):
                m_s[h], acc_s[h] = st

        def off_diagonal(kb, _):
            chunk(kb, False)
            return 0

        lax.fori_loop(0, i, off_diagonal, 0)
        chunk(i, True)
        for p in range(n_heads // 2):
            h0, h1 = 2 * p, 2 * p + 1
            out = _finish_pair(acc_s[h0], acc_s[h1]) * _silu(z_ref[0, rows, pair_cols(h0)].astype(F32))
            o_ref[0, rows, pair_cols(h0)] = out.astype(o_ref.dtype)
        return 0

    lax.fori_loop(0, seq // t, q_block, 0)


def _fox_attention(main, qf, kf, batch, seq):
    t = FOX_T
    blk = lambda base: pl.BlockSpec((1, seq, A_W), lambda b, base=base: (b, 0, base // A_W))
    feat = pl.BlockSpec((1, A_HEADS, seq, LANES), lambda b: (b, 0, 0, 0))
    return pl.pallas_call(
        _fox_kernel,
        grid=(batch,),
        in_specs=[blk(MAIN_QA), blk(MAIN_KA), blk(MAIN_VA), blk(MAIN_ZA), feat, feat],
        out_specs=pl.BlockSpec((1, seq, A_W), lambda b: (b, 0, 0)),
        out_shape=jax.ShapeDtypeStruct((batch, seq, A_W), BF16),
        scratch_shapes=[pltpu.VMEM((A_HEADS, HEAD_DIM + ONES_ROWS, seq), BF16),
                        pltpu.VMEM((A_HEADS, t, 2 * LANES), BF16),
                        pltpu.VMEM((A_HEADS, 1, t), F32),
                        pltpu.VMEM((A_HEADS, HEAD_DIM + ONES_ROWS, t), F32)],
        compiler_params=_cparams(1),
        name="fox_attention",
    )(main, main, main, main, qf, kf)


def _lane_group(lane, first):
    rel = lane - first
    return jnp.where((rel >= 0) & (rel < 9), rel // 3, -1)


def _key_pos_features(pos, lane, first):
    hi = lax.shift_right_logical(pos, 6).astype(F32)
    lo = (pos & (POS_SPLIT - 1)).astype(F32)
    grp = _lane_group(lane, first)
    return jnp.where(grp == 0, hi, jnp.where(grp == 1, lo, jnp.where(grp == 2, 1.0, 0.0)))


def _query_pos_features(pos, lane, first, slope):
    grp = _lane_group(lane, first)
    x = jnp.where(grp == 0, POS_SPLIT * slope,
                  jnp.where(grp == 1, slope, jnp.where(grp == 2, -slope * pos.astype(F32), 0.0)))
    hi, mid, lo = _split3(x)
    piece = (lane - first) % 3
    return jnp.where(piece == 0, hi, jnp.where(piece == 1, mid, lo))


def _dsa_kernel(side_ref, z_ref, gcq_ref, wuq_ref, wuqi_ref, gckv_ref, wkv_ref, wvk_ref, wvt_ref,
                tile_ref, tri_ref, qfeat_ref, o_ref,
                qb_s, qi_s, kt_s, ka_s, kb_s, vt_s, wt_s, qm_s, qa_s, key_s, hi_s, lo_s, mb_s, m_s, acc_s,
                *, top_k):
    seq = side_ref.shape[1]
    t = ATT_T
    n_heads = B_HEADS
    lane = lax.broadcasted_iota(I32, (1, LANES), 1)
    lane2 = lax.broadcasted_iota(I32, (1, 2 * LANES), 1)
    key_le_query = (lax.broadcasted_iota(I32, (t, t), 0) <= lax.broadcasted_iota(I32, (t, t), 1))
    row_iota = lax.broadcasted_iota(I32, (t, 1), 0)

    def prep(c, _):
        r0 = pl.multiple_of(c * t, t)
        rows = pl.ds(r0, t)
        blk = side_ref[0, rows, :]
        cqn = _rms(blk[:, SIDE_CQ:SIDE_CQ + B_Q_RANK], gcq_ref[...]).astype(BF16)
        qb_s[rows, :] = _dot(cqn, wuq_ref[...]).astype(BF16)
        qi_s[rows, :] = _dot(cqn, wuqi_ref[...]).astype(BF16)
        kvn = _rms(blk[:, B_Q_RANK:SIDE_MISC], gckv_ref[...]).astype(BF16)
        pos = r0 + row_iota
        ka_s[rows, :] = jnp.where(lane < HEAD_DIM, _dot(kvn, wkv_ref[...]),
                                  _key_pos_features(pos, lane, HEAD_DIM)).astype(BF16)
        kb_s[rows, :] = jnp.where(lane >= HEAD_DIM, _dot(kvn, wvk_ref[...]),
                                  _key_pos_features(pos, lane, 0)).astype(BF16)
        vt_s[:HEAD_DIM, rows] = _dot_nt(wvt_ref[...], kvn).astype(BF16)
        vt_s[HEAD_DIM:, rows] = jnp.ones((ONES_ROWS, t), BF16)
        misc = blk[:, SIDE_MISC:]
        kt_s[rows, :] = _dot(misc.astype(BF16), tile_ref[...]).astype(BF16)
        wt_s[:, rows] = misc.T
        return 0

    lax.fori_loop(0, seq // t, prep, 0)

    def q_block(i, _):
        r0 = pl.multiple_of(i * t, t)
        rows = pl.ds(r0, t)
        nk = i + 1

        qi = qi_s[rows, :]
        for h in range(IDX_HEADS):
            in_head = (lane2 >= h * IDX_DIM) & (lane2 < (h + 1) * IDX_DIM)
            qm_s[h] = jnp.where(in_head, qi, jnp.zeros_like(qi))
        w_rows = [wt_s[SIDE_WIDX + h:SIDE_WIDX + h + 1, rows] for h in range(IDX_HEADS)]

        def score_chunk(kc, diagonal):
            c0 = pl.multiple_of(kc * t, t)
            kt = kt_s[pl.ds(c0, t), :]
            acc = jnp.zeros((t, t), F32)
            for h in range(IDX_HEADS):
                acc = acc + w_rows[h] * jnp.maximum(_dot_nt(kt, qm_s[h]), 0.0)
            bits = lax.bitcast_convert_type(acc, I32)
            key = bits ^ (lax.shift_right_arithmetic(bits, 31) & 0x7FFFFFFF)
            key = jnp.where(key == -1, 0, key)
            if diagonal:
                key = jnp.where(key_le_query, key, INT_MIN)
            key_s[pl.ds(c0, t), :] = key
            hi_s[pl.ds(c0, t), :] = lax.shift_right_arithmetic(key, 16).astype(I16)

        def score_quad(kq, _):
            for j in range(4):
                score_chunk(4 * kq + j, False)
            return 0

        lax.fori_loop(0, i // 4, score_quad, 0)

        @pl.when(i % 4 >= 2)
        def _():
            score_chunk((i // 4) * 4, False)
            score_chunk((i // 4) * 4 + 1, False)

        @pl.when(i % 2 == 1)
        def _():
            score_chunk(i - 1, False)

        score_chunk(i, True)

        def count(pred):
            def body(kc, acc):
                c0 = pl.multiple_of(kc * t, t)
                return acc + _tree_fold(jnp.where(pred(key_s[pl.ds(c0, t), :]), 1.0, 0.0), jnp.add)
            acc = lax.fori_loop(0, nk, body, jnp.zeros((SUBLANES, t), F32))
            return jnp.sum(acc, axis=0, keepdims=True)

        def count16(src, cand):
            def body(kc, acc):
                c0 = pl.multiple_of(kc * t, t)
                hit = jnp.where(src[pl.ds(c0, t), :] >= cand, jnp.ones((), BF16), jnp.zeros((), BF16))
                return acc + _tree_fold(hit, jnp.add, PACKED_ROWS)
            acc = lax.fori_loop(0, nk, body, jnp.zeros((PACKED_ROWS, t), BF16))
            return jnp.sum(acc.astype(F32), axis=0, keepdims=True)

        def search16(src):
            def step(p, thr):
                cand = thr + lax.shift_left(jnp.int32(1), 15 - p)
                return jnp.where(count16(src, cand.astype(I16)) >= top_k, cand, thr)
            return lax.fori_loop(0, 16, step, jnp.full((1, t), INT16_MIN, I32))

        t_hi = search16(hi_s)

        def low_body(kc, _):
            c0 = pl.multiple_of(kc * t, t)
            key = key_s[pl.ds(c0, t), :]
            hi = lax.shift_right_arithmetic(key, 16)
            lo = (key & 0xFFFF) + INT16_MIN
            lo_s[pl.ds(c0, t), :] = jnp.where(hi > t_hi, INT16_MAX, jnp.where(hi == t_hi, lo, INT16_MIN)).astype(I16)
            return 0

        lax.fori_loop(0, nk, low_body, 0)
        t_lo = search16(lo_s)
        thr = lax.shift_left(t_hi, 16) | ((t_lo - INT16_MIN) & 0xFFFF)
        thr = jnp.maximum(thr, INT_MIN + 1)

        def mask_body(kc, acc):
            c0 = pl.multiple_of(kc * t, t)
            admitted = key_s[pl.ds(c0, t), :] >= thr
            mb_s[pl.ds(c0, t), :] = jnp.where(admitted, 0.0, NEG)
            return acc + _tree_fold(jnp.where(admitted, 1.0, 0.0), jnp.add)

        n_ge = jnp.sum(lax.fori_loop(0, nk, mask_body, jnp.zeros((SUBLANES, t), F32)), axis=0, keepdims=True)

        @pl.when(jnp.max(n_ge) > top_k)
        def _():
            room = top_k - count(lambda k: k > thr)

            def tie_body(kc, seen):
                c0 = pl.multiple_of(kc * t, t)
                tie = key_s[pl.ds(c0, t), :] == thr
                rank = seen + _dot(tri_ref[...], jnp.where(tie, 1.0, 0.0).astype(BF16))
                drop = tie & (rank > room)
                mb_s[pl.ds(c0, t), :] = jnp.where(drop, NEG, mb_s[pl.ds(c0, t), :])
                return rank[t - 1:t, :]

            lax.fori_loop(0, nk, tie_body, jnp.zeros((1, t), F32))

        for h in range(n_heads):
            half = (lane >= HEAD_DIM) if h % 2 else (lane < HEAD_DIM)
            q2 = qb_s[rows, (h // 2) * LANES:(h // 2 + 1) * LANES]
            qa_s[h] = jnp.where(half, q2, qfeat_ref[h, rows, :])
            m_s[h] = jnp.full((1, t), NEG, F32)
            acc_s[h] = jnp.zeros((HEAD_DIM + ONES_ROWS, t), F32)

        def attend(kc, _):
            c0 = pl.multiple_of(kc * t, t)
            cols = pl.ds(c0, t)
            states = [(m_s[h], acc_s[h]) for h in range(n_heads)]
            mask = mb_s[cols, :]
            k_ops = (ka_s[cols, :], kb_s[cols, :])
            s_ts = [_dot_nt(k_ops[h % 2], qa_s[h]) + mask for h in range(n_heads)]
            v_list = [vt_s[:, cols]] * n_heads
            for h, st in enumerate(_online_update_heads(states, s_ts, v_list)):
                m_s[h], acc_s[h] = st
            return 0

        def attend_pair(kp, _):
            attend(2 * kp, 0)
            attend(2 * kp + 1, 0)
            return 0

        lax.fori_loop(0, nk // 2, attend_pair, 0)

        @pl.when(nk % 2 == 1)
        def _():
            attend(nk - 1, 0)

        for hp in range(n_heads // 2):
            cols = slice(hp * LANES, (hp + 1) * LANES)
            out = _finish_pair(acc_s[2 * hp], acc_s[2 * hp + 1]) * _silu(z_ref[0, rows, cols].astype(F32))
            o_ref[0, rows, cols] = out.astype(o_ref.dtype)
        return 0

    lax.fori_loop(0, seq // t, q_block, 0)


def _dsa_attention(side, main, w, tile, tri, batch, seq):
    assert seq // POS_SPLIT <= 256
    assert seq // PACKED_ROWS <= 256
    top_k = min(IDX_TOPK_MAX, seq // 4)
    slopes = tuple(float(2.0 ** (-8.0 * (h + 1) / B_HEADS)) * LOG2E for h in range(B_HEADS))
    t = ATT_T
    const = lambda a: pl.BlockSpec(a.shape, lambda b: (0,) * a.ndim)
    pos = jnp.arange(seq, dtype=I32).reshape(seq, 1)
    lane = jnp.arange(LANES, dtype=I32).reshape(1, LANES)
    qfeat = jnp.stack([_query_pos_features(pos, lane, 0 if h % 2 else HEAD_DIM, slopes[h])
                       for h in range(B_HEADS)]).astype(BF16)
    weights = [w["g_cq"], w["w_uq"], w["w_uq_idx"], w["g_ckv"], w["w_kv"], w["w_vk"], w["w_v_t"], tile, tri, qfeat]
    return pl.pallas_call(
        functools.partial(_dsa_kernel, top_k=top_k),
        grid=(batch,),
        in_specs=[pl.BlockSpec((1, seq, 512), lambda b: (b, 0, 0)),
                  pl.BlockSpec((1, seq, B_W), lambda b: (b, 0, MAIN_ZB // B_W))]
                 + [const(a) for a in weights],
        out_specs=pl.BlockSpec((1, seq, B_W), lambda b: (b, 0, 0)),
        out_shape=jax.ShapeDtypeStruct((batch, seq, B_W), BF16),
        scratch_shapes=[pltpu.VMEM((seq, B_W), BF16),
                        pltpu.VMEM((seq, IDX_HEADS * IDX_DIM), BF16),
                        pltpu.VMEM((seq, IDX_HEADS * IDX_DIM), BF16),
                        pltpu.VMEM((seq, LANES), BF16),
                        pltpu.VMEM((seq, LANES), BF16),
                        pltpu.VMEM((HEAD_DIM + ONES_ROWS, seq), BF16),
                        pltpu.VMEM((LANES, seq), F32),
                        pltpu.VMEM((IDX_HEADS, t, IDX_HEADS * IDX_DIM), BF16),
                        pltpu.VMEM((B_HEADS, t, LANES), BF16),
                        pltpu.VMEM((seq, t), I32),
                        pltpu.VMEM((seq, t), I16),
                        pltpu.VMEM((seq, t), I16),
                        pltpu.VMEM((seq, t), F32),
                        pltpu.VMEM((B_HEADS, 1, t), F32),
                        pltpu.VMEM((B_HEADS, HEAD_DIM + ONES_ROWS, t), F32)],
        compiler_params=_cparams(1),
        name="dsa_attention",
    )(side, main, *weights)


C_UNROLL = 8


def _dilated_kernel(q_ref, k_ref, v_ref, bias_ref, z0_ref, z1_ref, z2_ref, o_ref, o_s, l_s):
    seq = q_ref.shape[1]
    sp = pl.program_id(1)
    grp = pl.program_id(2)
    lane = lax.broadcasted_iota(I32, (1, LANES), 1)
    cw = CLASS_WINDOW

    def rows(start, d):
        return pl.ds(start, cw) if d == 1 else pl.ds(start, cw, stride=d)

    def group_body(g, d):
        per_class = seq // (d * cw)

        ld = lambda ref, s0: ref[0, rows(s0, d), :].astype(BF16)

        def load_block(n, u, before):
            r = n // per_class
            ib = n % per_class
            start = r + ib * (cw * d)
            b = dict(start=start, q=ld(q_ref, start), kc=ld(k_ref, start), vc=ld(v_ref, start), prev=None)
            if per_class <= C_UNROLL:
                if u % per_class:
                    b.update(prev="static", kp=before["kc"], vp=before["vc"])
            elif u:
                b.update(prev="static", kp=before["kc"], vp=before["vc"])
            else:
                pstart = r + jnp.maximum(ib - 1, 0) * (cw * d)
                b.update(prev="dynamic", kp=ld(k_ref, pstart), vp=ld(v_ref, pstart), has_prev=ib > 0)
            return b

        def blk(it, _):
            blocks = []
            for u in range(C_UNROLL):
                blocks.append(load_block(it * C_UNROLL + u, u, blocks[-1] if blocks else None))
            logits = []
            for b in blocks:
                for e in range(2):
                    half = (lane >= HEAD_DIM) if e else (lane < HEAD_DIM)
                    qm = jnp.where(half, b["q"], jnp.zeros_like(b["q"]))
                    s_cur = _dot_nt(qm, b["kc"]) + bias_ref[0, e, :, cw:]
                    s_prev = None
                    if b["prev"] == "static":
                        s_prev = _dot_nt(qm, b["kp"]) + bias_ref[0, e, :, :cw]
                    elif b["prev"] == "dynamic":
                        s_prev = _dot_nt(qm, b["kp"]) + jnp.where(b["has_prev"], bias_ref[0, e, :, :cw], NEG)
                    logits.append((s_cur, s_prev))
            probs = []
            for s_cur, s_prev in logits:
                if s_prev is None:
                    m = jnp.max(s_cur, axis=-1, keepdims=True)
                    p_cur, p_prev = jnp.exp(s_cur - m), None
                    l = jnp.sum(p_cur, axis=-1, keepdims=True)
                else:
                    m = jnp.max(jnp.maximum(s_cur, s_prev), axis=-1, keepdims=True)
                    p_cur, p_prev = jnp.exp(s_cur - m), jnp.exp(s_prev - m)
                    l = jnp.sum(p_cur + p_prev, axis=-1, keepdims=True)
                    p_prev = p_prev.astype(BF16)
                probs.append((p_cur.astype(BF16), p_prev, 1.0 / l, m + jnp.log(l)))
            for u, b in enumerate(blocks):
                outs = []
                for e in range(2):
                    p_cur, p_prev, inv_l, _ = probs[2 * u + e]
                    o = _dot(p_cur, b["vc"])
                    if p_prev is not None:
                        o = o + _dot(p_prev, b["vp"])
                    outs.append(o * inv_l)
                o_s[g, rows(b["start"], d), :] = jnp.where(lane < HEAD_DIM, outs[0], outs[1])
                l_s[g, rows(b["start"], d), :] = jnp.where(lane < HEAD_DIM, probs[2 * u][3], probs[2 * u + 1][3])
            return 0

        lax.fori_loop(0, seq // (cw * C_UNROLL), blk, 0)

    for g, (_, d) in enumerate(C_GROUPS):
        pl.when(grp == g)(functools.partial(group_body, g, d))

    n_grp = len(C_GROUPS)
    z_refs = (z0_ref, z1_ref, z2_ref)

    def combine(sp_static):
        def chunk(c, _):
            r0 = pl.multiple_of(c * 256, 256)
            ls = [l_s[g, pl.ds(r0, 256), :] for g in range(n_grp)]
            m = functools.reduce(jnp.maximum, ls)
            ws = [jnp.exp(x - m) for x in ls]
            inv = 1.0 / functools.reduce(lambda a, b: a + b, ws)
            for g in range(n_grp):
                col = (g * 2 + sp_static) * LANES
                y = (ws[g] * inv) * o_s[g, pl.ds(r0, 256), :]
                y = y * _silu(z_refs[g][0, pl.ds(r0, 256), :].astype(F32))
                o_ref[0, pl.ds(r0, 256), col:col + LANES] = y.astype(o_ref.dtype)
            return 0

        lax.fori_loop(0, seq // 256, chunk, 0)

    for s in range(2):
        pl.when((grp == n_grp - 1) & (sp == s))(functools.partial(combine, s))


def _dilated_attention(side, main, bias, batch, seq):
    assert seq % (C_GROUPS[-1][1] * CLASS_WINDOW) == 0
    assert (seq // CLASS_WINDOW) % C_UNROLL == 0
    n_grp = len(C_GROUPS)
    blk = lambda base: pl.BlockSpec(
        (1, seq, LANES), lambda b, s, g, base=base: (b, 0, base // LANES + 2 * g + s))
    zblk = lambda g: pl.BlockSpec(
        (1, seq, LANES), lambda b, s, _, g=g: (b, 0, MAIN_ZC // LANES + 2 * g + s))
    return pl.pallas_call(
        _dilated_kernel,
        grid=(batch, 2, n_grp),
        in_specs=[blk(SIDE_QC), blk(SIDE_KC), blk(SIDE_VC),
                  pl.BlockSpec((1, 2, CLASS_WINDOW, 2 * CLASS_WINDOW), lambda b, s, g: (2 * g + s, 0, 0, 0)),
                  zblk(0), zblk(1), zblk(2)],
        out_specs=pl.BlockSpec((1, seq, C_W), lambda b, s, g: (b, 0, 0)),
        out_shape=jax.ShapeDtypeStruct((batch, seq, C_W), BF16),
        scratch_shapes=[pltpu.VMEM((n_grp, seq, LANES), F32),
                        pltpu.VMEM((n_grp, seq, LANES), F32)],
        compiler_params=_cparams(3),
        name="dilated_attention",
    )(side, side, side, bias, main, main, main)


def _dilated_bias():
    cw = CLASS_WINDOW
    slopes = np.exp2(-8.0 * (np.arange(C_HEADS, dtype=np.float32) + 1.0) / C_HEADS).astype(np.float32)
    i = np.arange(cw)[:, None]
    j = np.arange(cw)[None, :]
    out = np.empty((C_HEADS // 2, 2, cw, 2 * cw), np.float32)
    for h in range(C_HEADS):
        d = C_GROUPS[h // C_HEADS_PER_GROUP][1]
        prev = np.where(j >= i, -slopes[h] * np.float32(d) * (cw + i - j).astype(np.float32), NEG)
        cur = np.where(j <= i, -slopes[h] * np.float32(d) * (i - j).astype(np.float32), NEG)
        out[h // 2, h % 2] = np.concatenate([prev, cur], axis=1)
    return jnp.asarray(out)


def _merge_kernel(x_ref, ya_ref, yb_ref, yc_ref, g_ref, wa_ref, wb_ref, wc_ref, wo_ref, fg_ref, o_ref,
                  *, final):
    gate = lambda n: _sigmoid(g_ref[:, n * D_MODEL:(n + 1) * D_MODEL].astype(F32))
    merged = (gate(0) * _dot(ya_ref[...], wa_ref[...])
              + gate(1) * _dot(yb_ref[...], wb_ref[...])
              + gate(2) * _dot(yc_ref[...], wc_ref[...]))
    y = x_ref[...] + _dot(merged.astype(BF16), wo_ref[...])
    if final:
        y = _rms(y, fg_ref[...])
    o_ref[...] = y


def _merge(x2d, ya, yb, yc, main, w_a, w_b, w_c, w_o, final_g, final, tm=1024):
    m = x2d.shape[0]
    row = lambda w: pl.BlockSpec((tm, w), lambda i: (i, 0))
    const = lambda a: pl.BlockSpec(a.shape, lambda i: (0, 0), pipeline_mode=pl.Buffered(1))
    return pl.pallas_call(
        functools.partial(_merge_kernel, final=final),
        grid=(m // tm,),
        in_specs=[row(D_MODEL), row(A_W), row(B_W), row(C_W), row(3 * D_MODEL),
                  const(w_a), const(w_b), const(w_c), const(w_o), const(final_g)],
        out_specs=row(D_MODEL),
        out_shape=jax.ShapeDtypeStruct((m, D_MODEL), F32),
        compiler_params=_cparams(1),
        name="gated_merge",
    )(x2d, ya, yb, yc, main, w_a, w_b, w_c, w_o, final_g)


def _layer_weights(norm_g, w_in, b_forget, g_cq, w_uq, w_uq_idx, g_ckv, w_ukv, w_a, w_b, w_c, w_o):
    points = np.cumsum(IN_WIDTHS)[:-1].tolist()
    (wqa, wka, wva, wfa, wza, wcq, wckv, wkidx, wwidx, wzb,
     wqc, wkc, wvc, wzc, wga, wgb, wgc) = jnp.split(w_in, points, axis=1)
    zeros = lambda n: jnp.zeros((D_MODEL, n), F32)
    w_main = jnp.concatenate([wga, wgb, wgc, wzb, wqa * (SCALE * LOG2E), wka, wva, wza, wzc], axis=1)
    w_side = jnp.concatenate(
        [wcq, wckv, wkidx, wwidx, zeros(LANES - IDX_DIM - IDX_HEADS),
         wfa, zeros(LANES - A_HEADS), wqc * SCALE, wkc, wvc, zeros(SIDE_N - SIDE_VC - C_W)], axis=1)
    assert w_main.shape[1] == MAIN_N and w_side.shape[1] == SIDE_N
    wk, wv = w_ukv[:, :HEAD_DIM], w_ukv[:, HEAD_DIM:]
    w_kv = jnp.concatenate([wk, wv], axis=1).astype(BF16)
    w_vk = jnp.concatenate([wv, wk], axis=1).astype(BF16)
    return dict(
        norm_g=norm_g.reshape(1, D_MODEL),
        w_main=w_main.astype(BF16), w_side=w_side.astype(BF16),
        b_pad=jnp.zeros((1, LANES), F32).at[0, :A_HEADS].set(b_forget),
        g_cq=g_cq.reshape(1, B_Q_RANK), w_uq=(w_uq * (SCALE * LOG2E)).astype(BF16), w_uq_idx=w_uq_idx.astype(BF16),
        g_ckv=g_ckv.reshape(1, B_KV_RANK), w_kv=w_kv, w_vk=w_vk, w_v_t=wv.T.astype(BF16),
        w_a=w_a.astype(BF16), w_b=w_b.astype(BF16), w_c=w_c.astype(BF16), w_o=w_o.astype(BF16))


def _constants():
    r = np.arange(LANES)
    tri128 = (r[None, :] <= r[:, None]).astype(np.float32)
    r2 = np.arange(ATT_T)
    tri_t = (r2[None, :] <= r2[:, None]).astype(np.float32)
    c = np.arange(IDX_HEADS * IDX_DIM)
    tile = ((r[:, None] == c[None, :] % IDX_DIM) & (r[:, None] < IDX_DIM)).astype(np.float32)
    sel = np.zeros((A_HEADS, LANES, 2 * LANES), np.float32)
    for h in range(A_HEADS):
        for piece in range(3):
            sel[h, piece * A_HEADS + h, piece] = 1.0
            sel[h, piece * A_HEADS + h, LANES + 3 + piece] = -1.0
    one = np.zeros((1, 2 * LANES), np.float32)
    one[0, 3:6] = 1.0
    one[0, LANES:LANES + 3] = 1.0
    bf = lambda a: jnp.asarray(a, BF16)
    cum = (bf(tri128), bf(sel), jnp.asarray(one))
    return cum, bf(tri_t), bf(tile), _dilated_bias()


def _hybrid_layer(x2d, batch, seq, w, consts, final_g, final):
    cum_consts, tri_t, tile, bias_c = consts
    main = _rms_matmul(x2d, w["norm_g"], w["w_main"], BF16, tm=1024, tn=MAIN_N)
    side = _rms_matmul(x2d, w["norm_g"], w["w_side"], F32, tm=1024, tn=SIDE_N)
    main3 = main.reshape(batch, seq, MAIN_N)
    side3 = side.reshape(batch, seq, SIDE_N)
    qf, kf = _cum_forget(side3, w["b_pad"], cum_consts, batch, seq)
    ya = _fox_attention(main3, qf, kf, batch, seq)
    yb = _dsa_attention(side3, main3, w, tile, tri_t, batch, seq)
    yc = _dilated_attention(side3, main3, bias_c, batch, seq)
    m = batch * seq
    return _merge(x2d, ya.reshape(m, A_W), yb.reshape(m, B_W), yc.reshape(m, C_W), main,
                  w["w_a"], w["w_b"], w["w_c"], w["w_o"], final_g, final)


def kernel(x, norm_g, w_in, b_forget, g_cq, w_uq, w_uq_idx, g_ckv, w_ukv, w_a, w_b, w_c, w_o, final_g):
    batch, seq, d_model = x.shape
    assert d_model == D_MODEL and seq % ATT_T == 0 and seq % FOX_T == 0
    depth = norm_g.shape[0]
    consts = _constants()
    fg = final_g.reshape(1, D_MODEL)
    x2d = x.reshape(batch * seq, D_MODEL)
    for l in range(depth):
        w = _layer_weights(norm_g[l], w_in[l], b_forget[l], g_cq[l], w_uq[l], w_uq_idx[l], g_ckv[l],
                           w_ukv[l], w_a[l], w_b[l], w_c[l], w_o[l])
        x2d = _hybrid_layer(x2d, batch, seq, w, consts, fg, final=(l == depth - 1))
    return x2d.reshape(batch, seq, D_MODEL)
```
